```python
import jax, jax.numpy as jnp
from jax import lax
import numpy as np

D_MODEL = 1024
BATCH = 4
SEQ = 4096
DEPTH = 1

CTX_LEN = 256
GRID_W = 64
D_MIX = D_MODEL
HEAD_DIM = 64
ATT_HEADS = 8
ATT_KV_HEADS = 2
GQA_GROUP = ATT_HEADS // ATT_KV_HEADS
D_ATT = ATT_HEADS * HEAD_DIM
D_ATT_KV = ATT_KV_HEADS * HEAD_DIM
ATT_SCALE = HEAD_DIM ** -0.5
ROPE_THETA = 10000.0
ROPE_PAIRS = HEAD_DIM // 4
Q_BLOCK = 128
RWKV_HEADS = 8
RWKV_HEAD = 64
D_RWKV = RWKV_HEADS * RWKV_HEAD
D_DECAY_LORA = 64
D_AAA_LORA = 64
D_GATE_LORA = 128
D_RWKV_IN = 3 * D_RWKV + D_DECAY_LORA + D_AAA_LORA + D_GATE_LORA
D_IN = D_ATT + 2 * D_ATT_KV + D_RWKV_IN
ATT_CUTS = (D_ATT, D_ATT + D_ATT_KV, D_ATT + 2 * D_ATT_KV)
RWKV_CUTS = (D_RWKV, 2 * D_RWKV, 3 * D_RWKV, 3 * D_RWKV + D_DECAY_LORA,
             3 * D_RWKV + D_DECAY_LORA + D_AAA_LORA)
N_EXPERTS = 16
CAPACITY_FACTOR = 2
D_EXPERT = D_MODEL
N_MOD = 6
LN_EPS = 1e-5
RMS_EPS = 1e-6
GN_EPS = 64e-5
L2_EPS = 1e-12
ALPHA = (2.0 * DEPTH) ** 0.25
BETA = (8.0 * DEPTH) ** -0.25

kernel_name = 'hybrid_gqa_rwkv7_ec_dit_layer'


def layer_norm(x):
    xf = x.astype(jnp.float32)
    mu = jnp.mean(xf, -1, keepdims=True)
    var = jnp.mean(jnp.square(xf - mu), -1, keepdims=True)
    return ((xf - mu) * lax.rsqrt(var + LN_EPS)).astype(x.dtype)


def layer_norm_affine(x, g, b):
    return layer_norm(x) * g + b


def modulate(h, shift, scale):
    return h * (1.0 + scale) + shift


def to_heads(t, n):
    return t.reshape(t.shape[:-1] + (n, t.shape[-1] // n))


def head_rms_norm(t, gain):
    tf = t.astype(jnp.float32)
    return (tf * lax.rsqrt(jnp.mean(tf * tf, -1, keepdims=True) + RMS_EPS) * gain).astype(t.dtype)


def axial_rope_tables(n_tokens):
    rows = n_tokens // GRID_W
    row = jnp.repeat(jnp.arange(rows), GRID_W)
    col = jnp.tile(jnp.arange(GRID_W), rows)
    inv = ROPE_THETA ** (-jnp.arange(ROPE_PAIRS, dtype=jnp.float32) / ROPE_PAIRS)
    ang = jnp.stack([row, col], -1).astype(jnp.float32)[:, :, None] * inv
    return jnp.cos(ang), jnp.sin(ang)


def apply_axial_rope(t, cos, sin):
    B, L, H, _ = t.shape
    tf = t.astype(jnp.float32).reshape(B, L, H, 2, 2, ROPE_PAIRS)
    t1, t2 = tf[..., 0, :], tf[..., 1, :]
    cs, sn = cos[None, :, None], sin[None, :, None]
    out = jnp.stack([t1 * cs - t2 * sn, t2 * cs + t1 * sn], axis=-2)
    return out.reshape(B, L, H, HEAD_DIM).astype(t.dtype)


def gqa_attend(q, k, v):
    B, Tq = q.shape[:2]
    qg = q.reshape(B, Tq, ATT_KV_HEADS, GQA_GROUP, HEAD_DIM)
    s = jnp.einsum('bqkgd,bskd->bkgqs', qg, k).astype(jnp.float32) * ATT_SCALE
    p = jax.nn.softmax(s, axis=-1).astype(v.dtype)
    o = jnp.einsum('bkgqs,bskd->bqkgd', p, v)
    return o.reshape(B, Tq, D_ATT)


def token_shift(p, mu):
    pad = jnp.pad(p, ((0, 0), (1, 1), (0, 0)))
    return p + (0.5 * (pad[:, :-2] + pad[:, 2:]) - p) * mu


def rwkv_inputs(rw, decay_w0, decay_up, iclr_a0, iclr_up, gate_up, k_k, k_a):
    f32 = jnp.float32
    r, k, v, wl, al, gl = jnp.split(rw, RWKV_CUTS, axis=-1)
    g = jax.nn.sigmoid(gl) @ gate_up
    kk = to_heads(k * k_k, RWKV_HEADS).astype(f32)
    kk = kk / jnp.maximum(jnp.sqrt(jnp.sum(kk * kk, -1, keepdims=True)), L2_EPS)
    wt = jnp.tanh(wl)
    dirs = []
    for d in range(2):
        w = -jax.nn.softplus(-(decay_w0[d] + wt @ decay_up[d])) - 0.5
        decay = jnp.exp(-jnp.exp(to_heads(w, RWKV_HEADS).astype(f32)))
        a = jax.nn.sigmoid(iclr_a0[d] + al @ iclr_up[d])
        kd = to_heads(k * (1.0 + (a - 1.0) * k_a), RWKV_HEADS).astype(f32)
        a_h = to_heads(a, RWKV_HEADS).astype(f32)
        dirs.append((decay, kd, -kk, kk * a_h))
    r_h = to_heads(r, RWKV_HEADS).astype(f32)
    k_h = to_heads(k, RWKV_HEADS).astype(f32)
    v_h = to_heads(v, RWKV_HEADS).astype(f32)
    return r_h, k_h, v_h, g, dirs


def wkv7_scan(r, decay, k, v, a, b, s0, reverse, return_y):
    xs = tuple(jnp.moveaxis(t, 1, 0) for t in (r, decay, k, v, a, b))

    def step(s, inp):
        rt, wt, kt, vt, at, bt = inp
        sa = jnp.einsum('bhvk,bhk->bhv', s, at)
        s = s * wt[:, :, None, :] + sa[..., None] * bt[:, :, None, :] + vt[..., None] * kt[:, :, None, :]
        y = jnp.einsum('bhvk,bhk->bhv', s, rt) if return_y else None
        return s, y

    s_fin, ys = lax.scan(step, s0, xs, reverse=reverse)
    return s_fin, (jnp.moveaxis(ys, 0, 1) if return_y else None)


def rwkv_output(y, r, k, v, g, r_k, lnx_g, lnx_b):
    B, T = y.shape[:2]
    mu = jnp.mean(y, -1, keepdims=True)
    var = jnp.mean(jnp.square(y - mu), -1, keepdims=True)
    yn = ((y - mu) * lax.rsqrt(var + GN_EPS)).reshape(B, T, D_RWKV) * lnx_g + lnx_b
    bonus = (jnp.sum(r * k * r_k, -1, keepdims=True) * v).reshape(B, T, D_RWKV)
    return (yn + bonus) * g


def token_mixer(h_lat, h_ctx, w_in, q_gain, k_gain, tshift_mu, decay_w0, decay_up, iclr_a0, iclr_up,
                gate_up, k_k, k_a, r_k, lnx_g, lnx_b, w_out, with_ctx_out):
    B, L, _ = h_lat.shape
    q_l, k_l, v_l, rw_l = jnp.split(h_lat @ w_in, ATT_CUTS, axis=-1)
    q_c, k_c, v_c, rw_c = jnp.split(h_ctx @ w_in, ATT_CUTS, axis=-1)

    cos, sin = axial_rope_tables(L)
    q_l = apply_axial_rope(head_rms_norm(to_heads(q_l, ATT_HEADS), q_gain), cos, sin)
    k_l = apply_axial_rope(head_rms_norm(to_heads(k_l, ATT_KV_HEADS), k_gain), cos, sin)
    k_c = head_rms_norm(to_heads(k_c, ATT_KV_HEADS), k_gain)
    v_l = to_heads(v_l, ATT_KV_HEADS)
    v_c = to_heads(v_c, ATT_KV_HEADS)
    k_all = jnp.concatenate([k_l, k_c], axis=1)
    v_all = jnp.concatenate([v_l, v_c], axis=1)
    n_blk = L // Q_BLOCK
    q_blocks = jnp.moveaxis(q_l.reshape(B, n_blk, Q_BLOCK, ATT_HEADS, HEAD_DIM), 1, 0)
    att_l = lax.map(lambda qb: gqa_attend(qb, k_all, v_all), q_blocks)
    att_l = jnp.moveaxis(att_l, 0, 1).reshape(B, L, D_ATT)

    rwkv_p = (decay_w0, decay_up, iclr_a0, iclr_up, gate_up, k_k, k_a)
    r_c, kb_c, vr_c, g_c, dirs_c = rwkv_inputs(token_shift(rw_c, tshift_mu), *rwkv_p)
    r_l, kb_l, vr_l, g_l, dirs_l = rwkv_inputs(token_shift(rw_l, tshift_mu), *rwkv_p)
    s0 = jnp.zeros((h_ctx.shape[0], RWKV_HEADS, RWKV_HEAD, RWKV_HEAD), jnp.float32)
    y_l, y_c = None, None
    for d, reverse in enumerate((False, True)):
        dc, kc, ac, bc = dirs_c[d]
        s_c, yc = wkv7_scan(r_c, dc, kc, vr_c, ac, bc, s0, reverse, with_ctx_out)
        dl, kl, al, bl = dirs_l[d]
        _, yl = wkv7_scan(r_l, dl, kl, vr_l, al, bl, s_c, reverse, True)
        y_l = yl if y_l is None else y_l + yl
        if with_ctx_out:
            y_c = yc if y_c is None else y_c + yc
    rw_out_l = rwkv_output(y_l, r_l, kb_l, vr_l, g_l, r_k, lnx_g, lnx_b).astype(h_lat.dtype)
    out_l = jnp.concatenate([att_l, rw_out_l], axis=-1) @ w_out
    if not with_ctx_out:
        return out_l, None
    q_c = head_rms_norm(to_heads(q_c, ATT_HEADS), q_gain)
    att_c = gqa_attend(q_c, k_c, v_c)
    rw_out_c = rwkv_output(y_c, r_c, kb_c, vr_c, g_c, r_k, lnx_g, lnx_b).astype(h_ctx.dtype)
    out_c = jnp.concatenate([att_c, rw_out_c], axis=-1) @ w_out
    return out_l, out_c


def expert_choice_ffn(h, router_w, w_gate, w_up, w_down):
    B, T, D = h.shape
    cap = CAPACITY_FACTOR * T // N_EXPERTS
    logits = jnp.einsum('btd,de->bte', h, router_w).astype(jnp.float32)
    aff = jax.nn.softmax(logits, axis=-1)
    gate, idx = lax.top_k(jnp.swapaxes(aff, 1, 2), cap)
    xin = jax.vmap(lambda hb, ib: hb[ib])(h, idx)
    hid = jax.nn.silu(jnp.einsum('becd,edf->becf', xin, w_gate)) * jnp.einsum('becd,edf->becf', xin, w_up)
    y = jnp.einsum('becf,efd->becd', hid, w_down) * gate[..., None].astype(h.dtype)
    return jax.vmap(lambda ib, yb: jnp.zeros((T, D), h.dtype).at[ib.reshape(-1)].add(yb.reshape(-1, D)))(idx, y)


def setup_inputs(seed: int = 0) -> dict:
    key = jax.random.key(seed)
    ks = jax.random.split(key, 32)
    f32 = jnp.float32
    nl = DEPTH

    def nrm(k, shape, s):
        return jax.random.normal(k, shape, f32) * s

    return {
        'x': nrm(ks[0], (BATCH, SEQ, D_MODEL), 1.0),
        'c': nrm(ks[1], (BATCH, D_MODEL), 1.0),
        'ctx': nrm(ks[2], (BATCH, CTX_LEN, D_MODEL), 1.0),
        'c_ctx': nrm(ks[3], (D_MODEL,), 1.0),
        'w_ada': nrm(ks[4], (nl, D_MODEL, N_MOD * D_MODEL), 0.02),
        'b_ada': nrm(ks[5], (nl, N_MOD * D_MODEL), 0.02),
        'w_in': nrm(ks[6], (nl, D_MODEL, D_IN), D_MODEL ** -0.5),
        'q_gain': 1.0 + nrm(ks[7], (nl, HEAD_DIM), 0.1),
        'k_gain': 1.0 + nrm(ks[8], (nl, HEAD_DIM), 0.1),
        'tshift_mu': jax.random.uniform(ks[9], (nl, D_RWKV_IN), f32),
        'decay_w0': jax.random.uniform(ks[10], (nl, 2, D_RWKV), f32, -6.0, -1.0),
        'decay_up': nrm(ks[11], (nl, 2, D_DECAY_LORA, D_RWKV), 0.1 * D_DECAY_LORA ** -0.5),
        'iclr_a0': nrm(ks[12], (nl, 2, D_RWKV), 0.5),
        'iclr_up': nrm(ks[13], (nl, 2, D_AAA_LORA, D_RWKV), D_AAA_LORA ** -0.5),
        'gate_up': nrm(ks[14], (nl, D_GATE_LORA, D_RWKV), D_GATE_LORA ** -0.5),
        'k_k': 0.85 + nrm(ks[15], (nl, D_RWKV), 0.1),
        'k_a': 1.0 + nrm(ks[16], (nl, D_RWKV), 0.1),
        'r_k': nrm(ks[17], (nl, RWKV_HEADS, RWKV_HEAD), 0.1),
        'lnx_g': 1.0 + nrm(ks[18], (nl, D_RWKV), 0.1),
        'lnx_b': nrm(ks[19], (nl, D_RWKV), 0.02),
        'w_out': nrm(ks[20], (nl, D_MIX, D_MODEL), BETA * D_MIX ** -0.5),
        'ln1_g': 1.0 + nrm(ks[21], (nl, D_MODEL), 0.1),
        'ln1_b': nrm(ks[22], (nl, D_MODEL), 0.02),
        'router_w': nrm(ks[23], (nl, D_MODEL, N_EXPERTS), D_MODEL ** -0.5),
        'exp_w_gate': nrm(ks[24], (nl, N_EXPERTS, D_MODEL, D_EXPERT), D_MODEL ** -0.5),
        'exp_w_up': nrm(ks[25], (nl, N_EXPERTS, D_MODEL, D_EXPERT), D_MODEL ** -0.5),
        'exp_w_down': nrm(ks[26], (nl, N_EXPERTS, D_EXPERT, D_MODEL), BETA * D_EXPERT ** -0.5),
        'ln2_g': 1.0 + nrm(ks[27], (nl, D_MODEL), 0.1),
        'ln2_b': nrm(ks[28], (nl, D_MODEL), 0.02),
    }


def reference(x, c, ctx, c_ctx, w_ada, b_ada, w_in, q_gain, k_gain, tshift_mu, decay_w0, decay_up,
              iclr_a0, iclr_up, gate_up, k_k, k_a, r_k, lnx_g, lnx_b, w_out, ln1_g, ln1_b,
              router_w, exp_w_gate, exp_w_up, exp_w_down, ln2_g, ln2_b):
    for li in range(DEPTH):
        last = li == DEPTH - 1
        mod = jax.nn.silu(c) @ w_ada[li] + b_ada[li]
        sh1, sc1, gt1, sh2, sc2, gt2 = jnp.split(mod[:, None, :], N_MOD, axis=-1)
        mc = jnp.split(jax.nn.silu(c_ctx) @ w_ada[li] + b_ada[li], N_MOD, axis=-1)
        h_lat = modulate(layer_norm(x), sh1, sc1)
        h_ctx = modulate(layer_norm(ctx), mc[0], mc[1])
        mix_l, mix_c = token_mixer(h_lat, h_ctx, w_in[li], q_gain[li], k_gain[li], tshift_mu[li],
                                   decay_w0[li], decay_up[li], iclr_a0[li], iclr_up[li], gate_up[li],
                                   k_k[li], k_a[li], r_k[li], lnx_g[li], lnx_b[li], w_out[li],
                                   not last)
        x = layer_norm_affine(ALPHA * x + gt1 * mix_l, ln1_g[li], ln1_b[li])
        h2 = modulate(layer_norm(x), sh2, sc2)
        ffn_l = expert_choice_ffn(h2, router_w[li], exp_w_gate[li], exp_w_up[li], exp_w_down[li])
        x = layer_norm_affine(ALPHA * x + gt2 * ffn_l, ln2_g[li], ln2_b[li])
        if not last:
            ctx = layer_norm_affine(ALPHA * ctx + mc[2] * mix_c, ln1_g[li], ln1_b[li])
            h2c = modulate(layer_norm(ctx), mc[3], mc[4])
            ffn_c = expert_choice_ffn(h2c, router_w[li], exp_w_gate[li], exp_w_up[li], exp_w_down[li])
            ctx = layer_norm_affine(ALPHA * ctx + mc[5] * ffn_c, ln2_g[li], ln2_b[li])
    return x
```

```python
import functools

import jax
import jax.numpy as jnp
from jax import lax
from jax.experimental import pallas as pl
from jax.experimental.pallas import tpu as pltpu

F32 = jnp.float32
BF16 = jnp.bfloat16

HEAD_DIM = 64
ATT_HEADS = 8
ATT_KV_HEADS = 2
GQA_GROUP = ATT_HEADS // ATT_KV_HEADS
D_ATT = ATT_HEADS * HEAD_DIM
D_ATT_KV = ATT_KV_HEADS * HEAD_DIM
ATT_SCALE = HEAD_DIM ** -0.5
ROPE_THETA = 10000.0
ROPE_PAIRS = HEAD_DIM // 4
GRID_W = 64
RWKV_HEAD = 64
D_RWKV = 512
D_LORA = 128
N_EXPERTS = 16
CAPACITY_FACTOR = 2
N_MOD = 6
LN_EPS = 1e-5
RMS_EPS = 1e-6
GN_EPS = 64e-5
L2_EPS = 1e-12
DEPTH = 1
ALPHA = (2.0 * DEPTH) ** 0.25

WKV_CHUNK = 64
ROW_TILE = 256
ATT_Q_TILE = 128
COMBINE_TILE = 1024
VMEM_LIMIT = 48 * 1024 * 1024


def _params(sem):
    return pltpu.CompilerParams(dimension_semantics=sem, vmem_limit_bytes=VMEM_LIMIT)


def _split3(x):
    hi = x.astype(BF16)
    r1 = x - hi.astype(F32)
    mid = r1.astype(BF16)
    lo = (r1 - mid.astype(F32)).astype(BF16)
    return hi, mid, lo


def _split2(x):
    hi = x.astype(BF16)
    lo = (x - hi.astype(F32)).astype(BF16)
    return hi, lo


def _dg(a, b, dims):
    return lax.dot_general(a, b, (dims, ((), ())), preferred_element_type=F32)


_NN = ((1,), (0,))
_NT = ((1,), (1,))
_TN = ((0,), (0,))


def _mm(a, b, dims=_NN, passes=3):
    if passes == 1:
        return _dg(a.astype(BF16), b.astype(BF16), dims)
    if passes == 3:
        ah, al = _split2(a)
        bh, bl = _split2(b)
        return _dg(ah, bh, dims) + (_dg(ah, bl, dims) + _dg(al, bh, dims))
    ah, am, al = _split3(a)
    bh, bm, bl = _split3(b)
    return (_dg(ah, bh, dims) + (_dg(ah, bm, dims) + _dg(am, bh, dims))
            + (_dg(am, bm, dims) + _dg(ah, bl, dims) + _dg(al, bh, dims)))


def _pieces(x, passes):
    return (x.astype(BF16),) if passes == 1 else _split2(x)


def _mmp(a, b, dims):
    out = _dg(a[0], b[0], dims)
    if len(a) > 1 and len(b) > 1:
        return out + (_dg(a[0], b[1], dims) + _dg(a[1], b[0], dims))
    if len(a) > 1:
        return out + _dg(a[1], b[0], dims)
    if len(b) > 1:
        return out + _dg(a[0], b[1], dims)
    return out


def _mm_exact_lhs(a_bf16, b, dims=_NN):
    bh, bm, bl = _split3(b)
    return _dg(a_bf16, bh, dims) + (_dg(a_bf16, bm, dims) + _dg(a_bf16, bl, dims))


def _mm_exact_rhs(a, b_bf16, dims=_NN):
    ah, am, al = _split3(a)
    return _dg(ah, b_bf16, dims) + (_dg(am, b_bf16, dims) + _dg(al, b_bf16, dims))


def _sigmoid(x):
    return 1.0 / (1.0 + jnp.exp(-x))


def _layer_norm(x):
    mu = jnp.mean(x, axis=-1, keepdims=True)
    xc = x - mu
    var = jnp.mean(xc * xc, axis=-1, keepdims=True)
    return xc * lax.rsqrt(var + LN_EPS)


def _ada_kernel(c_ref, w_ref, b_ref, o_ref):
    c = c_ref[...]
    o_ref[...] = _mm(c * _sigmoid(c), w_ref[...], passes=6) + b_ref[...]


def _ada_mod(c_rows, w_ada, b_ada):
    R, D = c_rows.shape
    N = w_ada.shape[1]
    tn = 1024
    return pl.pallas_call(
        _ada_kernel,
        grid=(N // tn,),
        in_specs=[pl.BlockSpec((R, D), lambda j: (0, 0)),
                  pl.BlockSpec((D, tn), lambda j: (0, j)),
                  pl.BlockSpec((1, tn), lambda j: (0, j))],
        out_specs=pl.BlockSpec((R, tn), lambda j: (0, j)),
        out_shape=jax.ShapeDtypeStruct((R, N), F32),
        compiler_params=_params(("arbitrary",)),
        name="ada_mod",
    )(c_rows, w_ada, b_ada.reshape(1, N))


def _rope(t, cos, sin_lo, sin_hi, reps):
    w = t.shape[-1]
    tile = lambda a: jnp.concatenate([a] * reps, axis=1) if reps > 1 else a
    return (t * tile(cos) + pltpu.roll(t, w - ROPE_PAIRS, 1) * tile(sin_lo)
            + pltpu.roll(t, ROPE_PAIRS, 1) * tile(sin_hi))


def _inproj_kernel(x_ref, ctx_ref, sh_ref, sc_ref, w_ref, bones_ref, qg_ref, kg_ref,
                   cos_ref, slo_ref, shi_ref, q_ref, k_ref, v_ref, rw_ref, *, n_ctx_tiles):
    i = pl.program_id(1)
    x = jnp.where(i < n_ctx_tiles, ctx_ref[0], x_ref[0])
    h = _layer_norm(x) * (1.0 + sc_ref[0]) + sh_ref[0]
    p = _dg(h.astype(BF16), w_ref[...], _NN)
    bones = bones_ref[...]
    cos, slo, shi = cos_ref[...], slo_ref[...], shi_ref[...]

    q = p[:, :D_ATT]
    ss = _mm_exact_rhs(q * q, bones)
    q = q * lax.rsqrt(ss * (1.0 / HEAD_DIM) + RMS_EPS) * qg_ref[...]
    q = _rope(q, cos, slo, shi, D_ATT // 128) * ATT_SCALE
    q_ref[0] = q.astype(BF16)

    k = p[:, D_ATT:D_ATT + D_ATT_KV]
    ss = _mm_exact_rhs(k * k, bones[:D_ATT_KV, :D_ATT_KV])
    k = k * lax.rsqrt(ss * (1.0 / HEAD_DIM) + RMS_EPS) * kg_ref[...]
    k = _rope(k, cos, slo, shi, D_ATT_KV // 128).astype(BF16)
    v = p[:, D_ATT + D_ATT_KV:D_ATT + 2 * D_ATT_KV].astype(BF16)
    for g in range(ATT_KV_HEADS):
        k_ref[0, g] = k[:, g * HEAD_DIM:(g + 1) * HEAD_DIM]
        v_ref[0, g] = v[:, g * HEAD_DIM:(g + 1) * HEAD_DIM]
    rw_ref[0] = p[:, D_ATT + 2 * D_ATT_KV:]


def _inproj(x, ctx, shift, scale, w_in_bf16, bones, qg, kg, cos, slo, shi):
    B, L, D = x.shape
    n_ctx = ctx.shape[1]
    tm = ROW_TILE
    nct = n_ctx // tm
    Tc = n_ctx + L
    d_in = w_in_bf16.shape[1]
    d_rw = d_in - D_ATT - 2 * D_ATT_KV
    mod_row = lambda b, i: (jnp.where(i < nct, B, b), 0, 0)
    const = lambda b, i: (0, 0)
    return pl.pallas_call(
        functools.partial(_inproj_kernel, n_ctx_tiles=nct),
        grid=(B, Tc // tm),
        in_specs=[pl.BlockSpec((1, tm, D), lambda b, i: (b, jnp.maximum(i - nct, 0), 0)),
                  pl.BlockSpec((1, tm, D), lambda b, i: (b, jnp.minimum(i, nct - 1), 0)),
                  pl.BlockSpec((1, 1, D), mod_row),
                  pl.BlockSpec((1, 1, D), mod_row),
                  pl.BlockSpec((D, d_in), const),
                  pl.BlockSpec(bones.shape, const),
                  pl.BlockSpec((1, D_ATT), const),
                  pl.BlockSpec((1, D_ATT_KV), const),
                  pl.BlockSpec((tm, 128), lambda b, i: (i, 0)),
                  pl.BlockSpec((tm, 128), lambda b, i: (i, 0)),
                  pl.BlockSpec((tm, 128), lambda b, i: (i, 0))],
        out_specs=[pl.BlockSpec((1, tm, D_ATT), lambda b, i: (b, jnp.maximum(i - nct, 0), 0)),
                   pl.BlockSpec((1, ATT_KV_HEADS, tm, HEAD_DIM), lambda b, i: (b, 0, i, 0)),
                   pl.BlockSpec((1, ATT_KV_HEADS, tm, HEAD_DIM), lambda b, i: (b, 0, i, 0)),
                   pl.BlockSpec((1, tm, d_rw), lambda b, i: (b, i, 0))],
        out_shape=[jax.ShapeDtypeStruct((B, L, D_ATT), BF16),
                   jax.ShapeDtypeStruct((B, ATT_KV_HEADS, Tc, HEAD_DIM), BF16),
                   jax.ShapeDtypeStruct((B, ATT_KV_HEADS, Tc, HEAD_DIM), BF16),
                   jax.ShapeDtypeStruct((B, Tc, d_rw), F32)],
        compiler_params=_params(("arbitrary", "arbitrary")),
        name="inproj",
    )(x, ctx, shift, scale, w_in_bf16, bones, qg, kg, cos, slo, shi)


def _attn_kernel(q_ref, k_ref, v_ref, o_ref):
    q = q_ref[0]
    tq = q.shape[0]
    qs = jnp.concatenate([q[:, h * HEAD_DIM:(h + 1) * HEAD_DIM] for h in range(GQA_GROUP)], axis=0)
    s = _dg(qs, k_ref[0, 0], _NT)
    m = jnp.max(s, axis=-1, keepdims=True)
    p = jnp.exp(s - m)
    l = jnp.sum(p, axis=-1, keepdims=True)
    o = _dg(p.astype(BF16), v_ref[0, 0], _NN) / l
    o_ref[0] = jnp.concatenate([o[h * tq:(h + 1) * tq] for h in range(GQA_GROUP)], axis=1).astype(BF16)


def _attention(q, k, v):
    B, L, _ = q.shape
    Tc = k.shape[2]
    tq = ATT_Q_TILE
    gw = GQA_GROUP * HEAD_DIM
    return pl.pallas_call(
        _attn_kernel,
        grid=(B, ATT_KV_HEADS, L // tq),
        in_specs=[pl.BlockSpec((1, tq, gw), lambda b, g, i: (b, i, g)),
                  pl.BlockSpec((1, 1, Tc, HEAD_DIM), lambda b, g, i: (b, g, 0, 0)),
                  pl.BlockSpec((1, 1, Tc, HEAD_DIM), lambda b, g, i: (b, g, 0, 0))],
        out_specs=pl.BlockSpec((1, tq, gw), lambda b, g, i: (b, i, g)),
        out_shape=jax.ShapeDtypeStruct((B, L, D_ATT), BF16),
        compiler_params=_params(("arbitrary", "arbitrary", "arbitrary")),
        name="attention",
    )(q, k, v)


def _rwkv_prep_kernel(rw_ref, prev_ref, next_ref, mu_ref, w0_ref, dup_ref, a0_ref, aup_ref, gup_ref,
                      kk_ref, ka_ref, rk_ref, bones_ref,
                      r_o, v_o, kk_o, g_o, bonus_o, lw_o, kd_o, b_o, *, n_ctx, n_tok):
    i = pl.program_id(1)
    p = rw_ref[0]
    tm = p.shape[0]
    t0 = i * tm
    has_prev = jnp.where((t0 == 0) | (t0 == n_ctx), 0.0, 1.0)
    has_next = jnp.where((t0 + tm == n_ctx) | (t0 + tm == n_tok), 0.0, 1.0)
    row = lax.broadcasted_iota(jnp.int32, (tm, 1), 0)
    up = jnp.where(row == 0, prev_ref[0, 7:8, :] * has_prev, pltpu.roll(p, 1, 0))
    dn = jnp.where(row == tm - 1, next_ref[0, 0:1, :] * has_next, pltpu.roll(p, tm - 1, 0))
    s = p + (0.5 * (up + dn) - p) * mu_ref[...]

    D = D_RWKV
    r, k, v = s[:, :D], s[:, D:2 * D], s[:, 2 * D:3 * D]
    lora = s[:, 3 * D:3 * D + D_LORA]
    gl = s[:, 3 * D + D_LORA:]
    bones = bones_ref[...]

    g_o[0] = _mm(_sigmoid(gl), gup_ref[...], passes=3)
    kk = k * kk_ref[...]
    nrm = jnp.sqrt(_mm_exact_rhs(kk * kk, bones))
    kk = kk / jnp.maximum(nrm, L2_EPS)
    wt = jnp.tanh(lora)
    r_o[0] = r
    v_o[0] = v
    kk_o[0] = kk
    bonus_o[0] = _mm_exact_rhs(r * k * rk_ref[...], bones) * v
    for d in range(2):
        z = w0_ref[d:d + 1, :] + _mm(wt, dup_ref[d], passes=3)
        softplus_neg = jnp.maximum(-z, 0.0) + jnp.log(1.0 + jnp.exp(-jnp.abs(z)))
        w = -softplus_neg - 0.5
        lw_o[d, 0] = -jnp.exp(w)
        a = _sigmoid(a0_ref[d:d + 1, :] + _mm(lora, aup_ref[d], passes=3))
        kd_o[d, 0] = k * (1.0 + (a - 1.0) * ka_ref[...])
        b_o[d, 0] = kk * a


def _rwkv_prep(rw, n_ctx, mu, w0, dup, a0, aup, gup, k_k, k_a, r_k, bones):
    B, Tc, d_rw = rw.shape
    tm = ROW_TILE
    D = D_RWKV
    const2 = lambda b, i: (0, 0)
    const3 = lambda b, i: (0, 0, 0)
    tok = pl.BlockSpec((1, tm, D), lambda b, i: (b, i, 0))
    tok2 = pl.BlockSpec((2, 1, tm, D), lambda b, i: (0, b, i, 0))
    s1 = jax.ShapeDtypeStruct((B, Tc, D), F32)
    s2 = jax.ShapeDtypeStruct((2, B, Tc, D), F32)
    hb = tm // 8
    return pl.pallas_call(
        functools.partial(_rwkv_prep_kernel, n_ctx=n_ctx, n_tok=Tc),
        grid=(B, Tc // tm),
        in_specs=[pl.BlockSpec((1, tm, d_rw), lambda b, i: (b, i, 0)),
                  pl.BlockSpec((1, 8, d_rw), lambda b, i: (b, jnp.maximum(i * hb - 1, 0), 0)),
                  pl.BlockSpec((1, 8, d_rw), lambda b, i: (b, jnp.minimum((i + 1) * hb, Tc // 8 - 1), 0)),
                  pl.BlockSpec((1, d_rw), const2),
                  pl.BlockSpec((2, D), const2),
                  pl.BlockSpec((2, D_LORA, D), const3),
                  pl.BlockSpec((2, D), const2),
                  pl.BlockSpec((2, D_LORA, D), const3),
                  pl.BlockSpec((D_LORA, D), const2),
                  pl.BlockSpec((1, D), const2),
                  pl.BlockSpec((1, D), const2),
                  pl.BlockSpec((1, D), const2),
                  pl.BlockSpec(bones.shape, const2)],
        out_specs=[tok, tok, tok, tok, tok, tok2, tok2, tok2],
        out_shape=[s1, s1, s1, s1, s1, s2, s2, s2],
        compiler_params=_params(("arbitrary", "arbitrary")),
        name="rwkv_prep",
    )(rw, rw, rw, mu, w0, dup, a0, aup, gup, k_k, k_a, r_k, bones)


def _wkv_kernel(r_ref, v_ref, kk_ref, lw_ref, kd_ref, b_ref, y_ref, s_ref, *, reverse, n_heads, passes):
    c = pl.program_id(1)

    @pl.when(c == 0)
    def _():
        s_ref[...] = jnp.zeros_like(s_ref)

    C = r_ref.shape[1]
    N = RWKV_HEAD
    row = lax.broadcasted_iota(jnp.int32, (C, C), 0)
    col = lax.broadcasted_iota(jnp.int32, (C, C), 1)
    if reverse:
        strict = row < col
        incl = row <= col
    else:
        strict = row > col
        incl = row >= col
    tri = jnp.where(incl, 1.0, 0.0).astype(BF16)

    lw = lw_ref[0, 0]
    cl = _mm_exact_lhs(tri, lw)
    ecl = jnp.exp(cl)
    ecx = jnp.exp(cl - lw)
    eneg = jnp.exp(-cl)
    last = 0 if reverse else C - 1
    etot = ecl[last:last + 1, :]
    rt_all = r_ref[0] * ecl
    at_all = -(kk_ref[0] * ecx)
    kt_all = kd_ref[0, 0] * eneg
    bt_all = b_ref[0, 0] * eneg
    kh_all = kt_all * etot
    bh_all = bt_all * etot
    v_all = v_ref[0]

    sp = lambda x: _pieces(x, passes)
    heads = range(n_heads)
    hs = lambda t, h: tuple(p[:, h * N:(h + 1) * N] for p in t)
    at_p, rt_p, kt_p, bt_p, v_p = sp(at_all), sp(rt_all), sp(kt_all), sp(bt_all), sp(v_all)
    kh_p, bh_p = sp(kh_all), sp(bh_all)
    s0 = [s_ref[h] for h in heads]
    s0_p = [sp(s) for s in s0]
    ar = [tuple(jnp.concatenate([a, r], axis=0) for a, r in zip(hs(at_p, h), hs(rt_p, h))) for h in heads]
    g_b = [_mmp(ar[h], hs(bt_p, h), _NT) for h in heads]
    g_k = [_mmp(ar[h], hs(kt_p, h), _NT) for h in heads]
    l_ak = [sp(jnp.where(strict, g[:C], 0.0)) for g in g_k]
    u = [_mmp(hs(at_p, h), s0_p[h], _NT) + _mmp(l_ak[h], hs(v_p, h), _NN) for h in heads]
    pw = [sp(jnp.where(strict, g[:C], 0.0)) for g in g_b]
    n = 1
    while True:
        u = [u[h] + _mmp(pw[h], sp(u[h]), _NN) for h in heads]
        n *= 2
        if n >= C:
            break
        pw = [sp(_mmp(pw[h], pw[h], _NN)) for h in heads]
    u_p = [sp(x) for x in u]
    l_rb = [sp(jnp.where(incl, g[C:], 0.0)) for g in g_b]
    l_rk = [sp(jnp.where(incl, g[C:], 0.0)) for g in g_k]
    for h in heads:
        y = (_mmp(hs(rt_p, h), s0_p[h], _NT) + _mmp(l_rb[h], u_p[h], _NN)
             + _mmp(l_rk[h], hs(v_p, h), _NN))
        y_ref[0, :, h * N:(h + 1) * N] = y
    for h in heads:
        s_ref[h] = (s0[h] * etot[:, h * N:(h + 1) * N] + _mmp(u_p[h], hs(bh_p, h), _TN)
                    + _mmp(hs(v_p, h), hs(kh_p, h), _TN))


def _wkv_scan(r, v, kk, lw, kd, b, n_ctx, *, reverse, passes=3):
    B, T, D = r.shape
    C = WKV_CHUNK
    n_heads = D // RWKV_HEAD
    n_chunks = T // C
    n_cc = n_ctx // C
    d = 1 if reverse else 0

    def tok(c):
        if not reverse:
            return c
        return jnp.where(c < n_cc, n_cc - 1 - c, n_chunks - 1 + n_cc - c)

    shared = pl.BlockSpec((1, C, D), lambda bi, c: (bi, tok(c), 0))
    per_dir = pl.BlockSpec((1, 1, C, D), lambda bi, c: (d, bi, tok(c), 0))
    return pl.pallas_call(
        functools.partial(_wkv_kernel, reverse=reverse, n_heads=n_heads, passes=passes),
        grid=(B, n_chunks),
        in_specs=[shared, shared, shared, per_dir, per_dir, per_dir],
        out_specs=pl.BlockSpec((1, C, D), lambda bi, c: (bi, tok(c), 0)),
        out_shape=jax.ShapeDtypeStruct((B, T, D), F32),
        scratch_shapes=[pltpu.VMEM((n_heads, RWKV_HEAD, RWKV_HEAD), F32)],
        compiler_params=_params(("arbitrary", "arbitrary")),
        name="wkv_rev" if reverse else "wkv_fwd",
    )(r, v, kk, lw, kd, b)


def _post_kernel(yf_ref, yb_ref, bonus_ref, g_ref, att_ref, x_ref, gt_ref, sh_ref, sc_ref, wo_ref,
                 lxg_ref, lxb_ref, l1g_ref, l1b_ref, rwt_ref, bones_ref, x1_ref, h2_ref, aff_ref):
    bones = bones_ref[...]
    y = yf_ref[0] + yb_ref[0]
    inv = 1.0 / RWKV_HEAD
    mu = _mm_exact_rhs(y, bones) * inv
    yc = y - mu
    var = _mm_exact_rhs(yc * yc, bones) * inv
    yn = yc * lax.rsqrt(var + GN_EPS) * lxg_ref[...] + lxb_ref[...]
    rw_out = (yn + bonus_ref[0]) * g_ref[0]
    mix = _dg(att_ref[0], wo_ref[:D_ATT, :], _NN) + _dg(rw_out.astype(BF16), wo_ref[D_ATT:, :], _NN)
    x1 = _layer_norm(ALPHA * x_ref[0] + gt_ref[0] * mix) * l1g_ref[...] + l1b_ref[...]
    x1_ref[0] = x1
    h2 = _layer_norm(x1) * (1.0 + sc_ref[0]) + sh_ref[0]
    h2_ref[0] = h2.astype(BF16)
    logits = _mm(rwt_ref[...], h2, _NT, passes=6)
    e = jnp.exp(logits - jnp.max(logits, axis=0, keepdims=True))
    aff_ref[0] = e / jnp.sum(e, axis=0, keepdims=True)


def _post(yf, yb, bonus, g, att, x, gt1, sh2, sc2, w_out_bf16, lnx_g, lnx_b, ln1_g, ln1_b, router_wt,
          bones, n_ctx):
    B, L, D = x.shape
    tm = ROW_TILE
    nct = n_ctx // tm
    E = router_wt.shape[0]
    const = lambda b, i: (0, 0)
    cat = pl.BlockSpec((1, tm, D_RWKV), lambda b, i: (b, i + nct, 0))
    mod = pl.BlockSpec((1, 1, D), lambda b, i: (b, 0, 0))
    vec = lambda n: pl.BlockSpec((1, n), const)
    return pl.pallas_call(
        _post_kernel,
        grid=(B, L // tm),
        in_specs=[cat, cat, cat, cat,
                  pl.BlockSpec((1, tm, D_ATT), lambda b, i: (b, i, 0)),
                  pl.BlockSpec((1, tm, D), lambda b, i: (b, i, 0)),
                  mod, mod, mod,
                  pl.BlockSpec(w_out_bf16.shape, const),
                  vec(D_RWKV), vec(D_RWKV), vec(D), vec(D),
                  pl.BlockSpec((E, D), const),
                  pl.BlockSpec(bones.shape, const)],
        out_specs=[pl.BlockSpec((1, tm, D), lambda b, i: (b, i, 0)),
                   pl.BlockSpec((1, tm, D), lambda b, i: (b, i, 0)),
                   pl.BlockSpec((1, E, tm), lambda b, i: (b, 0, i))],
        out_shape=[jax.ShapeDtypeStruct((B, L, D), F32),
                   jax.ShapeDtypeStruct((B, L, D), BF16),
                   jax.ShapeDtypeStruct((B, E, L), F32)],
        compiler_params=_params(("arbitrary", "arbitrary")),
        name="post_mix",
    )(yf, yb, bonus, g, att, x, gt1, sh2, sc2, w_out_bf16, lnx_g, lnx_b, ln1_g, ln1_b, router_wt, bones)


def _cumsum_lanes(x):
    n = x.shape[1]
    lane = lax.broadcasted_iota(jnp.int32, x.shape, 1)
    s = 1
    while s < n:
        x = x + jnp.where(lane >= s, pltpu.roll(x, s, 1), 0)
        s *= 2
    return x


def _route_kernel(aff_ref, pos_ref, gate_ref, *, cap):
    aff = aff_ref[0]
    E = aff.shape[0]
    count = lambda m: jnp.sum(jnp.where(m, 1, 0), axis=1, keepdims=True)
    thr_bits = jnp.zeros((E, 1), jnp.int32)
    for bit in range(29, -1, -1):
        cand = thr_bits | (1 << bit)
        cand_f = lax.bitcast_convert_type(cand, F32)
        thr_bits = jnp.where(count(aff >= cand_f) >= cap, cand, thr_bits)
    thr = lax.bitcast_convert_type(thr_bits, F32)
    above = aff > thr
    tie = aff == thr
    need = cap - count(above)
    tie_rank = _cumsum_lanes(jnp.where(tie, 1, 0))
    sel = above | (tie & (tie_rank <= need))
    slot = _cumsum_lanes(jnp.where(sel, 1, 0)) - 1
    pos = jnp.where(sel, slot, -1)
    gate = jnp.where(sel, aff, 0.0)
    for e in range(E):
        pos_ref[0, e] = pos[e:e + 1, :]
        gate_ref[0, e] = gate[e:e + 1, :]


def _route(aff, cap):
    B, E, L = aff.shape
    out = pl.BlockSpec((1, E, 1, L), lambda b: (b, 0, 0, 0))
    return pl.pallas_call(
        functools.partial(_route_kernel, cap=cap),
        grid=(B,),
        in_specs=[pl.BlockSpec((1, E, L), lambda b: (b, 0, 0))],
        out_specs=[out, out],
        out_shape=[jax.ShapeDtypeStruct((B, E, 1, L), jnp.int32),
                   jax.ShapeDtypeStruct((B, E, 1, L), F32)],
        compiler_params=_params(("arbitrary",)),
        name="route",
    )(aff)


def _expert_kernel(h_ref, pos_ref, gate_ref, wg_ref, wu_ref, wd_ref, y_ref, *, cap):
    pos = pos_ref[0, 0]
    L = pos.shape[1]
    slot = lax.broadcasted_iota(jnp.int32, (cap, L), 0)
    hit = pos == slot
    onehot = jnp.where(hit, 1.0, 0.0).astype(BF16)
    gate = jnp.sum(jnp.where(hit, gate_ref[0, 0], 0.0), axis=1, keepdims=True)
    xin = _dg(onehot, h_ref[0], _NN).astype(BF16)
    hg = _dg(xin, wg_ref[0], _NN)
    hu = _dg(xin, wu_ref[0], _NN)
    hid = (hg * _sigmoid(hg) * hu).astype(BF16)
    y_ref[0, 0] = (_dg(hid, wd_ref[0], _NN) * gate).astype(BF16)


def _experts(h2, pos, gate, wg, wu, wd, cap):
    B, L, D = h2.shape
    E, _, F = wg.shape
    rowv = pl.BlockSpec((1, 1, 1, L), lambda b, e: (b, e, 0, 0))
    return pl.pallas_call(
        functools.partial(_expert_kernel, cap=cap),
        grid=(B, E),
        in_specs=[pl.BlockSpec((1, L, D), lambda b, e: (b, 0, 0)),
                  rowv, rowv,
                  pl.BlockSpec((1, D, F), lambda b, e: (e, 0, 0)),
                  pl.BlockSpec((1, D, F), lambda b, e: (e, 0, 0)),
                  pl.BlockSpec((1, F, D), lambda b, e: (e, 0, 0))],
        out_specs=pl.BlockSpec((1, 1, cap, D), lambda b, e: (b, e, 0, 0)),
        out_shape=jax.ShapeDtypeStruct((B, E, cap, D), BF16),
        compiler_params=_params(("arbitrary", "arbitrary")),
        name="experts",
    )(h2, pos, gate, wg, wu, wd)


def _combine_kernel(pos_ref, y_ref, x1_ref, gt_ref, lg_ref, lb_ref, o_ref, acc_ref, *, cap):
    e = pl.program_id(2)

    @pl.when(e == 0)
    def _():
        acc_ref[...] = jnp.zeros_like(acc_ref)

    pos = pos_ref[0, 0]
    slot = lax.broadcasted_iota(jnp.int32, (cap, pos.shape[1]), 0)
    onehot = jnp.where(pos == slot, 1.0, 0.0).astype(BF16)
    acc_ref[...] += _dg(onehot, y_ref[0, 0], _TN)

    @pl.when(e == pl.num_programs(2) - 1)
    def _():
        z = ALPHA * x1_ref[0] + gt_ref[0] * acc_ref[...]
        o_ref[0] = _layer_norm(z) * lg_ref[...] + lb_ref[...]


def _combine(pos, y, x1, gt2, ln2_g, ln2_b, cap):
    B, L, D = x1.shape
    E = y.shape[1]
    tl = min(COMBINE_TILE, L)
    const = lambda b, i, e: (0, 0)
    return pl.pallas_call(
        functools.partial(_combine_kernel, cap=cap),
        grid=(B, L // tl, E),
        in_specs=[pl.BlockSpec((1, 1, 1, tl), lambda b, i, e: (b, e, 0, i)),
                  pl.BlockSpec((1, 1, cap, D), lambda b, i, e: (b, e, 0, 0)),
                  pl.BlockSpec((1, tl, D), lambda b, i, e: (b, i, 0)),
                  pl.BlockSpec((1, 1, D), lambda b, i, e: (b, 0, 0)),
                  pl.BlockSpec((1, D), const),
                  pl.BlockSpec((1, D), const)],
        out_specs=pl.BlockSpec((1, tl, D), lambda b, i, e: (b, i, 0)),
        out_shape=jax.ShapeDtypeStruct((B, L, D), F32),
        scratch_shapes=[pltpu.VMEM((tl, D), F32)],
        compiler_params=_params(("arbitrary", "arbitrary", "arbitrary")),
        name="combine",
    )(pos, y, x1, gt2, ln2_g, ln2_b)


def _rope_tables(n_ctx, n_lat):
    lane = jnp.arange(128)
    j = lane % HEAD_DIM
    axis = j // (2 * ROPE_PAIRS)
    upper = (j % (2 * ROPE_PAIRS)) >= ROPE_PAIRS
    inv = ROPE_THETA ** (-(j % ROPE_PAIRS).astype(F32) / ROPE_PAIRS)
    t = jnp.arange(n_lat)
    coord = jnp.where(axis[None, :] == 0, (t // GRID_W)[:, None], (t % GRID_W)[:, None]).astype(F32)
    ang = coord * inv[None, :]
    cos, sin = jnp.cos(ang), jnp.sin(ang)
    s_lo = jnp.where(upper[None, :], 0.0, -sin)
    s_hi = jnp.where(upper[None, :], sin, 0.0)
    pad = lambda a, fill: jnp.concatenate([jnp.full((n_ctx, 128), fill, F32), a], axis=0)
    return pad(cos, 1.0), pad(s_lo, 0.0), pad(s_hi, 0.0)


def _head_ones(n):
    h = jnp.arange(n) // HEAD_DIM
    return (h[:, None] == h[None, :]).astype(BF16)


def kernel(x, c, ctx, c_ctx, w_ada, b_ada, w_in, q_gain, k_gain, tshift_mu, decay_w0, decay_up, iclr_a0,
           iclr_up, gate_up, k_k, k_a, r_k, lnx_g, lnx_b, w_out, ln1_g, ln1_b, router_w, exp_w_gate,
           exp_w_up, exp_w_down, ln2_g, ln2_b):
    B, L, D = x.shape
    n_ctx = ctx.shape[1]
    li = 0
    row = lambda a: a.reshape(1, -1)

    n_rows = -(-(B + 1) // 8) * 8
    c_rows = jnp.concatenate([c, c_ctx[None, :], jnp.zeros((n_rows - B - 1, D), F32)], axis=0)
    mod = _ada_mod(c_rows, w_ada[li], b_ada[li]).reshape(n_rows, N_MOD, 1, D)
    sh1, sc1, gt1, sh2, sc2, gt2 = (mod[:, m] for m in range(N_MOD))

    bones = _head_ones(D_RWKV)
    cos, s_lo, s_hi = _rope_tables(n_ctx, L)
    q, k, v, rw = _inproj(x, ctx, sh1, sc1, w_in[li].astype(BF16), bones,
                          row(jnp.tile(q_gain[li], ATT_HEADS)), row(jnp.tile(k_gain[li], ATT_KV_HEADS)),
                          cos, s_lo, s_hi)
    att = _attention(q, k, v)

    zpad = jnp.zeros((2, D_LORA // 2, D_RWKV), F32)
    dup = jnp.concatenate([decay_up[li], zpad], axis=1)
    aup = jnp.concatenate([zpad, iclr_up[li]], axis=1)
    r, vr, kk, g, bonus, lw, kd, bv = _rwkv_prep(
        rw, n_ctx, row(tshift_mu[li]), decay_w0[li], dup, iclr_a0[li], aup, gate_up[li],
        row(k_k[li]), row(k_a[li]), row(r_k[li]), bones)
    yf = _wkv_scan(r, vr, kk, lw, kd, bv, n_ctx, reverse=False)
    yb = _wkv_scan(r, vr, kk, lw, kd, bv, n_ctx, reverse=True)

    x1, h2, aff = _post(yf, yb, bonus, g, att, x, gt1, sh2, sc2, w_out[li].astype(BF16),
                        row(lnx_g[li]), row(lnx_b[li]), row(ln1_g[li]), row(ln1_b[li]),
                        router_w[li].T, bones, n_ctx)
    cap = CAPACITY_FACTOR * L // N_EXPERTS
    pos, gate = _route(aff, cap)
    y = _experts(h2, pos, gate, exp_w_gate[li].astype(BF16), exp_w_up[li].astype(BF16),
                 exp_w_down[li].astype(BF16), cap)
    return _combine(pos, y, x1, gt2, row(ln2_g[li]), row(ln2_b[li]), cap)
```

```python
import functools

import jax
import jax.numpy as jnp
from jax import lax
from jax.experimental import pallas as pl
from jax.experimental.pallas import tpu as pltpu

F32 = jnp.float32
BF16 = jnp.bfloat16

HEAD_DIM = 64
ATT_HEADS = 8
ATT_KV_HEADS = 2
GQA_GROUP = ATT_HEADS // ATT_KV_HEADS
D_ATT = ATT_HEADS * HEAD_DIM
D_ATT_KV = ATT_KV_HEADS * HEAD_DIM
ATT_SCALE = HEAD_DIM ** -0.5
ROPE_THETA = 10000.0
ROPE_PAIRS = HEAD_DIM // 4
GRID_W = 64
RWKV_HEAD = 64
D_RWKV = 512
D_LORA = 128
N_EXPERTS = 16
CAPACITY_FACTOR = 2
N_MOD = 6
LN_EPS = 1e-5
RMS_EPS = 1e-6
GN_EPS = 64e-5
L2_EPS = 1e-12
DEPTH = 1
ALPHA = (2.0 * DEPTH) ** 0.25

WKV_CHUNK = 64
ROW_TILE = 256
ATT_Q_TILE = 128
COMBINE_TILE = 1024
VMEM_LIMIT = 48 * 1024 * 1024


def _params(sem):
    return pltpu.CompilerParams(dimension_semantics=sem, vmem_limit_bytes=VMEM_LIMIT)


def _split3(x):
    hi = x.astype(BF16)
    r1 = x - hi.astype(F32)
    mid = r1.astype(BF16)
    lo = (r1 - mid.astype(F32)).astype(BF16)
    return hi, mid, lo


def _split2(x):
    hi = x.astype(BF16)
    lo = (x - hi.astype(F32)).astype(BF16)
    return hi, lo


def _dg(a, b, dims):
    return lax.dot_general(a, b, (dims, ((), ())), preferred_element_type=F32)


_NN = ((1,), (0,))
_NT = ((1,), (1,))
_TN = ((0,), (0,))


def _mm(a, b, dims=_NN, passes=3):
    if passes == 1:
        return _dg(a.astype(BF16), b.astype(BF16), dims)
    if passes == 3:
        ah, al = _split2(a)
        bh, bl = _split2(b)
        return _dg(ah, bh, dims) + (_dg(ah, bl, dims) + _dg(al, bh, dims))
    ah, am, al = _split3(a)
    bh, bm, bl = _split3(b)
    return (_dg(ah, bh, dims) + (_dg(ah, bm, dims) + _dg(am, bh, dims))
            + (_dg(am, bm, dims) + _dg(ah, bl, dims) + _dg(al, bh, dims)))


def _pieces(x, passes):
    return (x.astype(BF16),) if passes == 1 else _split2(x)


def _mmp(a, b, dims):
    out = _dg(a[0], b[0], dims)
    if len(a) > 1 and len(b) > 1:
        return out + (_dg(a[0], b[1], dims) + _dg(a[1], b[0], dims))
    if len(a) > 1:
        return out + _dg(a[1], b[0], dims)
    if len(b) > 1:
        return out + _dg(a[0], b[1], dims)
    return out


def _mm_exact_lhs(a_bf16, b, dims=_NN):
    bh, bm, bl = _split3(b)
    return _dg(a_bf16, bh, dims) + (_dg(a_bf16, bm, dims) + _dg(a_bf16, bl, dims))


def _mm_exact_rhs(a, b_bf16, dims=_NN):
    ah, am, al = _split3(a)
    return _dg(ah, b_bf16, dims) + (_dg(am, b_bf16, dims) + _dg(al, b_bf16, dims))


def _sigmoid(x):
    return 1.0 / (1.0 + jnp.exp(-x))


def _layer_norm(x):
    mu = jnp.mean(x, axis=-1, keepdims=True)
    xc = x - mu
    var = jnp.mean(xc * xc, axis=-1, keepdims=True)
    return xc * lax.rsqrt(var + LN_EPS)


def _ada_kernel(c_ref, w_ref, b_ref, o_ref):
    c = c_ref[...]
    o_ref[...] = _mm(c * _sigmoid(c), w_ref[...], passes=6) + b_ref[...]


def _ada_mod(c_rows, w_ada, b_ada):
    R, D = c_rows.shape
    N = w_ada.shape[1]
    tn = 1024
    return pl.pallas_call(
        _ada_kernel,
        grid=(N // tn,),
        in_specs=[pl.BlockSpec((R, D), lambda j: (0, 0)),
                  pl.BlockSpec((D, tn), lambda j: (0, j)),
                  pl.BlockSpec((1, tn), lambda j: (0, j))],
        out_specs=pl.BlockSpec((R, tn), lambda j: (0, j)),
        out_shape=jax.ShapeDtypeStruct((R, N), F32),
        compiler_params=_params(("arbitrary",)),
        name="ada_mod",
    )(c_rows, w_ada, b_ada.reshape(1, N))


def _rope(t, cos, sin_lo, sin_hi, reps):
    w = t.shape[-1]
    tile = lambda a: jnp.concatenate([a] * reps, axis=1) if reps > 1 else a
    return (t * tile(cos) + pltpu.roll(t, w - ROPE_PAIRS, 1) * tile(sin_lo)
            + pltpu.roll(t, ROPE_PAIRS, 1) * tile(sin_hi))


def _inproj_kernel(x_ref, ctx_ref, sh_ref, sc_ref, w_ref, bones_ref, qg_ref, kg_ref,
                   cos_ref, slo_ref, shi_ref, q_ref, k_ref, v_ref, rw_ref, *, n_ctx_tiles):
    i = pl.program_id(1)
    x = jnp.where(i < n_ctx_tiles, ctx_ref[0], x_ref[0])
    h = _layer_norm(x) * (1.0 + sc_ref[0]) + sh_ref[0]
    p = _dg(h.astype(BF16), w_ref[...], _NN)
    bones = bones_ref[...]
    cos, slo, shi = cos_ref[...], slo_ref[...], shi_ref[...]

    q = p[:, :D_ATT]
    ss = _mm_exact_rhs(q * q, bones)
    q = q * lax.rsqrt(ss * (1.0 / HEAD_DIM) + RMS_EPS) * qg_ref[...]
    q = _rope(q, cos, slo, shi, D_ATT // 128) * ATT_SCALE
    q_ref[0] = q.astype(BF16)

    k = p[:, D_ATT:D_ATT + D_ATT_KV]
    ss = _mm_exact_rhs(k * k, bones[:D_ATT_KV, :D_ATT_KV])
    k = k * lax.rsqrt(ss * (1.0 / HEAD_DIM) + RMS_EPS) * kg_ref[...]
    k = _rope(k, cos, slo, shi, D_ATT_KV // 128).astype(BF16)
    v = p[:, D_ATT + D_ATT_KV:D_ATT + 2 * D_ATT_KV].astype(BF16)
    for g in range(ATT_KV_HEADS):
        k_ref[0, g] = k[:, g * HEAD_DIM:(g + 1) * HEAD_DIM]
        v_ref[0, g] = v[:, g * HEAD_DIM:(g + 1) * HEAD_DIM]
    rw_ref[0] = p[:, D_ATT + 2 * D_ATT_KV:]


def _inproj(x, ctx, shift, scale, w_in_bf16, bones, qg, kg, cos, slo, shi):
    B, L, D = x.shape
    n_ctx = ctx.shape[1]
    tm = ROW_TILE
    nct = n_ctx // tm
    Tc = n_ctx + L
    d_in = w_in_bf16.shape[1]
    d_rw = d_in - D_ATT - 2 * D_ATT_KV
    mod_row = lambda b, i: (jnp.where(i < nct, B, b), 0, 0)
    const = lambda b, i: (0, 0)
    return pl.pallas_call(
        functools.partial(_inproj_kernel, n_ctx_tiles=nct),
        grid=(B, Tc // tm),
        in_specs=[pl.BlockSpec((1, tm, D), lambda b, i: (b, jnp.maximum(i - nct, 0), 0)),
                  pl.BlockSpec((1, tm, D), lambda b, i: (b, jnp.minimum(i, nct - 1), 0)),
                  pl.BlockSpec((1, 1, D), mod_row),
                  pl.BlockSpec((1, 1, D), mod_row),
                  pl.BlockSpec((D, d_in), const),
                  pl.BlockSpec(bones.shape, const),
                  pl.BlockSpec((1, D_ATT), const),
                  pl.BlockSpec((1, D_ATT_KV), const),
                  pl.BlockSpec((tm, 128), lambda b, i: (i, 0)),
                  pl.BlockSpec((tm, 128), lambda b, i: (i, 0)),
                  pl.BlockSpec((tm, 128), lambda b, i: (i, 0))],
        out_specs=[pl.BlockSpec((1, tm, D_ATT), lambda b, i: (b, jnp.maximum(i - nct, 0), 0)),
                   pl.BlockSpec((1, ATT_KV_HEADS, tm, HEAD_DIM), lambda b, i: (b, 0, i, 0)),
                   pl.BlockSpec((1, ATT_KV_HEADS, tm, HEAD_DIM), lambda b, i: (b, 0, i, 0)),
                   pl.BlockSpec((1, tm, d_rw), lambda b, i: (b, i, 0))],
        out_shape=[jax.ShapeDtypeStruct((B, L, D_ATT), BF16),
                   jax.ShapeDtypeStruct((B, ATT_KV_HEADS, Tc, HEAD_DIM), BF16),
                   jax.ShapeDtypeStruct((B, ATT_KV_HEADS, Tc, HEAD_DIM), BF16),
                   jax.ShapeDtypeStruct((B, Tc, d_rw), F32)],
        compiler_params=_params(("arbitrary", "arbitrary")),
        name="inproj",
    )(x, ctx, shift, scale, w_in_bf16, bones, qg, kg, cos, slo, shi)


def _attn_kernel(q_ref, k_ref, v_ref, o_ref, s0, s1, p0, p1, l0, l1):
    i = pl.program_id(2)

    @pl.when(i == 0)
    def _():
        for ref in (s0, s1, p0, p1):
            ref[...] = jnp.zeros_like(ref)
        l0[...] = jnp.ones_like(l0)
        l1[...] = jnp.ones_like(l1)

    def step(s_w, s_r, p_w, l_w, p_r, l_r):
        q = q_ref[0]
        tq = q.shape[0]
        qs = jnp.concatenate([q[:, h * HEAD_DIM:(h + 1) * HEAD_DIM] for h in range(GQA_GROUP)], axis=0)
        s_w[...] = _dg(qs, k_ref[0, 0], _NT)
        s = s_r[...]
        p = jnp.exp(s - jnp.max(s, axis=-1, keepdims=True))
        p_w[...] = p.astype(BF16)
        l_w[...] = jnp.sum(p, axis=-1, keepdims=True)
        o = _dg(p_r[...], v_ref[0, 0], _NN) / l_r[...]
        o_ref[0] = jnp.concatenate([o[h * tq:(h + 1) * tq] for h in range(GQA_GROUP)], axis=1).astype(BF16)

    @pl.when(i % 2 == 0)
    def _():
        step(s0, s1, p1, l1, p0, l0)

    @pl.when(i % 2 == 1)
    def _():
        step(s1, s0, p0, l0, p1, l1)


def _attention(q, k, v):
    B, L, _ = q.shape
    Tc = k.shape[2]
    tq = ATT_Q_TILE
    n = L // tq
    gw = GQA_GROUP * HEAD_DIM
    rows = GQA_GROUP * tq
    return pl.pallas_call(
        _attn_kernel,
        grid=(B, ATT_KV_HEADS, n + 2),
        in_specs=[pl.BlockSpec((1, tq, gw), lambda b, g, i: (b, jnp.minimum(i, n - 1), g)),
                  pl.BlockSpec((1, 1, Tc, HEAD_DIM), lambda b, g, i: (b, g, 0, 0)),
                  pl.BlockSpec((1, 1, Tc, HEAD_DIM), lambda b, g, i: (b, g, 0, 0))],
        out_specs=pl.BlockSpec((1, tq, gw), lambda b, g, i: (b, jnp.clip(i - 2, 0, n - 1), g)),
        out_shape=jax.ShapeDtypeStruct((B, L, D_ATT), BF16),
        scratch_shapes=[pltpu.VMEM((rows, Tc), F32), pltpu.VMEM((rows, Tc), F32),
                        pltpu.VMEM((rows, Tc), BF16), pltpu.VMEM((rows, Tc), BF16),
                        pltpu.VMEM((rows, 1), F32), pltpu.VMEM((rows, 1), F32)],
        compiler_params=_params(("arbitrary", "arbitrary", "arbitrary")),
        name="attention",
    )(q, k, v)


def _rwkv_prep_kernel(rw_ref, prev_ref, next_ref, mu_ref, w0_ref, dup_ref, a0_ref, aup_ref, gup_ref,
                      kk_ref, ka_ref, rk_ref, bones_ref,
                      r_o, v_o, kk_o, g_o, bonus_o, lw_o, kd_o, b_o, *, n_ctx, n_tok):
    i = pl.program_id(1)
    p = rw_ref[0]
    tm = p.shape[0]
    t0 = i * tm
    has_prev = jnp.where((t0 == 0) | (t0 == n_ctx), 0.0, 1.0)
    has_next = jnp.where((t0 + tm == n_ctx) | (t0 + tm == n_tok), 0.0, 1.0)
    row = lax.broadcasted_iota(jnp.int32, (tm, 1), 0)
    up = jnp.where(row == 0, prev_ref[0, 7:8, :] * has_prev, pltpu.roll(p, 1, 0))
    dn = jnp.where(row == tm - 1, next_ref[0, 0:1, :] * has_next, pltpu.roll(p, tm - 1, 0))
    s = p + (0.5 * (up + dn) - p) * mu_ref[...]

    D = D_RWKV
    r, k, v = s[:, :D], s[:, D:2 * D], s[:, 2 * D:3 * D]
    lora = s[:, 3 * D:3 * D + D_LORA]
    gl = s[:, 3 * D + D_LORA:]
    bones = bones_ref[...]

    g_o[0] = _mm(_sigmoid(gl), gup_ref[...], passes=3)
    kk = k * kk_ref[...]
    nrm = jnp.sqrt(_mm_exact_rhs(kk * kk, bones))
    kk = kk / jnp.maximum(nrm, L2_EPS)
    wt = jnp.tanh(lora)
    r_o[0] = r
    v_o[0] = v
    kk_o[0] = kk
    bonus_o[0] = _mm_exact_rhs(r * k * rk_ref[...], bones) * v
    for d in range(2):
        z = w0_ref[d:d + 1, :] + _mm(wt, dup_ref[d], passes=3)
        softplus_neg = jnp.maximum(-z, 0.0) + jnp.log(1.0 + jnp.exp(-jnp.abs(z)))
        w = -softplus_neg - 0.5
        lw_o[d, 0] = -jnp.exp(w)
        a = _sigmoid(a0_ref[d:d + 1, :] + _mm(lora, aup_ref[d], passes=3))
        kd_o[d, 0] = k * (1.0 + (a - 1.0) * ka_ref[...])
        b_o[d, 0] = kk * a


def _rwkv_prep(rw, n_ctx, mu, w0, dup, a0, aup, gup, k_k, k_a, r_k, bones):
    B, Tc, d_rw = rw.shape
    tm = ROW_TILE
    D = D_RWKV
    const2 = lambda b, i: (0, 0)
    const3 = lambda b, i: (0, 0, 0)
    tok = pl.BlockSpec((1, tm, D), lambda b, i: (b, i, 0))
    tok2 = pl.BlockSpec((2, 1, tm, D), lambda b, i: (0, b, i, 0))
    s1 = jax.ShapeDtypeStruct((B, Tc, D), F32)
    s2 = jax.ShapeDtypeStruct((2, B, Tc, D), F32)
    hb = tm // 8
    return pl.pallas_call(
        functools.partial(_rwkv_prep_kernel, n_ctx=n_ctx, n_tok=Tc),
        grid=(B, Tc // tm),
        in_specs=[pl.BlockSpec((1, tm, d_rw), lambda b, i: (b, i, 0)),
                  pl.BlockSpec((1, 8, d_rw), lambda b, i: (b, jnp.maximum(i * hb - 1, 0), 0)),
                  pl.BlockSpec((1, 8, d_rw), lambda b, i: (b, jnp.minimum((i + 1) * hb, Tc // 8 - 1), 0)),
                  pl.BlockSpec((1, d_rw), const2),
                  pl.BlockSpec((2, D), const2),
                  pl.BlockSpec((2, D_LORA, D), const3),
                  pl.BlockSpec((2, D), const2),
                  pl.BlockSpec((2, D_LORA, D), const3),
                  pl.BlockSpec((D_LORA, D), const2),
                  pl.BlockSpec((1, D), const2),
                  pl.BlockSpec((1, D), const2),
                  pl.BlockSpec((1, D), const2),
                  pl.BlockSpec(bones.shape, const2)],
        out_specs=[tok, tok, tok, tok, tok, tok2, tok2, tok2],
        out_shape=[s1, s1, s1, s1, s1, s2, s2, s2],
        compiler_params=_params(("arbitrary", "arbitrary")),
        name="rwkv_prep",
    )(rw, rw, rw, mu, w0, dup, a0, aup, gup, k_k, k_a, r_k, bones)


def _wkv_kernel(r_ref, v_ref, kk_ref, lw_ref, kd_ref, b_ref, y_ref, s_ref, *, reverse, n_heads, passes):
    c = pl.program_id(1)

    @pl.when(c == 0)
    def _():
        s_ref[...] = jnp.zeros_like(s_ref)

    C = r_ref.shape[1]
    N = RWKV_HEAD
    row = lax.broadcasted_iota(jnp.int32, (C, C), 0)
    col = lax.broadcasted_iota(jnp.int32, (C, C), 1)
    if reverse:
        strict = row < col
        incl = row <= col
    else:
        strict = row > col
        incl = row >= col
    tri = jnp.where(incl, 1.0, 0.0).astype(BF16)

    lw = lw_ref[0, 0]
    cl = _mm_exact_lhs(tri, lw)
    ecl = jnp.exp(cl)
    ecx = jnp.exp(cl - lw)
    eneg = jnp.exp(-cl)
    last = 0 if reverse else C - 1
    etot = ecl[last:last + 1, :]
    rt_all = r_ref[0] * ecl
    at_all = -(kk_ref[0] * ecx)
    kt_all = kd_ref[0, 0] * eneg
    bt_all = b_ref[0, 0] * eneg
    kh_all = kt_all * etot
    bh_all = bt_all * etot
    v_all = v_ref[0]

    sp = lambda x: _pieces(x, passes)
    heads = range(n_heads)
    hs = lambda t, h: tuple(p[:, h * N:(h + 1) * N] for p in t)
    at_p, rt_p, kt_p, bt_p, v_p = sp(at_all), sp(rt_all), sp(kt_all), sp(bt_all), sp(v_all)
    kh_p, bh_p = sp(kh_all), sp(bh_all)
    s0 = [s_ref[h] for h in heads]
    s0_p = [sp(s) for s in s0]
    ar = [tuple(jnp.concatenate([a, r], axis=0) for a, r in zip(hs(at_p, h), hs(rt_p, h))) for h in heads]
    g_b = [_mmp(ar[h], hs(bt_p, h), _NT) for h in heads]
    g_k = [_mmp(ar[h], hs(kt_p, h), _NT) for h in heads]
    l_ak = [sp(jnp.where(strict, g[:C], 0.0)) for g in g_k]
    u = [_mmp(hs(at_p, h), s0_p[h], _NT) + _mmp(l_ak[h], hs(v_p, h), _NN) for h in heads]
    pw = [sp(jnp.where(strict, g[:C], 0.0)) for g in g_b]
    n = 1
    while True:
        u = [u[h] + _mmp(pw[h], sp(u[h]), _NN) for h in heads]
        n *= 2
        if n >= C:
            break
        pw = [sp(_mmp(pw[h], pw[h], _NN)) for h in heads]
    u_p = [sp(x) for x in u]
    l_rb = [sp(jnp.where(incl, g[C:], 0.0)) for g in g_b]
    l_rk = [sp(jnp.where(incl, g[C:], 0.0)) for g in g_k]
    for h in heads:
        y = (_mmp(hs(rt_p, h), s0_p[h], _NT) + _mmp(l_rb[h], u_p[h], _NN)
             + _mmp(l_rk[h], hs(v_p, h), _NN))
        y_ref[0, :, h * N:(h + 1) * N] = y
    for h in heads:
        s_ref[h] = (s0[h] * etot[:, h * N:(h + 1) * N] + _mmp(u_p[h], hs(bh_p, h), _TN)
                    + _mmp(hs(v_p, h), hs(kh_p, h), _TN))


def _wkv_scan(r, v, kk, lw, kd, b, n_ctx, *, reverse, passes=3):
    B, T, D = r.shape
    C = WKV_CHUNK
    n_heads = D // RWKV_HEAD
    n_chunks = T // C
    n_cc = n_ctx // C
    d = 1 if reverse else 0

    def tok(c):
        if not reverse:
            return c
        return jnp.where(c < n_cc, n_cc - 1 - c, n_chunks - 1 + n_cc - c)

    shared = pl.BlockSpec((1, C, D), lambda bi, c: (bi, tok(c), 0))
    per_dir = pl.BlockSpec((1, 1, C, D), lambda bi, c: (d, bi, tok(c), 0))
    return pl.pallas_call(
        functools.partial(_wkv_kernel, reverse=reverse, n_heads=n_heads, passes=passes),
        grid=(B, n_chunks),
        in_specs=[shared, shared, shared, per_dir, per_dir, per_dir],
        out_specs=pl.BlockSpec((1, C, D), lambda bi, c: (bi, tok(c), 0)),
        out_shape=jax.ShapeDtypeStruct((B, T, D), F32),
        scratch_shapes=[pltpu.VMEM((n_heads, RWKV_HEAD, RWKV_HEAD), F32)],
        compiler_params=_params(("arbitrary", "arbitrary")),
        name="wkv_rev" if reverse else "wkv_fwd",
    )(r, v, kk, lw, kd, b)


def _post_kernel(yf_ref, yb_ref, bonus_ref, g_ref, att_ref, x_ref, gt_ref, sh_ref, sc_ref, wo_ref,
                 lxg_ref, lxb_ref, l1g_ref, l1b_ref, rwt_ref, bones_ref, x1_ref, h2_ref, aff_ref):
    bones = bones_ref[...]
    y = yf_ref[0] + yb_ref[0]
    inv = 1.0 / RWKV_HEAD
    mu = _mm_exact_rhs(y, bones) * inv
    yc = y - mu
    var = _mm_exact_rhs(yc * yc, bones) * inv
    yn = yc * lax.rsqrt(var + GN_EPS) * lxg_ref[...] + lxb_ref[...]
    rw_out = (yn + bonus_ref[0]) * g_ref[0]
    mix = _dg(att_ref[0], wo_ref[:D_ATT, :], _NN) + _dg(rw_out.astype(BF16), wo_ref[D_ATT:, :], _NN)
    x1 = _layer_norm(ALPHA * x_ref[0] + gt_ref[0] * mix) * l1g_ref[...] + l1b_ref[...]
    x1_ref[0] = x1
    h2 = _layer_norm(x1) * (1.0 + sc_ref[0]) + sh_ref[0]
    h2_ref[0] = h2.astype(BF16)
    logits = _mm(rwt_ref[...], h2, _NT, passes=6)
    e = jnp.exp(logits - jnp.max(logits, axis=0, keepdims=True))
    aff_ref[0] = e / jnp.sum(e, axis=0, keepdims=True)


def _post(yf, yb, bonus, g, att, x, gt1, sh2, sc2, w_out_bf16, lnx_g, lnx_b, ln1_g, ln1_b, router_wt,
          bones, n_ctx):
    B, L, D = x.shape
    tm = ROW_TILE
    nct = n_ctx // tm
    E = router_wt.shape[0]
    const = lambda b, i: (0, 0)
    cat = pl.BlockSpec((1, tm, D_RWKV), lambda b, i: (b, i + nct, 0))
    mod = pl.BlockSpec((1, 1, D), lambda b, i: (b, 0, 0))
    vec = lambda n: pl.BlockSpec((1, n), const)
    return pl.pallas_call(
        _post_kernel,
        grid=(B, L // tm),
        in_specs=[cat, cat, cat, cat,
                  pl.BlockSpec((1, tm, D_ATT), lambda b, i: (b, i, 0)),
                  pl.BlockSpec((1, tm, D), lambda b, i: (b, i, 0)),
                  mod, mod, mod,
                  pl.BlockSpec(w_out_bf16.shape, const),
                  vec(D_RWKV), vec(D_RWKV), vec(D), vec(D),
                  pl.BlockSpec((E, D), const),
                  pl.BlockSpec(bones.shape, const)],
        out_specs=[pl.BlockSpec((1, tm, D), lambda b, i: (b, i, 0)),
                   pl.BlockSpec((1, tm, D), lambda b, i: (b, i, 0)),
                   pl.BlockSpec((1, E, tm), lambda b, i: (b, 0, i))],
        out_shape=[jax.ShapeDtypeStruct((B, L, D), F32),
                   jax.ShapeDtypeStruct((B, L, D), BF16),
                   jax.ShapeDtypeStruct((B, E, L), F32)],
        compiler_params=_params(("arbitrary", "arbitrary")),
        name="post_mix",
    )(yf, yb, bonus, g, att, x, gt1, sh2, sc2, w_out_bf16, lnx_g, lnx_b, ln1_g, ln1_b, router_wt, bones)


def _cumsum_lanes(x):
    n = x.shape[1]
    lane = lax.broadcasted_iota(jnp.int32, x.shape, 1)
    s = 1
    while s < n:
        x = x + jnp.where(lane >= s, pltpu.roll(x, s, 1), 0)
        s *= 2
    return x


def _route_kernel(aff_ref, pos_ref, gate_ref, *, cap):
    aff = aff_ref[0]
    E = aff.shape[0]
    count = lambda m: jnp.sum(jnp.where(m, 1, 0), axis=1, keepdims=True)
    thr_bits = jnp.zeros((E, 1), jnp.int32)
    for bit in range(29, -1, -1):
        cand = thr_bits | (1 << bit)
        cand_f = lax.bitcast_convert_type(cand, F32)
        thr_bits = jnp.where(count(aff >= cand_f) >= cap, cand, thr_bits)
    thr = lax.bitcast_convert_type(thr_bits, F32)
    above = aff > thr
    tie = aff == thr
    need = cap - count(above)
    tie_rank = _cumsum_lanes(jnp.where(tie, 1, 0))
    sel = above | (tie & (tie_rank <= need))
    slot = _cumsum_lanes(jnp.where(sel, 1, 0)) - 1
    pos = jnp.where(sel, slot, -1)
    gate = jnp.where(sel, aff, 0.0)
    for e in range(E):
        pos_ref[0, e] = pos[e:e + 1, :]
        gate_ref[0, e] = gate[e:e + 1, :]


def _route(aff, cap):
    B, E, L = aff.shape
    out = pl.BlockSpec((1, E, 1, L), lambda b: (b, 0, 0, 0))
    return pl.pallas_call(
        functools.partial(_route_kernel, cap=cap),
        grid=(B,),
        in_specs=[pl.BlockSpec((1, E, L), lambda b: (b, 0, 0))],
        out_specs=[out, out],
        out_shape=[jax.ShapeDtypeStruct((B, E, 1, L), jnp.int32),
                   jax.ShapeDtypeStruct((B, E, 1, L), F32)],
        compiler_params=_params(("arbitrary",)),
        name="route",
    )(aff)


def _expert_kernel(h_ref, pos_ref, gate_ref, wg_ref, wu_ref, wd_ref, y_ref, *, cap):
    pos = pos_ref[0, 0]
    L = pos.shape[1]
    slot = lax.broadcasted_iota(jnp.int32, (cap, L), 0)
    hit = pos == slot
    onehot = jnp.where(hit, 1.0, 0.0).astype(BF16)
    gate = jnp.sum(jnp.where(hit, gate_ref[0, 0], 0.0), axis=1, keepdims=True)
    xin = _dg(onehot, h_ref[0], _NN).astype(BF16)
    hg = _dg(xin, wg_ref[0], _NN)
    hu = _dg(xin, wu_ref[0], _NN)
    hid = (hg * _sigmoid(hg) * hu).astype(BF16)
    y_ref[0, 0] = (_dg(hid, wd_ref[0], _NN) * gate).astype(BF16)


def _experts(h2, pos, gate, wg, wu, wd, cap):
    B, L, D = h2.shape
    E, _, F = wg.shape
    rowv = pl.BlockSpec((1, 1, 1, L), lambda b, e: (b, e, 0, 0))
    return pl.pallas_call(
        functools.partial(_expert_kernel, cap=cap),
        grid=(B, E),
        in_specs=[pl.BlockSpec((1, L, D), lambda b, e: (b, 0, 0)),
                  rowv, rowv,
                  pl.BlockSpec((1, D, F), lambda b, e: (e, 0, 0)),
                  pl.BlockSpec((1, D, F), lambda b, e: (e, 0, 0)),
                  pl.BlockSpec((1, F, D), lambda b, e: (e, 0, 0))],
        out_specs=pl.BlockSpec((1, 1, cap, D), lambda b, e: (b, e, 0, 0)),
        out_shape=jax.ShapeDtypeStruct((B, E, cap, D), BF16),
        compiler_params=_params(("arbitrary", "arbitrary")),
        name="experts",
    )(h2, pos, gate, wg, wu, wd)


def _combine_kernel(pos_ref, y_ref, x1_ref, gt_ref, lg_ref, lb_ref, o_ref, acc_ref, *, cap):
    e = pl.program_id(2)

    @pl.when(e == 0)
    def _():
        acc_ref[...] = jnp.zeros_like(acc_ref)

    pos = pos_ref[0, 0]
    slot = lax.broadcasted_iota(jnp.int32, (cap, pos.shape[1]), 0)
    onehot = jnp.where(pos == slot, 1.0, 0.0).astype(BF16)
    acc_ref[...] += _dg(onehot, y_ref[0, 0], _TN)

    @pl.when(e == pl.num_programs(2) - 1)
    def _():
        z = ALPHA * x1_ref[0] + gt_ref[0] * acc_ref[...]
        o_ref[0] = _layer_norm(z) * lg_ref[...] + lb_ref[...]


def _combine(pos, y, x1, gt2, ln2_g, ln2_b, cap):
    B, L, D = x1.shape
    E = y.shape[1]
    tl = min(COMBINE_TILE, L)
    const = lambda b, i, e: (0, 0)
    return pl.pallas_call(
        functools.partial(_combine_kernel, cap=cap),
        grid=(B, L // tl, E),
        in_specs=[pl.BlockSpec((1, 1, 1, tl), lambda b, i, e: (b, e, 0, i)),
                  pl.BlockSpec((1, 1, cap, D), lambda b, i, e: (b, e, 0, 0)),
                  pl.BlockSpec((1, tl, D), lambda b, i, e: (b, i, 0)),
                  pl.BlockSpec((1, 1, D), lambda b, i, e: (b, 0, 0)),
                  pl.BlockSpec((1, D), const),
                  pl.BlockSpec((1, D), const)],
        out_specs=pl.BlockSpec((1, tl, D), lambda b, i, e: (b, i, 0)),
        out_shape=jax.ShapeDtypeStruct((B, L, D), F32),
        scratch_shapes=[pltpu.VMEM((tl, D), F32)],
        compiler_params=_params(("arbitrary", "arbitrary", "arbitrary")),
        name="combine",
    )(pos, y, x1, gt2, ln2_g, ln2_b)


def _rope_tables(n_ctx, n_lat):
    lane = jnp.arange(128)
    j = lane % HEAD_DIM
    axis = j // (2 * ROPE_PAIRS)
    upper = (j % (2 * ROPE_PAIRS)) >= ROPE_PAIRS
    inv = ROPE_THETA ** (-(j % ROPE_PAIRS).astype(F32) / ROPE_PAIRS)
    t = jnp.arange(n_lat)
    coord = jnp.where(axis[None, :] == 0, (t // GRID_W)[:, None], (t % GRID_W)[:, None]).astype(F32)
    ang = coord * inv[None, :]
    cos, sin = jnp.cos(ang), jnp.sin(ang)
    s_lo = jnp.where(upper[None, :], 0.0, -sin)
    s_hi = jnp.where(upper[None, :], sin, 0.0)
    pad = lambda a, fill: jnp.concatenate([jnp.full((n_ctx, 128), fill, F32), a], axis=0)
    return pad(cos, 1.0), pad(s_lo, 0.0), pad(s_hi, 0.0)


def _head_ones(n):
    h = jnp.arange(n) // HEAD_DIM
    return (h[:, None] == h[None, :]).astype(BF16)


def kernel(x, c, ctx, c_ctx, w_ada, b_ada, w_in, q_gain, k_gain, tshift_mu, decay_w0, decay_up, iclr_a0,
           iclr_up, gate_up, k_k, k_a, r_k, lnx_g, lnx_b, w_out, ln1_g, ln1_b, router_w, exp_w_gate,
           exp_w_up, exp_w_down, ln2_g, ln2_b):
    B, L, D = x.shape
    n_ctx = ctx.shape[1]
    li = 0
    row = lambda a: a.reshape(1, -1)

    n_rows = -(-(B + 1) // 8) * 8
    c_rows = jnp.concatenate([c, c_ctx[None, :], jnp.zeros((n_rows - B - 1, D), F32)], axis=0)
    mod = _ada_mod(c_rows, w_ada[li], b_ada[li]).reshape(n_rows, N_MOD, 1, D)
    sh1, sc1, gt1, sh2, sc2, gt2 = (mod[:, m] for m in range(N_MOD))

    bones = _head_ones(D_RWKV)
    cos, s_lo, s_hi = _rope_tables(n_ctx, L)
    q, k, v, rw = _inproj(x, ctx, sh1, sc1, w_in[li].astype(BF16), bones,
                          row(jnp.tile(q_gain[li], ATT_HEADS)), row(jnp.tile(k_gain[li], ATT_KV_HEADS)),
                          cos, s_lo, s_hi)
    att = _attention(q, k, v)

    zpad = jnp.zeros((2, D_LORA // 2, D_RWKV), F32)
    dup = jnp.concatenate([decay_up[li], zpad], axis=1)
    aup = jnp.concatenate([zpad, iclr_up[li]], axis=1)
    r, vr, kk, g, bonus, lw, kd, bv = _rwkv_prep(
        rw, n_ctx, row(tshift_mu[li]), decay_w0[li], dup, iclr_a0[li], aup, gate_up[li],
        row(k_k[li]), row(k_a[li]), row(r_k[li]), bones)
    yf = _wkv_scan(r, vr, kk, lw, kd, bv, n_ctx, reverse=False)
    yb = _wkv_scan(r, vr, kk, lw, kd, bv, n_ctx, reverse=True)

    x1, h2, aff = _post(yf, yb, bonus, g, att, x, gt1, sh2, sc2, w_out[li].astype(BF16),
                        row(lnx_g[li]), row(lnx_b[li]), row(ln1_g[li]), row(ln1_b[li]),
                        router_w[li].T, bones, n_ctx)
    cap = CAPACITY_FACTOR * L // N_EXPERTS
    pos, gate = _route(aff, cap)
    y = _experts(h2, pos, gate, exp_w_gate[li].astype(BF16), exp_w_up[li].astype(BF16),
                 exp_w_down[li].astype(BF16), cap)
    return _combine(pos, y, x1, gt2, row(ln2_g[li]), row(ln2_b[li]), cap)
```

```python
import functools

import jax
import jax.numpy as jnp
from jax import lax
from jax.experimental import pallas as pl
from jax.experimental.pallas import tpu as pltpu

F32 = jnp.float32
BF16 = jnp.bfloat16

HEAD_DIM = 64
ATT_HEADS = 8
ATT_KV_HEADS = 2
GQA_GROUP = ATT_HEADS // ATT_KV_HEADS
D_ATT = ATT_HEADS * HEAD_DIM
D_ATT_KV = ATT_KV_HEADS * HEAD_DIM
ATT_SCALE = HEAD_DIM ** -0.5
LOG2_E = 1.4426950408889634
ROPE_THETA = 10000.0
ROPE_PAIRS = HEAD_DIM // 4
GRID_W = 64
RWKV_HEAD = 64
D_RWKV = 512
D_LORA = 128
N_EXPERTS = 16
CAPACITY_FACTOR = 2
N_MOD = 6
LN_EPS = 1e-5
RMS_EPS = 1e-6
GN_EPS = 64e-5
L2_EPS = 1e-12
DEPTH = 1
ALPHA = (2.0 * DEPTH) ** 0.25

WKV_CHUNK = 64
ROW_TILE = 256
ATT_Q_TILE = 128
ATT_K_TILE = 256
COMBINE_TILE = 1024
VMEM_LIMIT = 48 * 1024 * 1024


def _params(sem):
    return pltpu.CompilerParams(dimension_semantics=sem, vmem_limit_bytes=VMEM_LIMIT)


def _split3(x):
    hi = x.astype(BF16)
    r1 = x - hi.astype(F32)
    mid = r1.astype(BF16)
    lo = (r1 - mid.astype(F32)).astype(BF16)
    return hi, mid, lo


def _split2(x):
    hi = x.astype(BF16)
    lo = (x - hi.astype(F32)).astype(BF16)
    return hi, lo


def _dg(a, b, dims):
    return lax.dot_general(a, b, (dims, ((), ())), preferred_element_type=F32)


_NN = ((1,), (0,))
_NT = ((1,), (1,))
_TN = ((0,), (0,))


def _mm(a, b, dims=_NN, passes=3):
    if passes == 1:
        return _dg(a.astype(BF16), b.astype(BF16), dims)
    if passes == 3:
        ah, al = _split2(a)
        bh, bl = _split2(b)
        return _dg(ah, bh, dims) + (_dg(ah, bl, dims) + _dg(al, bh, dims))
    ah, am, al = _split3(a)
    bh, bm, bl = _split3(b)
    return (_dg(ah, bh, dims) + (_dg(ah, bm, dims) + _dg(am, bh, dims))
            + (_dg(am, bm, dims) + _dg(ah, bl, dims) + _dg(al, bh, dims)))


def _pieces(x, passes):
    return (x.astype(BF16),) if passes == 1 else _split2(x)


def _mmp(a, b, dims):
    out = _dg(a[0], b[0], dims)
    if len(a) > 1 and len(b) > 1:
        return out + (_dg(a[0], b[1], dims) + _dg(a[1], b[0], dims))
    if len(a) > 1:
        return out + _dg(a[1], b[0], dims)
    if len(b) > 1:
        return out + _dg(a[0], b[1], dims)
    return out


def _mm_exact_lhs(a_bf16, b, dims=_NN):
    bh, bm, bl = _split3(b)
    return _dg(a_bf16, bh, dims) + (_dg(a_bf16, bm, dims) + _dg(a_bf16, bl, dims))


def _mm_exact_rhs(a, b_bf16, dims=_NN):
    ah, am, al = _split3(a)
    return _dg(ah, b_bf16, dims) + (_dg(am, b_bf16, dims) + _dg(al, b_bf16, dims))


def _sigmoid(x):
    return 1.0 / (1.0 + jnp.exp(-x))


def _layer_norm(x):
    mu = jnp.mean(x, axis=-1, keepdims=True)
    xc = x - mu
    var = jnp.mean(xc * xc, axis=-1, keepdims=True)
    return xc * lax.rsqrt(var + LN_EPS)


def _ada_kernel(c_ref, w_ref, b_ref, o_ref):
    c = c_ref[...]
    o_ref[...] = _mm(c * _sigmoid(c), w_ref[...], passes=6) + b_ref[...]


def _ada_mod(c_rows, w_ada, b_ada):
    R, D = c_rows.shape
    N = w_ada.shape[1]
    tn = 1024
    return pl.pallas_call(
        _ada_kernel,
        grid=(N // tn,),
        in_specs=[pl.BlockSpec((R, D), lambda j: (0, 0)),
                  pl.BlockSpec((D, tn), lambda j: (0, j)),
                  pl.BlockSpec((1, tn), lambda j: (0, j))],
        out_specs=pl.BlockSpec((R, tn), lambda j: (0, j)),
        out_shape=jax.ShapeDtypeStruct((R, N), F32),
        compiler_params=_params(("arbitrary",)),
        name="ada_mod",
    )(c_rows, w_ada, b_ada.reshape(1, N))


def _rope(t, cos, sin_lo, sin_hi, reps):
    w = t.shape[-1]
    tile = lambda a: jnp.concatenate([a] * reps, axis=1) if reps > 1 else a
    return (t * tile(cos) + pltpu.roll(t, w - ROPE_PAIRS, 1) * tile(sin_lo)
            + pltpu.roll(t, ROPE_PAIRS, 1) * tile(sin_hi))


def _inproj_kernel(x_ref, ctx_ref, sh_ref, sc_ref, w_ref, bones_ref, qg_ref, kg_ref,
                   cos_ref, slo_ref, shi_ref, q_ref, k_ref, v_ref, rw_ref, *, n_ctx_tiles):
    i = pl.program_id(1)
    x = jnp.where(i < n_ctx_tiles, ctx_ref[0], x_ref[0])
    h = _layer_norm(x) * (1.0 + sc_ref[0]) + sh_ref[0]
    p = _dg(h.astype(BF16), w_ref[...], _NN)
    bones = bones_ref[...]
    cos, slo, shi = cos_ref[...], slo_ref[...], shi_ref[...]

    q = p[:, :D_ATT]
    ss = _mm_exact_rhs(q * q, bones)
    q = q * lax.rsqrt(ss * (1.0 / HEAD_DIM) + RMS_EPS) * qg_ref[...]
    q = _rope(q, cos, slo, shi, D_ATT // 128) * (ATT_SCALE * LOG2_E)
    q_ref[0] = q.astype(BF16)

    k = p[:, D_ATT:D_ATT + D_ATT_KV]
    ss = _mm_exact_rhs(k * k, bones[:D_ATT_KV, :D_ATT_KV])
    k = k * lax.rsqrt(ss * (1.0 / HEAD_DIM) + RMS_EPS) * kg_ref[...]
    k = _rope(k, cos, slo, shi, D_ATT_KV // 128).astype(BF16)
    v = p[:, D_ATT + D_ATT_KV:D_ATT + 2 * D_ATT_KV].astype(BF16)
    for g in range(ATT_KV_HEADS):
        k_ref[0, g] = k[:, g * HEAD_DIM:(g + 1) * HEAD_DIM]
        v_ref[0, g] = v[:, g * HEAD_DIM:(g + 1) * HEAD_DIM]
    rw_ref[0] = p[:, D_ATT + 2 * D_ATT_KV:]


def _inproj(x, ctx, shift, scale, w_in_bf16, bones, qg, kg, cos, slo, shi):
    B, L, D = x.shape
    n_ctx = ctx.shape[1]
    tm = ROW_TILE
    nct = n_ctx // tm
    Tc = n_ctx + L
    d_in = w_in_bf16.shape[1]
    d_rw = d_in - D_ATT - 2 * D_ATT_KV
    mod_row = lambda b, i: (jnp.where(i < nct, B, b), 0, 0)
    const = lambda b, i: (0, 0)
    return pl.pallas_call(
        functools.partial(_inproj_kernel, n_ctx_tiles=nct),
        grid=(B, Tc // tm),
        in_specs=[pl.BlockSpec((1, tm, D), lambda b, i: (b, jnp.maximum(i - nct, 0), 0)),
                  pl.BlockSpec((1, tm, D), lambda b, i: (b, jnp.minimum(i, nct - 1), 0)),
                  pl.BlockSpec((1, 1, D), mod_row),
                  pl.BlockSpec((1, 1, D), mod_row),
                  pl.BlockSpec((D, d_in), const),
                  pl.BlockSpec(bones.shape, const),
                  pl.BlockSpec((1, D_ATT), const),
                  pl.BlockSpec((1, D_ATT_KV), const),
                  pl.BlockSpec((tm, 128), lambda b, i: (i, 0)),
                  pl.BlockSpec((tm, 128), lambda b, i: (i, 0)),
                  pl.BlockSpec((tm, 128), lambda b, i: (i, 0))],
        out_specs=[pl.BlockSpec((1, tm, D_ATT), lambda b, i: (b, jnp.maximum(i - nct, 0), 0)),
                   pl.BlockSpec((1, ATT_KV_HEADS, tm, HEAD_DIM), lambda b, i: (b, 0, i, 0)),
                   pl.BlockSpec((1, ATT_KV_HEADS, tm, HEAD_DIM), lambda b, i: (b, 0, i, 0)),
                   pl.BlockSpec((1, tm, d_rw), lambda b, i: (b, i, 0))],
        out_shape=[jax.ShapeDtypeStruct((B, L, D_ATT), BF16),
                   jax.ShapeDtypeStruct((B, ATT_KV_HEADS, Tc, HEAD_DIM), BF16),
                   jax.ShapeDtypeStruct((B, ATT_KV_HEADS, Tc, HEAD_DIM), BF16),
                   jax.ShapeDtypeStruct((B, Tc, d_rw), F32)],
        compiler_params=_params(("arbitrary", "arbitrary")),
        name="inproj",
    )(x, ctx, shift, scale, w_in_bf16, bones, qg, kg, cos, slo, shi)


def _attn_kernel(q_ref, k_ref, vt_ref, o_ref, s0, s1, p0, p1, m0, m1):
    i = pl.program_id(2)
    nk, tk, cols = s0.shape

    @pl.when(i == 0)
    def _():
        for ref in (s0, s1, m0, m1):
            ref[...] = jnp.zeros_like(ref)
        p0[...] = jnp.ones_like(p0)
        p1[...] = jnp.ones_like(p1)

    def step(s_w, m_w, s_r, m_r, p_w, p_r):
        q = q_ref[0]
        tq = q.shape[0]
        qs = jnp.concatenate([q[:, h * HEAD_DIM:(h + 1) * HEAD_DIM] for h in range(GQA_GROUP)], axis=0)
        m_prev = m_r[0:1, :]

        def body(j, carry):
            m_run, acc = carry
            off = pl.multiple_of(j * tk, tk)
            s_new = _dg(k_ref[0, 0, pl.ds(off, tk), :], qs, _NT)
            s_w[j] = s_new
            m_run = jnp.maximum(m_run, jnp.max(s_new.reshape(tk // 8, 8, cols), axis=0))
            p_w[j] = jnp.exp2(s_r[j] - m_prev).astype(BF16)
            acc = acc + _dg(vt_ref[0, 0, j], p_r[j], _NN)
            return m_run, acc

        init = (jnp.full((8, cols), -jnp.inf, F32), jnp.zeros((vt_ref.shape[3], cols), F32))
        m_run, acc = lax.fori_loop(0, nk, body, init, unroll=8)
        m_w[...] = jnp.broadcast_to(jnp.max(m_run, axis=0, keepdims=True), m_w.shape)
        o = acc[:HEAD_DIM] / acc[HEAD_DIM:HEAD_DIM + 1]
        o_ref[0] = jnp.concatenate([o[:, h * tq:(h + 1) * tq].T for h in range(GQA_GROUP)],
                                   axis=1).astype(BF16)

    @pl.when(i % 2 == 0)
    def _():
        step(s0, m0, s1, m1, p1, p0)

    @pl.when(i % 2 == 1)
    def _():
        step(s1, m1, s0, m0, p0, p1)


def _attention(q, k, vt):
    B, L, _ = q.shape
    Tc = k.shape[2]
    tq = ATT_Q_TILE
    n = L // tq
    gw = GQA_GROUP * HEAD_DIM
    cols = GQA_GROUP * tq
    tk = ATT_K_TILE
    nk = Tc // tk
    return pl.pallas_call(
        _attn_kernel,
        grid=(B, ATT_KV_HEADS, n + 2),
        in_specs=[pl.BlockSpec((1, tq, gw), lambda b, g, i: (b, jnp.minimum(i, n - 1), g)),
                  pl.BlockSpec((1, 1, Tc, HEAD_DIM), lambda b, g, i: (b, g, 0, 0)),
                  pl.BlockSpec((1, 1, nk, vt.shape[3], tk), lambda b, g, i: (b, g, 0, 0, 0))],
        out_specs=pl.BlockSpec((1, tq, gw), lambda b, g, i: (b, jnp.clip(i - 2, 0, n - 1), g)),
        out_shape=jax.ShapeDtypeStruct((B, L, D_ATT), BF16),
        scratch_shapes=[pltpu.VMEM((nk, tk, cols), F32), pltpu.VMEM((nk, tk, cols), F32),
                        pltpu.VMEM((nk, tk, cols), BF16), pltpu.VMEM((nk, tk, cols), BF16),
                        pltpu.VMEM((8, cols), F32), pltpu.VMEM((8, cols), F32)],
        compiler_params=_params(("arbitrary", "arbitrary", "arbitrary")),
        name="attention",
    )(q, k, vt)


def _rwkv_prep_kernel(rw_ref, prev_ref, next_ref, mu_ref, w0_ref, dup_ref, a0_ref, aup_ref, gup_ref,
                      kk_ref, ka_ref, rk_ref, bones_ref,
                      r_o, v_o, kk_o, g_o, bonus_o, lw_o, kd_o, b_o, *, n_ctx, n_tok):
    i = pl.program_id(1)
    p = rw_ref[0]
    tm = p.shape[0]
    t0 = i * tm
    has_prev = jnp.where((t0 == 0) | (t0 == n_ctx), 0.0, 1.0)
    has_next = jnp.where((t0 + tm == n_ctx) | (t0 + tm == n_tok), 0.0, 1.0)
    row = lax.broadcasted_iota(jnp.int32, (tm, 1), 0)
    up = jnp.where(row == 0, prev_ref[0, 7:8, :] * has_prev, pltpu.roll(p, 1, 0))
    dn = jnp.where(row == tm - 1, next_ref[0, 0:1, :] * has_next, pltpu.roll(p, tm - 1, 0))
    s = p + (0.5 * (up + dn) - p) * mu_ref[...]

    D = D_RWKV
    r, k, v = s[:, :D], s[:, D:2 * D], s[:, 2 * D:3 * D]
    lora = s[:, 3 * D:3 * D + D_LORA]
    gl = s[:, 3 * D + D_LORA:]
    bones = bones_ref[...]

    g_o[0] = _mm(_sigmoid(gl), gup_ref[...], passes=3)
    kk = k * kk_ref[...]
    nrm = jnp.sqrt(_mm_exact_rhs(kk * kk, bones))
    kk = kk / jnp.maximum(nrm, L2_EPS)
    wt = jnp.tanh(lora)
    r_o[0] = r
    v_o[0] = v
    kk_o[0] = kk
    bonus_o[0] = _mm_exact_rhs(r * k * rk_ref[...], bones) * v
    for d in range(2):
        z = w0_ref[d:d + 1, :] + _mm(wt, dup_ref[d], passes=3)
        softplus_neg = jnp.maximum(-z, 0.0) + jnp.log(1.0 + jnp.exp(-jnp.abs(z)))
        w = -softplus_neg - 0.5
        lw_o[d, 0] = -jnp.exp(w)
        a = _sigmoid(a0_ref[d:d + 1, :] + _mm(lora, aup_ref[d], passes=3))
        kd_o[d, 0] = k * (1.0 + (a - 1.0) * ka_ref[...])
        b_o[d, 0] = kk * a


def _rwkv_prep(rw, n_ctx, mu, w0, dup, a0, aup, gup, k_k, k_a, r_k, bones):
    B, Tc, d_rw = rw.shape
    tm = ROW_TILE
    D = D_RWKV
    const2 = lambda b, i: (0, 0)
    const3 = lambda b, i: (0, 0, 0)
    tok = pl.BlockSpec((1, tm, D), lambda b, i: (b, i, 0))
    tok2 = pl.BlockSpec((2, 1, tm, D), lambda b, i: (0, b, i, 0))
    s1 = jax.ShapeDtypeStruct((B, Tc, D), F32)
    s2 = jax.ShapeDtypeStruct((2, B, Tc, D), F32)
    hb = tm // 8
    return pl.pallas_call(
        functools.partial(_rwkv_prep_kernel, n_ctx=n_ctx, n_tok=Tc),
        grid=(B, Tc // tm),
        in_specs=[pl.BlockSpec((1, tm, d_rw), lambda b, i: (b, i, 0)),
                  pl.BlockSpec((1, 8, d_rw), lambda b, i: (b, jnp.maximum(i * hb - 1, 0), 0)),
                  pl.BlockSpec((1, 8, d_rw), lambda b, i: (b, jnp.minimum((i + 1) * hb, Tc // 8 - 1), 0)),
                  pl.BlockSpec((1, d_rw), const2),
                  pl.BlockSpec((2, D), const2),
                  pl.BlockSpec((2, D_LORA, D), const3),
                  pl.BlockSpec((2, D), const2),
                  pl.BlockSpec((2, D_LORA, D), const3),
                  pl.BlockSpec((D_LORA, D), const2),
                  pl.BlockSpec((1, D), const2),
                  pl.BlockSpec((1, D), const2),
                  pl.BlockSpec((1, D), const2),
                  pl.BlockSpec(bones.shape, const2)],
        out_specs=[tok, tok, tok, tok, tok, tok2, tok2, tok2],
        out_shape=[s1, s1, s1, s1, s1, s2, s2, s2],
        compiler_params=_params(("arbitrary", "arbitrary")),
        name="rwkv_prep",
    )(rw, rw, rw, mu, w0, dup, a0, aup, gup, k_k, k_a, r_k, bones)


def _wkv_kernel(*refs, n_heads):
    ins, (y_refs, s_ref) = (refs[:6], refs[6:12]), (refs[12:14], refs[14])
    c = pl.program_id(1)

    @pl.when(c == 0)
    def _():
        s_ref[...] = jnp.zeros_like(s_ref)

    C = ins[0][0].shape[1]
    N = RWKV_HEAD
    row = lax.broadcasted_iota(jnp.int32, (C, C), 0)
    col = lax.broadcasted_iota(jnp.int32, (C, C), 1)
    cat = lambda xs, ys: tuple(jnp.concatenate([x, y], axis=0) for x, y in zip(xs, ys))

    chains = []
    for d, (r_ref, v_ref, kk_ref, lw_ref, kd_ref, b_ref) in enumerate(ins):
        strict, incl = (row < col, row <= col) if d else (row > col, row >= col)
        lw = lw_ref[0, 0]
        cl = _mm_exact_lhs(jnp.where(incl, 1.0, 0.0).astype(BF16), lw)
        ecl = jnp.exp(cl)
        eneg = jnp.exp(-cl)
        last = 0 if d else C - 1
        etot = ecl[last:last + 1, :]
        kt = kd_ref[0, 0] * eneg
        bt = b_ref[0, 0] * eneg
        full = dict(at=_split2(-(kk_ref[0] * jnp.exp(cl - lw))), rt=_split2(r_ref[0] * ecl),
                    kt=_split2(kt), bt=_split2(bt), v=_split2(v_ref[0]),
                    kh=_split2(kt * etot), bh=_split2(bt * etot))
        for h in range(n_heads):
            ch = {k: tuple(p[:, h * N:(h + 1) * N] for p in val) for k, val in full.items()}
            ch.update(d=d, h=h, strict=strict, incl=incl, etot=etot[:, h * N:(h + 1) * N], s0=s_ref[d, h])
            chains.append(ch)

    for ch in chains:
        ch["ar"] = cat(ch["at"], ch["rt"])
        ch["s0p"] = _split2(ch["s0"])
    for ch in chains:
        ch["gb"] = _mmp(ch["ar"], ch["bt"], _NT)
        ch["gk"] = _mmp(ch["ar"], ch["kt"], _NT)
        ch["zs"] = _mmp(ch["ar"], ch["s0p"], _NT)
    for ch in chains:
        gk = ch["gk"]
        lk = jnp.concatenate([jnp.where(ch["strict"], gk[:C], 0.0), jnp.where(ch["incl"], gk[C:], 0.0)], axis=0)
        ch["lv"] = _mmp(_split2(lk), ch["v"], _NN)
        ch["u"] = ch["zs"][:C] + ch["lv"][:C]
        ch["pw"] = _split2(jnp.where(ch["strict"], ch["gb"][:C], 0.0))
    n = 1
    while True:
        for ch in chains:
            ch["u"] = ch["u"] + _mmp(ch["pw"], _split2(ch["u"]), _NN)
        n *= 2
        if n >= C:
            break
        for ch in chains:
            ch["pw"] = _split2(_mmp(ch["pw"], ch["pw"], _NN))
    for ch in chains:
        ch["up"] = _split2(ch["u"])
        l_rb = _split2(jnp.where(ch["incl"], ch["gb"][C:], 0.0))
        y = ch["zs"][C:] + ch["lv"][C:] + _mmp(l_rb, ch["up"], _NN)
        y_refs[ch["d"]][0, :, ch["h"] * N:(ch["h"] + 1) * N] = y
    for ch in chains:
        upd = _mmp(cat(ch["up"], ch["v"]), cat(ch["bh"], ch["kh"]), _TN)
        s_ref[ch["d"], ch["h"]] = ch["s0"] * ch["etot"] + upd


def _wkv_scan(r, v, kk, lw, kd, b, n_ctx):
    B, T, D = r.shape
    C = WKV_CHUNK
    n_heads = D // RWKV_HEAD
    n_chunks = T // C
    n_cc = n_ctx // C
    tok = (lambda c: c,
           lambda c: jnp.where(c < n_cc, n_cc - 1 - c, n_chunks - 1 + n_cc - c))
    in_specs, args = [], []
    for d in range(2):
        shared = pl.BlockSpec((1, C, D), lambda bi, c, d=d: (bi, tok[d](c), 0))
        per_dir = pl.BlockSpec((1, 1, C, D), lambda bi, c, d=d: (d, bi, tok[d](c), 0))
        in_specs += [shared, shared, shared, per_dir, per_dir, per_dir]
        args += [r, v, kk, lw, kd, b]
    y = jax.ShapeDtypeStruct((B, T, D), F32)
    return pl.pallas_call(
        functools.partial(_wkv_kernel, n_heads=n_heads),
        grid=(B, n_chunks),
        in_specs=in_specs,
        out_specs=[pl.BlockSpec((1, C, D), lambda bi, c, d=d: (bi, tok[d](c), 0)) for d in range(2)],
        out_shape=[y, y],
        scratch_shapes=[pltpu.VMEM((2, n_heads, RWKV_HEAD, RWKV_HEAD), F32)],
        compiler_params=_params(("arbitrary", "arbitrary")),
        name="wkv_scan",
    )(*args)


def _post_kernel(yf_ref, yb_ref, bonus_ref, g_ref, att_ref, x_ref, gt_ref, sh_ref, sc_ref, wo_ref,
                 lxg_ref, lxb_ref, l1g_ref, l1b_ref, rwt_ref, bones_ref, x1_ref, h2_ref, aff_ref):
    bones = bones_ref[...]
    y = yf_ref[0] + yb_ref[0]
    inv = 1.0 / RWKV_HEAD
    mu = _mm_exact_rhs(y, bones) * inv
    yc = y - mu
    var = _mm_exact_rhs(yc * yc, bones) * inv
    yn = yc * lax.rsqrt(var + GN_EPS) * lxg_ref[...] + lxb_ref[...]
    rw_out = (yn + bonus_ref[0]) * g_ref[0]
    mix = _dg(att_ref[0], wo_ref[:D_ATT, :], _NN) + _dg(rw_out.astype(BF16), wo_ref[D_ATT:, :], _NN)
    x1 = _layer_norm(ALPHA * x_ref[0] + gt_ref[0] * mix) * l1g_ref[...] + l1b_ref[...]
    x1_ref[0] = x1
    h2 = _layer_norm(x1) * (1.0 + sc_ref[0]) + sh_ref[0]
    h2_ref[0] = h2.astype(BF16)
    logits = _mm(rwt_ref[...], h2, _NT, passes=6)
    e = jnp.exp(logits - jnp.max(logits, axis=0, keepdims=True))
    aff_ref[0] = e / jnp.sum(e, axis=0, keepdims=True)


def _post(yf, yb, bonus, g, att, x, gt1, sh2, sc2, w_out_bf16, lnx_g, lnx_b, ln1_g, ln1_b, router_wt,
          bones, n_ctx):
    B, L, D = x.shape
    tm = ROW_TILE
    nct = n_ctx // tm
    E = router_wt.shape[0]
    const = lambda b, i: (0, 0)
    cat = pl.BlockSpec((1, tm, D_RWKV), lambda b, i: (b, i + nct, 0))
    mod = pl.BlockSpec((1, 1, D), lambda b, i: (b, 0, 0))
    vec = lambda n: pl.BlockSpec((1, n), const)
    return pl.pallas_call(
        _post_kernel,
        grid=(B, L // tm),
        in_specs=[cat, cat, cat, cat,
                  pl.BlockSpec((1, tm, D_ATT), lambda b, i: (b, i, 0)),
                  pl.BlockSpec((1, tm, D), lambda b, i: (b, i, 0)),
                  mod, mod, mod,
                  pl.BlockSpec(w_out_bf16.shape, const),
                  vec(D_RWKV), vec(D_RWKV), vec(D), vec(D),
                  pl.BlockSpec((E, D), const),
                  pl.BlockSpec(bones.shape, const)],
        out_specs=[pl.BlockSpec((1, tm, D), lambda b, i: (b, i, 0)),
                   pl.BlockSpec((1, tm, D), lambda b, i: (b, i, 0)),
                   pl.BlockSpec((1, E, tm), lambda b, i: (b, 0, i))],
        out_shape=[jax.ShapeDtypeStruct((B, L, D), F32),
                   jax.ShapeDtypeStruct((B, L, D), BF16),
                   jax.ShapeDtypeStruct((B, E, L), F32)],
        compiler_params=_params(("arbitrary", "arbitrary")),
        name="post_mix",
    )(yf, yb, bonus, g, att, x, gt1, sh2, sc2, w_out_bf16, lnx_g, lnx_b, ln1_g, ln1_b, router_wt, bones)


def _cumsum_lanes(x):
    n = x.shape[1]
    lane = lax.broadcasted_iota(jnp.int32, x.shape, 1)
    s = 1
    while s < n:
        x = x + jnp.where(lane >= s, pltpu.roll(x, s, 1), 0)
        s *= 2
    return x


def _route_kernel(aff_ref, pos_ref, gate_ref, *, cap):
    aff = aff_ref[0]
    E = aff.shape[0]
    count = lambda m: jnp.sum(jnp.where(m, 1, 0), axis=1, keepdims=True)
    thr_bits = jnp.zeros((E, 1), jnp.int32)
    for bit in range(29, -1, -1):
        cand = thr_bits | (1 << bit)
        cand_f = lax.bitcast_convert_type(cand, F32)
        thr_bits = jnp.where(count(aff >= cand_f) >= cap, cand, thr_bits)
    thr = lax.bitcast_convert_type(thr_bits, F32)
    above = aff > thr
    tie = aff == thr
    need = cap - count(above)
    tie_rank = _cumsum_lanes(jnp.where(tie, 1, 0))
    sel = above | (tie & (tie_rank <= need))
    slot = _cumsum_lanes(jnp.where(sel, 1, 0)) - 1
    pos = jnp.where(sel, slot, -1)
    gate = jnp.where(sel, aff, 0.0)
    for e in range(E):
        pos_ref[0, e] = pos[e:e + 1, :]
        gate_ref[0, e] = gate[e:e + 1, :]


def _route(aff, cap):
    B, E, L = aff.shape
    out = pl.BlockSpec((1, E, 1, L), lambda b: (b, 0, 0, 0))
    return pl.pallas_call(
        functools.partial(_route_kernel, cap=cap),
        grid=(B,),
        in_specs=[pl.BlockSpec((1, E, L), lambda b: (b, 0, 0))],
        out_specs=[out, out],
        out_shape=[jax.ShapeDtypeStruct((B, E, 1, L), jnp.int32),
                   jax.ShapeDtypeStruct((B, E, 1, L), F32)],
        compiler_params=_params(("arbitrary",)),
        name="route",
    )(aff)


def _expert_kernel(h_ref, pos_ref, gate_ref, wg_ref, wu_ref, wd_ref, y_ref, *, cap):
    pos = pos_ref[0, 0]
    L = pos.shape[1]
    slot = lax.broadcasted_iota(jnp.int32, (cap, L), 0)
    hit = pos == slot
    onehot = jnp.where(hit, 1.0, 0.0).astype(BF16)
    gate = jnp.sum(jnp.where(hit, gate_ref[0, 0], 0.0), axis=1, keepdims=True)
    xin = _dg(onehot, h_ref[0], _NN).astype(BF16)
    hg = _dg(xin, wg_ref[0], _NN)
    hu = _dg(xin, wu_ref[0], _NN)
    hid = (hg * _sigmoid(hg) * hu).astype(BF16)
    y_ref[0, 0] = (_dg(hid, wd_ref[0], _NN) * gate).astype(BF16)


def _experts(h2, pos, gate, wg, wu, wd, cap):
    B, L, D = h2.shape
    E, _, F = wg.shape
    rowv = pl.BlockSpec((1, 1, 1, L), lambda b, e: (b, e, 0, 0))
    return pl.pallas_call(
        functools.partial(_expert_kernel, cap=cap),
        grid=(B, E),
        in_specs=[pl.BlockSpec((1, L, D), lambda b, e: (b, 0, 0)),
                  rowv, rowv,
                  pl.BlockSpec((1, D, F), lambda b, e: (e, 0, 0)),
                  pl.BlockSpec((1, D, F), lambda b, e: (e, 0, 0)),
                  pl.BlockSpec((1, F, D), lambda b, e: (e, 0, 0))],
        out_specs=pl.BlockSpec((1, 1, cap, D), lambda b, e: (b, e, 0, 0)),
        out_shape=jax.ShapeDtypeStruct((B, E, cap, D), BF16),
        compiler_params=_params(("arbitrary", "arbitrary")),
        name="experts",
    )(h2, pos, gate, wg, wu, wd)


def _combine_kernel(pos_ref, y_ref, x1_ref, gt_ref, lg_ref, lb_ref, o_ref, acc_ref, *, cap):
    e = pl.program_id(2)

    @pl.when(e == 0)
    def _():
        acc_ref[...] = jnp.zeros_like(acc_ref)

    pos = pos_ref[0, 0]
    slot = lax.broadcasted_iota(jnp.int32, (cap, pos.shape[1]), 0)
    onehot = jnp.where(pos == slot, 1.0, 0.0).astype(BF16)
    acc_ref[...] += _dg(onehot, y_ref[0, 0], _TN)

    @pl.when(e == pl.num_programs(2) - 1)
    def _():
        z = ALPHA * x1_ref[0] + gt_ref[0] * acc_ref[...]
        o_ref[0] = _layer_norm(z) * lg_ref[...] + lb_ref[...]


def _combine(pos, y, x1, gt2, ln2_g, ln2_b, cap):
    B, L, D = x1.shape
    E = y.shape[1]
    tl = min(COMBINE_TILE, L)
    const = lambda b, i, e: (0, 0)
    return pl.pallas_call(
        functools.partial(_combine_kernel, cap=cap),
        grid=(B, L // tl, E),
        in_specs=[pl.BlockSpec((1, 1, 1, tl), lambda b, i, e: (b, e, 0, i)),
                  pl.BlockSpec((1, 1, cap, D), lambda b, i, e: (b, e, 0, 0)),
                  pl.BlockSpec((1, tl, D), lambda b, i, e: (b, i, 0)),
                  pl.BlockSpec((1, 1, D), lambda b, i, e: (b, 0, 0)),
                  pl.BlockSpec((1, D), const),
                  pl.BlockSpec((1, D), const)],
        out_specs=pl.BlockSpec((1, tl, D), lambda b, i, e: (b, i, 0)),
        out_shape=jax.ShapeDtypeStruct((B, L, D), F32),
        scratch_shapes=[pltpu.VMEM((tl, D), F32)],
        compiler_params=_params(("arbitrary", "arbitrary", "arbitrary")),
        name="combine",
    )(pos, y, x1, gt2, ln2_g, ln2_b)


def _rope_tables(n_ctx, n_lat):
    lane = jnp.arange(128)
    j = lane % HEAD_DIM
    axis = j // (2 * ROPE_PAIRS)
    upper = (j % (2 * ROPE_PAIRS)) >= ROPE_PAIRS
    inv = ROPE_THETA ** (-(j % ROPE_PAIRS).astype(F32) / ROPE_PAIRS)
    t = jnp.arange(n_lat)
    coord = jnp.where(axis[None, :] == 0, (t // GRID_W)[:, None], (t % GRID_W)[:, None]).astype(F32)
    ang = coord * inv[None, :]
    cos, sin = jnp.cos(ang), jnp.sin(ang)
    s_lo = jnp.where(upper[None, :], 0.0, -sin)
    s_hi = jnp.where(upper[None, :], sin, 0.0)
    pad = lambda a, fill: jnp.concatenate([jnp.full((n_ctx, 128), fill, F32), a], axis=0)
    return pad(cos, 1.0), pad(s_lo, 0.0), pad(s_hi, 0.0)


def _head_ones(n):
    h = jnp.arange(n) // HEAD_DIM
    return (h[:, None] == h[None, :]).astype(BF16)


def kernel(x, c, ctx, c_ctx, w_ada, b_ada, w_in, q_gain, k_gain, tshift_mu, decay_w0, decay_up, iclr_a0,
           iclr_up, gate_up, k_k, k_a, r_k, lnx_g, lnx_b, w_out, ln1_g, ln1_b, router_w, exp_w_gate,
           exp_w_up, exp_w_down, ln2_g, ln2_b):
    B, L, D = x.shape
    n_ctx = ctx.shape[1]
    li = 0
    row = lambda a: a.reshape(1, -1)

    n_rows = -(-(B + 1) // 8) * 8
    c_rows = jnp.concatenate([c, c_ctx[None, :], jnp.zeros((n_rows - B - 1, D), F32)], axis=0)
    mod = _ada_mod(c_rows, w_ada[li], b_ada[li]).reshape(n_rows, N_MOD, 1, D)
    sh1, sc1, gt1, sh2, sc2, gt2 = (mod[:, m] for m in range(N_MOD))

    bones = _head_ones(D_RWKV)
    cos, s_lo, s_hi = _rope_tables(n_ctx, L)
    q, k, v, rw = _inproj(x, ctx, sh1, sc1, w_in[li].astype(BF16), bones,
                          row(jnp.tile(q_gain[li], ATT_HEADS)), row(jnp.tile(k_gain[li], ATT_KV_HEADS)),
                          cos, s_lo, s_hi)
    Tc = n_ctx + L
    vt = v.reshape(B, ATT_KV_HEADS, Tc // ATT_K_TILE, ATT_K_TILE, HEAD_DIM).swapaxes(-1, -2)
    vt = jnp.concatenate([vt, jnp.ones(vt.shape[:3] + (16, ATT_K_TILE), BF16)], axis=3)
    att = _attention(q, k, vt)

    zpad = jnp.zeros((2, D_LORA // 2, D_RWKV), F32)
    dup = jnp.concatenate([decay_up[li], zpad], axis=1)
    aup = jnp.concatenate([zpad, iclr_up[li]], axis=1)
    r, vr, kk, g, bonus, lw, kd, bv = _rwkv_prep(
        rw, n_ctx, row(tshift_mu[li]), decay_w0[li], dup, iclr_a0[li], aup, gate_up[li],
        row(k_k[li]), row(k_a[li]), row(r_k[li]), bones)
    yf, yb = _wkv_scan(r, vr, kk, lw, kd, bv, n_ctx)

    x1, h2, aff = _post(yf, yb, bonus, g, att, x, gt1, sh2, sc2, w_out[li].astype(BF16),
                        row(lnx_g[li]), row(lnx_b[li]), row(ln1_g[li]), row(ln1_b[li]),
                        router_w[li].T, bones, n_ctx)
    cap = CAPACITY_FACTOR * L // N_EXPERTS
    pos, gate = _route(aff, cap)
    y = _experts(h2, pos, gate, exp_w_gate[li].astype(BF16), exp_w_up[li].astype(BF16),
                 exp_w_down[li].astype(BF16), cap)
    return _combine(pos, y, x1, gt2, row(ln2_g[li]), row(ln2_b[li]), cap)
```

```python
import functools

import jax
import jax.numpy as jnp
from jax import lax
from jax.experimental import pallas as pl
from jax.experimental.pallas import tpu as pltpu

F32 = jnp.float32
BF16 = jnp.bfloat16

HEAD_DIM = 64
ATT_HEADS = 8
ATT_KV_HEADS = 2
GQA_GROUP = ATT_HEADS // ATT_KV_HEADS
D_ATT = ATT_HEADS * HEAD_DIM
D_ATT_KV = ATT_KV_HEADS * HEAD_DIM
ATT_SCALE = HEAD_DIM ** -0.5
LOG2_E = 1.4426950408889634
ROPE_THETA = 10000.0
ROPE_PAIRS = HEAD_DIM // 4
GRID_W = 64
RWKV_HEAD = 64
D_RWKV = 512
D_LORA = 128
N_EXPERTS = 16
CAPACITY_FACTOR = 2
N_MOD = 6
LN_EPS = 1e-5
RMS_EPS = 1e-6
GN_EPS = 64e-5
L2_EPS = 1e-12
DEPTH = 1
ALPHA = (2.0 * DEPTH) ** 0.25

WKV_CHUNK = 64
WKV_BLOCK = 256
ROW_TILE = 256
ATT_Q_TILE = 128
ATT_K_TILE = 256
COMBINE_TILE = 1024
VMEM_LIMIT = 48 * 1024 * 1024


def _params(sem):
    return pltpu.CompilerParams(dimension_semantics=sem, vmem_limit_bytes=VMEM_LIMIT)


def _split3(x):
    hi = x.astype(BF16)
    r1 = x - hi.astype(F32)
    mid = r1.astype(BF16)
    lo = (r1 - mid.astype(F32)).astype(BF16)
    return hi, mid, lo


def _split2(x):
    hi = x.astype(BF16)
    lo = (x - hi.astype(F32)).astype(BF16)
    return hi, lo


def _dg(a, b, dims):
    return lax.dot_general(a, b, (dims, ((), ())), preferred_element_type=F32)


_NN = ((1,), (0,))
_NT = ((1,), (1,))
_TN = ((0,), (0,))


def _mm(a, b, dims=_NN, passes=3):
    if passes == 1:
        return _dg(a.astype(BF16), b.astype(BF16), dims)
    if passes == 3:
        ah, al = _split2(a)
        bh, bl = _split2(b)
        return _dg(ah, bh, dims) + (_dg(ah, bl, dims) + _dg(al, bh, dims))
    ah, am, al = _split3(a)
    bh, bm, bl = _split3(b)
    return (_dg(ah, bh, dims) + (_dg(ah, bm, dims) + _dg(am, bh, dims))
            + (_dg(am, bm, dims) + _dg(ah, bl, dims) + _dg(al, bh, dims)))


def _pieces(x, passes):
    return (x.astype(BF16),) if passes == 1 else _split2(x)


def _mmp(a, b, dims):
    out = _dg(a[0], b[0], dims)
    if len(a) > 1 and len(b) > 1:
        return out + (_dg(a[0], b[1], dims) + _dg(a[1], b[0], dims))
    if len(a) > 1:
        return out + _dg(a[1], b[0], dims)
    if len(b) > 1:
        return out + _dg(a[0], b[1], dims)
    return out


def _mm_exact_lhs(a_bf16, b, dims=_NN):
    bh, bm, bl = _split3(b)
    return _dg(a_bf16, bh, dims) + (_dg(a_bf16, bm, dims) + _dg(a_bf16, bl, dims))


def _mm_exact_rhs(a, b_bf16, dims=_NN):
    ah, am, al = _split3(a)
    return _dg(ah, b_bf16, dims) + (_dg(am, b_bf16, dims) + _dg(al, b_bf16, dims))


def _sigmoid(x):
    return 1.0 / (1.0 + jnp.exp(-x))


def _layer_norm(x):
    mu = jnp.mean(x, axis=-1, keepdims=True)
    xc = x - mu
    var = jnp.mean(xc * xc, axis=-1, keepdims=True)
    return xc * lax.rsqrt(var + LN_EPS)


def _ada_kernel(c_ref, w_ref, b_ref, o_ref):
    c = c_ref[...]
    o_ref[...] = _mm(c * _sigmoid(c), w_ref[...], passes=6) + b_ref[...]


def _ada_mod(c_rows, w_ada, b_ada):
    R, D = c_rows.shape
    N = w_ada.shape[1]
    tn = 1024
    return pl.pallas_call(
        _ada_kernel,
        grid=(N // tn,),
        in_specs=[pl.BlockSpec((R, D), lambda j: (0, 0)),
                  pl.BlockSpec((D, tn), lambda j: (0, j)),
                  pl.BlockSpec((1, tn), lambda j: (0, j))],
        out_specs=pl.BlockSpec((R, tn), lambda j: (0, j)),
        out_shape=jax.ShapeDtypeStruct((R, N), F32),
        compiler_params=_params(("arbitrary",)),
        name="ada_mod",
    )(c_rows, w_ada, b_ada.reshape(1, N))


def _rope(t, cos, sin_lo, sin_hi, reps):
    w = t.shape[-1]
    tile = lambda a: jnp.concatenate([a] * reps, axis=1) if reps > 1 else a
    return (t * tile(cos) + pltpu.roll(t, w - ROPE_PAIRS, 1) * tile(sin_lo)
            + pltpu.roll(t, ROPE_PAIRS, 1) * tile(sin_hi))


def _inproj_kernel(x_ref, ctx_ref, sh_ref, sc_ref, w_ref, bones_ref, qg_ref, kg_ref,
                   cos_ref, slo_ref, shi_ref, q_ref, k_ref, v_ref, rw_ref, *, n_ctx_tiles):
    i = pl.program_id(1)
    x = jnp.where(i < n_ctx_tiles, ctx_ref[0], x_ref[0])
    h = _layer_norm(x) * (1.0 + sc_ref[0]) + sh_ref[0]
    p = _dg(h.astype(BF16), w_ref[...], _NN)
    bones = bones_ref[...]
    cos, slo, shi = cos_ref[...], slo_ref[...], shi_ref[...]

    q = p[:, :D_ATT]
    ss = _mm_exact_rhs(q * q, bones)
    q = q * lax.rsqrt(ss * (1.0 / HEAD_DIM) + RMS_EPS) * qg_ref[...]
    q = _rope(q, cos, slo, shi, D_ATT // 128) * (ATT_SCALE * LOG2_E)
    q_ref[0] = q.astype(BF16)

    k = p[:, D_ATT:D_ATT + D_ATT_KV]
    ss = _mm_exact_rhs(k * k, bones[:D_ATT_KV, :D_ATT_KV])
    k = k * lax.rsqrt(ss * (1.0 / HEAD_DIM) + RMS_EPS) * kg_ref[...]
    k = _rope(k, cos, slo, shi, D_ATT_KV // 128).astype(BF16)
    v = p[:, D_ATT + D_ATT_KV:D_ATT + 2 * D_ATT_KV].astype(BF16)
    for g in range(ATT_KV_HEADS):
        k_ref[0, g] = k[:, g * HEAD_DIM:(g + 1) * HEAD_DIM]
        v_ref[0, g] = v[:, g * HEAD_DIM:(g + 1) * HEAD_DIM]
    rw_ref[0] = p[:, D_ATT + 2 * D_ATT_KV:]


def _inproj(x, ctx, shift, scale, w_in_bf16, bones, qg, kg, cos, slo, shi):
    B, L, D = x.shape
    n_ctx = ctx.shape[1]
    tm = ROW_TILE
    nct = n_ctx // tm
    Tc = n_ctx + L
    d_in = w_in_bf16.shape[1]
    d_rw = d_in - D_ATT - 2 * D_ATT_KV
    mod_row = lambda b, i: (jnp.where(i < nct, B, b), 0, 0)
    const = lambda b, i: (0, 0)
    return pl.pallas_call(
        functools.partial(_inproj_kernel, n_ctx_tiles=nct),
        grid=(B, Tc // tm),
        in_specs=[pl.BlockSpec((1, tm, D), lambda b, i: (b, jnp.maximum(i - nct, 0), 0)),
                  pl.BlockSpec((1, tm, D), lambda b, i: (b, jnp.minimum(i, nct - 1), 0)),
                  pl.BlockSpec((1, 1, D), mod_row),
                  pl.BlockSpec((1, 1, D), mod_row),
                  pl.BlockSpec((D, d_in), const),
                  pl.BlockSpec(bones.shape, const),
                  pl.BlockSpec((1, D_ATT), const),
                  pl.BlockSpec((1, D_ATT_KV), const),
                  pl.BlockSpec((tm, 128), lambda b, i: (i, 0)),
                  pl.BlockSpec((tm, 128), lambda b, i: (i, 0)),
                  pl.BlockSpec((tm, 128), lambda b, i: (i, 0))],
        out_specs=[pl.BlockSpec((1, tm, D_ATT), lambda b, i: (b, jnp.maximum(i - nct, 0), 0)),
                   pl.BlockSpec((1, ATT_KV_HEADS, tm, HEAD_DIM), lambda b, i: (b, 0, i, 0)),
                   pl.BlockSpec((1, ATT_KV_HEADS, tm, HEAD_DIM), lambda b, i: (b, 0, i, 0)),
                   pl.BlockSpec((1, tm, d_rw), lambda b, i: (b, i, 0))],
        out_shape=[jax.ShapeDtypeStruct((B, L, D_ATT), BF16),
                   jax.ShapeDtypeStruct((B, ATT_KV_HEADS, Tc, HEAD_DIM), BF16),
                   jax.ShapeDtypeStruct((B, ATT_KV_HEADS, Tc, HEAD_DIM), BF16),
                   jax.ShapeDtypeStruct((B, Tc, d_rw), F32)],
        compiler_params=_params(("arbitrary", "arbitrary")),
        name="inproj",
    )(x, ctx, shift, scale, w_in_bf16, bones, qg, kg, cos, slo, shi)


def _attn_kernel(q_ref, k_ref, vt_ref, o_ref, s0, s1, p0, p1, m0, m1):
    i = pl.program_id(2)
    nk, tk, cols = s0.shape

    @pl.when(i == 0)
    def _():
        for ref in (s0, s1, m0, m1):
            ref[...] = jnp.zeros_like(ref)
        p0[...] = jnp.ones_like(p0)
        p1[...] = jnp.ones_like(p1)

    def step(s_w, m_w, s_r, m_r, p_w, p_r):
        q = q_ref[0]
        tq = q.shape[0]
        qs = jnp.concatenate([q[:, h * HEAD_DIM:(h + 1) * HEAD_DIM] for h in range(GQA_GROUP)], axis=0)
        m_prev = m_r[0:1, :]

        def body(j, carry):
            m_run, acc = carry
            off = pl.multiple_of(j * tk, tk)
            s_new = _dg(k_ref[0, 0, pl.ds(off, tk), :], qs, _NT)
            s_w[j] = s_new
            m_run = jnp.maximum(m_run, jnp.max(s_new.reshape(tk // 8, 8, cols), axis=0))
            p_w[j] = jnp.exp2(s_r[j] - m_prev).astype(BF16)
            acc = acc + _dg(vt_ref[0, 0, j], p_r[j], _NN)
            return m_run, acc

        init = (jnp.full((8, cols), -jnp.inf, F32), jnp.zeros((vt_ref.shape[3], cols), F32))
        m_run, acc = lax.fori_loop(0, nk, body, init, unroll=8)
        m_w[...] = jnp.broadcast_to(jnp.max(m_run, axis=0, keepdims=True), m_w.shape)
        o = acc[:HEAD_DIM] / acc[HEAD_DIM:HEAD_DIM + 1]
        o_ref[0] = jnp.concatenate([o[:, h * tq:(h + 1) * tq].T for h in range(GQA_GROUP)],
                                   axis=1).astype(BF16)

    @pl.when(i % 2 == 0)
    def _():
        step(s0, m0, s1, m1, p1, p0)

    @pl.when(i % 2 == 1)
    def _():
        step(s1, m1, s0, m0, p0, p1)


def _attention(q, k, vt):
    B, L, _ = q.shape
    Tc = k.shape[2]
    tq = ATT_Q_TILE
    n = L // tq
    gw = GQA_GROUP * HEAD_DIM
    cols = GQA_GROUP * tq
    tk = ATT_K_TILE
    nk = Tc // tk
    return pl.pallas_call(
        _attn_kernel,
        grid=(B, ATT_KV_HEADS, n + 2),
        in_specs=[pl.BlockSpec((1, tq, gw), lambda b, g, i: (b, jnp.minimum(i, n - 1), g)),
                  pl.BlockSpec((1, 1, Tc, HEAD_DIM), lambda b, g, i: (b, g, 0, 0)),
                  pl.BlockSpec((1, 1, nk, vt.shape[3], tk), lambda b, g, i: (b, g, 0, 0, 0))],
        out_specs=pl.BlockSpec((1, tq, gw), lambda b, g, i: (b, jnp.clip(i - 2, 0, n - 1), g)),
        out_shape=jax.ShapeDtypeStruct((B, L, D_ATT), BF16),
        scratch_shapes=[pltpu.VMEM((nk, tk, cols), F32), pltpu.VMEM((nk, tk, cols), F32),
                        pltpu.VMEM((nk, tk, cols), BF16), pltpu.VMEM((nk, tk, cols), BF16),
                        pltpu.VMEM((8, cols), F32), pltpu.VMEM((8, cols), F32)],
        compiler_params=_params(("arbitrary", "arbitrary", "arbitrary")),
        name="attention",
    )(q, k, vt)


def _rwkv_prep_kernel(rw_ref, prev_ref, next_ref, mu_ref, w0_ref, dup_ref, a0_ref, aup_ref, gup_ref,
                      kk_ref, ka_ref, rk_ref, bones_ref,
                      sh_o, g_o, bonus_o, dir_o, *, n_ctx, n_tok):
    i = pl.program_id(1)
    p = rw_ref[0]
    tm = p.shape[0]
    t0 = i * tm
    has_prev = jnp.where((t0 == 0) | (t0 == n_ctx), 0.0, 1.0)
    has_next = jnp.where((t0 + tm == n_ctx) | (t0 + tm == n_tok), 0.0, 1.0)
    row = lax.broadcasted_iota(jnp.int32, (tm, 1), 0)
    up = jnp.where(row == 0, prev_ref[0, 7:8, :] * has_prev, pltpu.roll(p, 1, 0))
    dn = jnp.where(row == tm - 1, next_ref[0, 0:1, :] * has_next, pltpu.roll(p, tm - 1, 0))
    s = p + (0.5 * (up + dn) - p) * mu_ref[...]

    D = D_RWKV
    r, k, v = s[:, :D], s[:, D:2 * D], s[:, 2 * D:3 * D]
    lora = s[:, 3 * D:3 * D + D_LORA]
    gl = s[:, 3 * D + D_LORA:]
    bones = bones_ref[...]

    g_o[0] = _mm(_sigmoid(gl), gup_ref[...], passes=3)
    kk = k * kk_ref[...]
    nrm = jnp.sqrt(_mm_exact_rhs(kk * kk, bones))
    kk = kk / jnp.maximum(nrm, L2_EPS)
    wt = jnp.tanh(lora)
    sh_o[0, :, 0:D] = r
    sh_o[0, :, D:2 * D] = v
    sh_o[0, :, 2 * D:3 * D] = kk
    bonus_o[0] = _mm_exact_rhs(r * k * rk_ref[...], bones) * v
    for d in range(2):
        z = w0_ref[d:d + 1, :] + _mm(wt, dup_ref[d], passes=3)
        softplus_neg = jnp.maximum(-z, 0.0) + jnp.log(1.0 + jnp.exp(-jnp.abs(z)))
        w = -softplus_neg - 0.5
        dir_o[d, 0, :, 0:D] = -jnp.exp(w)
        a = _sigmoid(a0_ref[d:d + 1, :] + _mm(lora, aup_ref[d], passes=3))
        dir_o[d, 0, :, D:2 * D] = k * (1.0 + (a - 1.0) * ka_ref[...])
        dir_o[d, 0, :, 2 * D:3 * D] = kk * a


def _rwkv_prep(rw, n_ctx, mu, w0, dup, a0, aup, gup, k_k, k_a, r_k, bones):
    B, Tc, d_rw = rw.shape
    tm = ROW_TILE
    D = D_RWKV
    const2 = lambda b, i: (0, 0)
    const3 = lambda b, i: (0, 0, 0)
    tok = lambda n: pl.BlockSpec((1, tm, n * D), lambda b, i: (b, i, 0))
    tok2 = pl.BlockSpec((2, 1, tm, 3 * D), lambda b, i: (0, b, i, 0))
    s1 = lambda n: jax.ShapeDtypeStruct((B, Tc, n * D), F32)
    s2 = jax.ShapeDtypeStruct((2, B, Tc, 3 * D), F32)
    hb = tm // 8
    return pl.pallas_call(
        functools.partial(_rwkv_prep_kernel, n_ctx=n_ctx, n_tok=Tc),
        grid=(B, Tc // tm),
        in_specs=[pl.BlockSpec((1, tm, d_rw), lambda b, i: (b, i, 0)),
                  pl.BlockSpec((1, 8, d_rw), lambda b, i: (b, jnp.maximum(i * hb - 1, 0), 0)),
                  pl.BlockSpec((1, 8, d_rw), lambda b, i: (b, jnp.minimum((i + 1) * hb, Tc // 8 - 1), 0)),
                  pl.BlockSpec((1, d_rw), const2),
                  pl.BlockSpec((2, D), const2),
                  pl.BlockSpec((2, D_LORA, D), const3),
                  pl.BlockSpec((2, D), const2),
                  pl.BlockSpec((2, D_LORA, D), const3),
                  pl.BlockSpec((D_LORA, D), const2),
                  pl.BlockSpec((1, D), const2),
                  pl.BlockSpec((1, D), const2),
                  pl.BlockSpec((1, D), const2),
                  pl.BlockSpec(bones.shape, const2)],
        out_specs=[tok(3), tok(1), tok(1), tok2],
        out_shape=[s1(3), s1(1), s1(1), s2],
        compiler_params=_params(("arbitrary", "arbitrary")),
        name="rwkv_prep",
    )(rw, rw, rw, mu, w0, dup, a0, aup, gup, k_k, k_a, r_k, bones)


def _wkv_kernel(sh0, dir0, sh1, dir1, y0, y1, s_ref, *, n_heads):
    @pl.when(pl.program_id(1) == 0)
    def _():
        s_ref[...] = jnp.zeros_like(s_ref)

    C = WKV_CHUNK
    n_sub = sh0.shape[1] // C
    lax.fori_loop(0, n_sub, functools.partial(_wkv_chunk, refs=(sh0, dir0, sh1, dir1, y0, y1, s_ref),
                                              n_heads=n_heads, n_sub=n_sub), 0)


def _wkv_chunk(sub, carry, *, refs, n_heads, n_sub):
    sh0, dir0, sh1, dir1, y0, y1, s_ref = refs
    C = WKV_CHUNK
    N = RWKV_HEAD
    D = n_heads * N
    row = lax.broadcasted_iota(jnp.int32, (C, C), 0)
    col = lax.broadcasted_iota(jnp.int32, (C, C), 1)
    cat = lambda xs, ys: tuple(jnp.concatenate([x, y], axis=0) for x, y in zip(xs, ys))
    offs = (pl.multiple_of(sub * C, C), pl.multiple_of((n_sub - 1 - sub) * C, C))
    y_refs = (y0, y1)

    chains = []
    for d, (sh_ref, dir_ref) in enumerate(((sh0, dir0), (sh1, dir1))):
        rows = pl.ds(offs[d], C)
        strict, incl = (row < col, row <= col) if d else (row > col, row >= col)
        lw = dir_ref[0, 0, rows, 0:D]
        cl = _mm_exact_lhs(jnp.where(incl, 1.0, 0.0).astype(BF16), lw)
        ecl = jnp.exp(cl)
        eneg = jnp.exp(-cl)
        last = 0 if d else C - 1
        etot = ecl[last:last + 1, :]
        kt = dir_ref[0, 0, rows, D:2 * D] * eneg
        bt = dir_ref[0, 0, rows, 2 * D:3 * D] * eneg
        full = dict(at=_split2(-(sh_ref[0, rows, 2 * D:3 * D] * jnp.exp(cl - lw))),
                    rt=_split2(sh_ref[0, rows, 0:D] * ecl),
                    kt=_split2(kt), bt=_split2(bt), v=_split2(sh_ref[0, rows, D:2 * D]),
                    kh=_split2(kt * etot), bh=_split2(bt * etot))
        for h in range(n_heads):
            ch = {k: tuple(p[:, h * N:(h + 1) * N] for p in val) for k, val in full.items()}
            ch.update(d=d, h=h, strict=strict, incl=incl, etot=etot[:, h * N:(h + 1) * N], s0=s_ref[d, h])
            chains.append(ch)

    for ch in chains:
        ch["ar"] = cat(ch["at"], ch["rt"])
        ch["s0p"] = _split2(ch["s0"])
    for ch in chains:
        ch["gb"] = _mmp(ch["ar"], ch["bt"], _NT)
        ch["gk"] = _mmp(ch["ar"], ch["kt"], _NT)
        ch["zs"] = _mmp(ch["ar"], ch["s0p"], _NT)
    for ch in chains:
        gk = ch["gk"]
        lk = jnp.concatenate([jnp.where(ch["strict"], gk[:C], 0.0), jnp.where(ch["incl"], gk[C:], 0.0)], axis=0)
        ch["lv"] = _mmp(_split2(lk), ch["v"], _NN)
        ch["u"] = ch["zs"][:C] + ch["lv"][:C]
        ch["pw"] = _split2(jnp.where(ch["strict"], ch["gb"][:C], 0.0))
    n = 1
    while True:
        for ch in chains:
            ch["u"] = ch["u"] + _mmp(ch["pw"], _split2(ch["u"]), _NN)
        n *= 2
        if n >= C:
            break
        for ch in chains:
            ch["pw"] = _split2(_mmp(ch["pw"], ch["pw"], _NN))
    for ch in chains:
        ch["up"] = _split2(ch["u"])
        l_rb = _split2(jnp.where(ch["incl"], ch["gb"][C:], 0.0))
        y = ch["zs"][C:] + ch["lv"][C:] + _mmp(l_rb, ch["up"], _NN)
        y_refs[ch["d"]][0, pl.ds(offs[ch["d"]], C), ch["h"] * N:(ch["h"] + 1) * N] = y
    for ch in chains:
        upd = _mmp(cat(ch["up"], ch["v"]), cat(ch["bh"], ch["kh"]), _TN)
        s_ref[ch["d"], ch["h"]] = ch["s0"] * ch["etot"] + upd
    return carry


def _wkv_scan(shared, per_dir, n_ctx):
    B, T, D3 = shared.shape
    D = D3 // 3
    tb = WKV_BLOCK
    n_blocks = T // tb
    n_cb = n_ctx // tb
    tok = (lambda c: c,
           lambda c: jnp.where(c < n_cb, n_cb - 1 - c, n_blocks - 1 + n_cb - c))
    in_specs, args = [], []
    for d in range(2):
        in_specs += [pl.BlockSpec((1, tb, D3), lambda bi, c, d=d: (bi, tok[d](c), 0)),
                     pl.BlockSpec((1, 1, tb, D3), lambda bi, c, d=d: (d, bi, tok[d](c), 0))]
        args += [shared, per_dir]
    y = jax.ShapeDtypeStruct((B, T, D), F32)
    return pl.pallas_call(
        functools.partial(_wkv_kernel, n_heads=D // RWKV_HEAD),
        grid=(B, n_blocks),
        in_specs=in_specs,
        out_specs=[pl.BlockSpec((1, tb, D), lambda bi, c, d=d: (bi, tok[d](c), 0)) for d in range(2)],
        out_shape=[y, y],
        scratch_shapes=[pltpu.VMEM((2, D // RWKV_HEAD, RWKV_HEAD, RWKV_HEAD), F32)],
        compiler_params=_params(("arbitrary", "arbitrary")),
        name="wkv_scan",
    )(*args)


def _post_kernel(yf_ref, yb_ref, bonus_ref, g_ref, att_ref, x_ref, gt_ref, sh_ref, sc_ref, wo_ref,
                 lxg_ref, lxb_ref, l1g_ref, l1b_ref, rwt_ref, bones_ref, x1_ref, h2_ref, aff_ref):
    bones = bones_ref[...]
    y = yf_ref[0] + yb_ref[0]
    inv = 1.0 / RWKV_HEAD
    mu = _mm_exact_rhs(y, bones) * inv
    yc = y - mu
    var = _mm_exact_rhs(yc * yc, bones) * inv
    yn = yc * lax.rsqrt(var + GN_EPS) * lxg_ref[...] + lxb_ref[...]
    rw_out = (yn + bonus_ref[0]) * g_ref[0]
    mix = _dg(att_ref[0], wo_ref[:D_ATT, :], _NN) + _dg(rw_out.astype(BF16), wo_ref[D_ATT:, :], _NN)
    x1 = _layer_norm(ALPHA * x_ref[0] + gt_ref[0] * mix) * l1g_ref[...] + l1b_ref[...]
    x1_ref[0] = x1
    h2 = _layer_norm(x1) * (1.0 + sc_ref[0]) + sh_ref[0]
    h2_ref[0] = h2.astype(BF16)
    logits = _mm(rwt_ref[...], h2, _NT, passes=6)
    e = jnp.exp(logits - jnp.max(logits, axis=0, keepdims=True))
    aff_ref[0] = e / jnp.sum(e, axis=0, keepdims=True)


def _post(yf, yb, bonus, g, att, x, gt1, sh2, sc2, w_out_bf16, lnx_g, lnx_b, ln1_g, ln1_b, router_wt,
          bones, n_ctx):
    B, L, D = x.shape
    tm = ROW_TILE
    nct = n_ctx // tm
    E = router_wt.shape[0]
    const = lambda b, i: (0, 0)
    cat = pl.BlockSpec((1, tm, D_RWKV), lambda b, i: (b, i + nct, 0))
    mod = pl.BlockSpec((1, 1, D), lambda b, i: (b, 0, 0))
    vec = lambda n: pl.BlockSpec((1, n), const)
    return pl.pallas_call(
        _post_kernel,
        grid=(B, L // tm),
        in_specs=[cat, cat, cat, cat,
                  pl.BlockSpec((1, tm, D_ATT), lambda b, i: (b, i, 0)),
                  pl.BlockSpec((1, tm, D), lambda b, i: (b, i, 0)),
                  mod, mod, mod,
                  pl.BlockSpec(w_out_bf16.shape, const),
                  vec(D_RWKV), vec(D_RWKV), vec(D), vec(D),
                  pl.BlockSpec((E, D), const),
                  pl.BlockSpec(bones.shape, const)],
        out_specs=[pl.BlockSpec((1, tm, D), lambda b, i: (b, i, 0)),
                   pl.BlockSpec((1, tm, D), lambda b, i: (b, i, 0)),
                   pl.BlockSpec((1, E, tm), lambda b, i: (b, 0, i))],
        out_shape=[jax.ShapeDtypeStruct((B, L, D), F32),
                   jax.ShapeDtypeStruct((B, L, D), BF16),
                   jax.ShapeDtypeStruct((B, E, L), F32)],
        compiler_params=_params(("arbitrary", "arbitrary")),
        name="post_mix",
    )(yf, yb, bonus, g, att, x, gt1, sh2, sc2, w_out_bf16, lnx_g, lnx_b, ln1_g, ln1_b, router_wt, bones)


def _cumsum_lanes(x):
    n = x.shape[1]
    lane = lax.broadcasted_iota(jnp.int32, x.shape, 1)
    s = 1
    while s < n:
        x = x + jnp.where(lane >= s, pltpu.roll(x, s, 1), 0)
        s *= 2
    return x


def _route_kernel(aff_ref, pos_ref, gate_ref, *, cap):
    aff = aff_ref[0]
    E = aff.shape[0]
    count = lambda m: jnp.sum(jnp.where(m, 1, 0), axis=1, keepdims=True)
    thr_bits = jnp.zeros((E, 1), jnp.int32)
    for bit in range(29, -1, -1):
        cand = thr_bits | (1 << bit)
        cand_f = lax.bitcast_convert_type(cand, F32)
        thr_bits = jnp.where(count(aff >= cand_f) >= cap, cand, thr_bits)
    thr = lax.bitcast_convert_type(thr_bits, F32)
    above = aff > thr
    tie = aff == thr
    need = cap - count(above)
    tie_rank = _cumsum_lanes(jnp.where(tie, 1, 0))
    sel = above | (tie & (tie_rank <= need))
    slot = _cumsum_lanes(jnp.where(sel, 1, 0)) - 1
    pos = jnp.where(sel, slot, -1)
    gate = jnp.where(sel, aff, 0.0)
    for e in range(E):
        pos_ref[0, e] = pos[e:e + 1, :]
        gate_ref[0, e] = gate[e:e + 1, :]


def _route(aff, cap):
    B, E, L = aff.shape
    out = pl.BlockSpec((1, E, 1, L), lambda b: (b, 0, 0, 0))
    return pl.pallas_call(
        functools.partial(_route_kernel, cap=cap),
        grid=(B,),
        in_specs=[pl.BlockSpec((1, E, L), lambda b: (b, 0, 0))],
        out_specs=[out, out],
        out_shape=[jax.ShapeDtypeStruct((B, E, 1, L), jnp.int32),
                   jax.ShapeDtypeStruct((B, E, 1, L), F32)],
        compiler_params=_params(("arbitrary",)),
        name="route",
    )(aff)


def _expert_kernel(h_ref, pos_ref, wg_ref, wu_ref, wd_ref, y_ref, *, cap):
    pos = pos_ref[0, 0]
    slot = lax.broadcasted_iota(jnp.int32, (cap, pos.shape[1]), 0)
    onehot = jnp.where(pos == slot, 1.0, 0.0).astype(BF16)
    xin = _dg(onehot, h_ref[0], _NN).astype(BF16)
    hg = _dg(xin, wg_ref[0], _NN)
    hu = _dg(xin, wu_ref[0], _NN)
    hid = (hg * _sigmoid(hg) * hu).astype(BF16)
    y_ref[0, 0] = _dg(hid, wd_ref[0], _NN).astype(BF16)


def _experts(h2, pos, wg, wu, wd, cap):
    B, L, D = h2.shape
    E, _, F = wg.shape
    rowv = pl.BlockSpec((1, 1, 1, L), lambda b, e: (b, e, 0, 0))
    return pl.pallas_call(
        functools.partial(_expert_kernel, cap=cap),
        grid=(B, E),
        in_specs=[pl.BlockSpec((1, L, D), lambda b, e: (b, 0, 0)),
                  rowv,
                  pl.BlockSpec((1, D, F), lambda b, e: (e, 0, 0)),
                  pl.BlockSpec((1, D, F), lambda b, e: (e, 0, 0)),
                  pl.BlockSpec((1, F, D), lambda b, e: (e, 0, 0))],
        out_specs=pl.BlockSpec((1, 1, cap, D), lambda b, e: (b, e, 0, 0)),
        out_shape=jax.ShapeDtypeStruct((B, E, cap, D), BF16),
        compiler_params=_params(("arbitrary", "arbitrary")),
        name="experts",
    )(h2, pos, wg, wu, wd)


def _combine_kernel(pos_ref, gate_ref, y_ref, x1_ref, gt_ref, lg_ref, lb_ref, o_ref, acc_ref, *, cap):
    e = pl.program_id(2)

    @pl.when(e == 0)
    def _():
        acc_ref[...] = jnp.zeros_like(acc_ref)

    pos = pos_ref[0, 0]
    slot = lax.broadcasted_iota(jnp.int32, (cap, pos.shape[1]), 0)
    weights = jnp.where(pos == slot, gate_ref[0, 0], 0.0).astype(BF16)
    acc_ref[...] += _dg(weights, y_ref[0, 0], _TN)

    @pl.when(e == pl.num_programs(2) - 1)
    def _():
        z = ALPHA * x1_ref[0] + gt_ref[0] * acc_ref[...]
        o_ref[0] = _layer_norm(z) * lg_ref[...] + lb_ref[...]


def _combine(pos, gate, y, x1, gt2, ln2_g, ln2_b, cap):
    B, L, D = x1.shape
    E = y.shape[1]
    tl = min(COMBINE_TILE, L)
    const = lambda b, i, e: (0, 0)
    return pl.pallas_call(
        functools.partial(_combine_kernel, cap=cap),
        grid=(B, L // tl, E),
        in_specs=[pl.BlockSpec((1, 1, 1, tl), lambda b, i, e: (b, e, 0, i)),
                  pl.BlockSpec((1, 1, 1, tl), lambda b, i, e: (b, e, 0, i)),
                  pl.BlockSpec((1, 1, cap, D), lambda b, i, e: (b, e, 0, 0)),
                  pl.BlockSpec((1, tl, D), lambda b, i, e: (b, i, 0)),
                  pl.BlockSpec((1, 1, D), lambda b, i, e: (b, 0, 0)),
                  pl.BlockSpec((1, D), const),
                  pl.BlockSpec((1, D), const)],
        out_specs=pl.BlockSpec((1, tl, D), lambda b, i, e: (b, i, 0)),
        out_shape=jax.ShapeDtypeStruct((B, L, D), F32),
        scratch_shapes=[pltpu.VMEM((tl, D), F32)],
        compiler_params=_params(("arbitrary", "arbitrary", "arbitrary")),
        name="combine",
    )(pos, gate, y, x1, gt2, ln2_g, ln2_b)


def _rope_tables(n_ctx, n_lat):
    lane = jnp.arange(128)
    j = lane % HEAD_DIM
    axis = j // (2 * ROPE_PAIRS)
    upper = (j % (2 * ROPE_PAIRS)) >= ROPE_PAIRS
    inv = ROPE_THETA ** (-(j % ROPE_PAIRS).astype(F32) / ROPE_PAIRS)
    t = jnp.arange(n_lat)
    coord = jnp.where(axis[None, :] == 0, (t // GRID_W)[:, None], (t % GRID_W)[:, None]).astype(F32)
    ang = coord * inv[None, :]
    cos, sin = jnp.cos(ang), jnp.sin(ang)
    s_lo = jnp.where(upper[None, :], 0.0, -sin)
    s_hi = jnp.where(upper[None, :], sin, 0.0)
    pad = lambda a, fill: jnp.concatenate([jnp.full((n_ctx, 128), fill, F32), a], axis=0)
    return pad(cos, 1.0), pad(s_lo, 0.0), pad(s_hi, 0.0)


def _head_ones(n):
    h = jnp.arange(n) // HEAD_DIM
    return (h[:, None] == h[None, :]).astype(BF16)


def kernel(x, c, ctx, c_ctx, w_ada, b_ada, w_in, q_gain, k_gain, tshift_mu, decay_w0, decay_up, iclr_a0,
           iclr_up, gate_up, k_k, k_a, r_k, lnx_g, lnx_b, w_out, ln1_g, ln1_b, router_w, exp_w_gate,
           exp_w_up, exp_w_down, ln2_g, ln2_b):
    B, L, D = x.shape
    n_ctx = ctx.shape[1]
    li = 0
    row = lambda a: a.reshape(1, -1)

    n_rows = -(-(B + 1) // 8) * 8
    c_rows = jnp.concatenate([c, c_ctx[None, :], jnp.zeros((n_rows - B - 1, D), F32)], axis=0)
    mod = _ada_mod(c_rows, w_ada[li], b_ada[li]).reshape(n_rows, N_MOD, 1, D)
    sh1, sc1, gt1, sh2, sc2, gt2 = (mod[:, m] for m in range(N_MOD))

    bones = _head_ones(D_RWKV)
    cos, s_lo, s_hi = _rope_tables(n_ctx, L)
    q, k, v, rw = _inproj(x, ctx, sh1, sc1, w_in[li].astype(BF16), bones,
                          row(jnp.tile(q_gain[li], ATT_HEADS)), row(jnp.tile(k_gain[li], ATT_KV_HEADS)),
                          cos, s_lo, s_hi)
    Tc = n_ctx + L
    vt = v.reshape(B, ATT_KV_HEADS, Tc // ATT_K_TILE, ATT_K_TILE, HEAD_DIM).swapaxes(-1, -2)
    vt = jnp.concatenate([vt, jnp.ones(vt.shape[:3] + (16, ATT_K_TILE), BF16)], axis=3)
    att = _attention(q, k, vt)

    zpad = jnp.zeros((2, D_LORA // 2, D_RWKV), F32)
    dup = jnp.concatenate([decay_up[li], zpad], axis=1)
    aup = jnp.concatenate([zpad, iclr_up[li]], axis=1)
    scan_shared, g, bonus, scan_dir = _rwkv_prep(
        rw, n_ctx, row(tshift_mu[li]), decay_w0[li], dup, iclr_a0[li], aup, gate_up[li],
        row(k_k[li]), row(k_a[li]), row(r_k[li]), bones)
    yf, yb = _wkv_scan(scan_shared, scan_dir, n_ctx)

    x1, h2, aff = _post(yf, yb, bonus, g, att, x, gt1, sh2, sc2, w_out[li].astype(BF16),
                        row(lnx_g[li]), row(lnx_b[li]), row(ln1_g[li]), row(ln1_b[li]),
                        router_w[li].T, bones, n_ctx)
    cap = CAPACITY_FACTOR * L // N_EXPERTS
    pos, gate = _route(aff, cap)
    y = _experts(h2, pos, exp_w_gate[li].astype(BF16), exp_w_up[li].astype(BF16),
                 exp_w_down[li].astype(BF16), cap)
    return _combine(pos, gate, y, x1, gt2, row(ln2_g[li]), row(ln2_b[li]), cap)
```

```python
import functools

import jax
import jax.numpy as jnp
from jax import lax
from jax.experimental import pallas as pl
from jax.experimental.pallas import tpu as pltpu

F32 = jnp.float32
BF16 = jnp.bfloat16

HEAD_DIM = 64
ATT_HEADS = 8
ATT_KV_HEADS = 2
GQA_GROUP = ATT_HEADS // ATT_KV_HEADS
D_ATT = ATT_HEADS * HEAD_DIM
D_ATT_KV = ATT_KV_HEADS * HEAD_DIM
ATT_SCALE = HEAD_DIM ** -0.5
LOG2_E = 1.4426950408889634
ROPE_THETA = 10000.0
ROPE_PAIRS = HEAD_DIM // 4
GRID_W = 64
RWKV_HEAD = 64
D_RWKV = 512
D_LORA = 128
N_EXPERTS = 16
CAPACITY_FACTOR = 2
N_MOD = 6
LN_EPS = 1e-5
RMS_EPS = 1e-6
GN_EPS = 64e-5
L2_EPS = 1e-12
DEPTH = 1
ALPHA = (2.0 * DEPTH) ** 0.25

WKV_CHUNK = 64
WKV_BLOCK = 256
ROW_TILE = 256
ATT_Q_TILE = 128
ATT_K_TILE = 256
COMBINE_TILE = 1024
COMBINE_WINDOW = 256
VMEM_LIMIT = 48 * 1024 * 1024


def _params(sem):
    return pltpu.CompilerParams(dimension_semantics=sem, vmem_limit_bytes=VMEM_LIMIT)


def _split3(x):
    hi = x.astype(BF16)
    r1 = x - hi.astype(F32)
    mid = r1.astype(BF16)
    lo = (r1 - mid.astype(F32)).astype(BF16)
    return hi, mid, lo


def _split2(x):
    hi = x.astype(BF16)
    lo = (x - hi.astype(F32)).astype(BF16)
    return hi, lo


def _dg(a, b, dims):
    return lax.dot_general(a, b, (dims, ((), ())), preferred_element_type=F32)


_NN = ((1,), (0,))
_NT = ((1,), (1,))
_TN = ((0,), (0,))


def _mm(a, b, dims=_NN, passes=3):
    if passes == 1:
        return _dg(a.astype(BF16), b.astype(BF16), dims)
    if passes == 3:
        ah, al = _split2(a)
        bh, bl = _split2(b)
        return _dg(ah, bh, dims) + (_dg(ah, bl, dims) + _dg(al, bh, dims))
    ah, am, al = _split3(a)
    bh, bm, bl = _split3(b)
    return (_dg(ah, bh, dims) + (_dg(ah, bm, dims) + _dg(am, bh, dims))
            + (_dg(am, bm, dims) + _dg(ah, bl, dims) + _dg(al, bh, dims)))


def _pieces(x, passes):
    return (x.astype(BF16),) if passes == 1 else _split2(x)


def _mmp(a, b, dims):
    out = _dg(a[0], b[0], dims)
    if len(a) > 1 and len(b) > 1:
        return out + (_dg(a[0], b[1], dims) + _dg(a[1], b[0], dims))
    if len(a) > 1:
        return out + _dg(a[1], b[0], dims)
    if len(b) > 1:
        return out + _dg(a[0], b[1], dims)
    return out


def _mm_exact_lhs(a_bf16, b, dims=_NN):
    bh, bm, bl = _split3(b)
    return _dg(a_bf16, bh, dims) + (_dg(a_bf16, bm, dims) + _dg(a_bf16, bl, dims))


def _mm_exact_rhs(a, b_bf16, dims=_NN):
    ah, am, al = _split3(a)
    return _dg(ah, b_bf16, dims) + (_dg(am, b_bf16, dims) + _dg(al, b_bf16, dims))


def _sigmoid(x):
    return 1.0 / (1.0 + jnp.exp(-x))


def _layer_norm(x):
    mu = jnp.mean(x, axis=-1, keepdims=True)
    xc = x - mu
    var = jnp.mean(xc * xc, axis=-1, keepdims=True)
    return xc * lax.rsqrt(var + LN_EPS)


def _ada_kernel(c_ref, w_ref, b_ref, o_ref):
    c = c_ref[...]
    o_ref[...] = _mm(c * _sigmoid(c), w_ref[...], passes=6) + b_ref[...]


def _ada_mod(c_rows, w_ada, b_ada):
    R, D = c_rows.shape
    N = w_ada.shape[1]
    tn = 1024
    return pl.pallas_call(
        _ada_kernel,
        grid=(N // tn,),
        in_specs=[pl.BlockSpec((R, D), lambda j: (0, 0)),
                  pl.BlockSpec((D, tn), lambda j: (0, j)),
                  pl.BlockSpec((1, tn), lambda j: (0, j))],
        out_specs=pl.BlockSpec((R, tn), lambda j: (0, j)),
        out_shape=jax.ShapeDtypeStruct((R, N), F32),
        compiler_params=_params(("arbitrary",)),
        name="ada_mod",
    )(c_rows, w_ada, b_ada.reshape(1, N))


def _rope(t, cos, sin_lo, sin_hi, reps):
    w = t.shape[-1]
    tile = lambda a: jnp.concatenate([a] * reps, axis=1) if reps > 1 else a
    return (t * tile(cos) + pltpu.roll(t, w - ROPE_PAIRS, 1) * tile(sin_lo)
            + pltpu.roll(t, ROPE_PAIRS, 1) * tile(sin_hi))


def _inproj_kernel(x_ref, ctx_ref, sh_ref, sc_ref, w_ref, bones_ref, qg_ref, kg_ref,
                   cos_ref, slo_ref, shi_ref, q_ref, k_ref, v_ref, rw_ref, *, n_ctx_tiles):
    i = pl.program_id(1)
    x = jnp.where(i < n_ctx_tiles, ctx_ref[0], x_ref[0])
    h = _layer_norm(x) * (1.0 + sc_ref[0]) + sh_ref[0]
    p = _dg(h.astype(BF16), w_ref[...], _NN)
    bones = bones_ref[...]
    cos, slo, shi = cos_ref[...], slo_ref[...], shi_ref[...]

    q = p[:, :D_ATT]
    ss = _mm_exact_rhs(q * q, bones)
    q = q * lax.rsqrt(ss * (1.0 / HEAD_DIM) + RMS_EPS) * qg_ref[...]
    q = _rope(q, cos, slo, shi, D_ATT // 128) * (ATT_SCALE * LOG2_E)
    q_ref[0] = q.astype(BF16)

    k = p[:, D_ATT:D_ATT + D_ATT_KV]
    ss = _mm_exact_rhs(k * k, bones[:D_ATT_KV, :D_ATT_KV])
    k = k * lax.rsqrt(ss * (1.0 / HEAD_DIM) + RMS_EPS) * kg_ref[...]
    k = _rope(k, cos, slo, shi, D_ATT_KV // 128).astype(BF16)
    v = p[:, D_ATT + D_ATT_KV:D_ATT + 2 * D_ATT_KV].astype(BF16)
    for g in range(ATT_KV_HEADS):
        k_ref[0, g] = k[:, g * HEAD_DIM:(g + 1) * HEAD_DIM]
        v_ref[0, g] = v[:, g * HEAD_DIM:(g + 1) * HEAD_DIM]
    rw_ref[0] = p[:, D_ATT + 2 * D_ATT_KV:]


def _inproj(x, ctx, shift, scale, w_in_bf16, bones, qg, kg, cos, slo, shi):
    B, L, D = x.shape
    n_ctx = ctx.shape[1]
    tm = ROW_TILE
    nct = n_ctx // tm
    Tc = n_ctx + L
    d_in = w_in_bf16.shape[1]
    d_rw = d_in - D_ATT - 2 * D_ATT_KV
    mod_row = lambda b, i: (jnp.where(i < nct, B, b), 0, 0)
    const = lambda b, i: (0, 0)
    return pl.pallas_call(
        functools.partial(_inproj_kernel, n_ctx_tiles=nct),
        grid=(B, Tc // tm),
        in_specs=[pl.BlockSpec((1, tm, D), lambda b, i: (b, jnp.maximum(i - nct, 0), 0)),
                  pl.BlockSpec((1, tm, D), lambda b, i: (b, jnp.minimum(i, nct - 1), 0)),
                  pl.BlockSpec((1, 1, D), mod_row),
                  pl.BlockSpec((1, 1, D), mod_row),
                  pl.BlockSpec((D, d_in), const),
                  pl.BlockSpec(bones.shape, const),
                  pl.BlockSpec((1, D_ATT), const),
                  pl.BlockSpec((1, D_ATT_KV), const),
                  pl.BlockSpec((tm, 128), lambda b, i: (i, 0)),
                  pl.BlockSpec((tm, 128), lambda b, i: (i, 0)),
                  pl.BlockSpec((tm, 128), lambda b, i: (i, 0))],
        out_specs=[pl.BlockSpec((1, tm, D_ATT), lambda b, i: (b, jnp.maximum(i - nct, 0), 0)),
                   pl.BlockSpec((1, ATT_KV_HEADS, tm, HEAD_DIM), lambda b, i: (b, 0, i, 0)),
                   pl.BlockSpec((1, ATT_KV_HEADS, tm, HEAD_DIM), lambda b, i: (b, 0, i, 0)),
                   pl.BlockSpec((1, tm, d_rw), lambda b, i: (b, i, 0))],
        out_shape=[jax.ShapeDtypeStruct((B, L, D_ATT), BF16),
                   jax.ShapeDtypeStruct((B, ATT_KV_HEADS, Tc, HEAD_DIM), BF16),
                   jax.ShapeDtypeStruct((B, ATT_KV_HEADS, Tc, HEAD_DIM), BF16),
                   jax.ShapeDtypeStruct((B, Tc, d_rw), F32)],
        compiler_params=_params(("arbitrary", "arbitrary")),
        name="inproj",
    )(x, ctx, shift, scale, w_in_bf16, bones, qg, kg, cos, slo, shi)


def _attn_kernel(q_ref, k_ref, vt_ref, o_ref, s0, s1, p0, p1, m0, m1):
    i = pl.program_id(2)
    nk, tk, cols = s0.shape

    @pl.when(i == 0)
    def _():
        for ref in (s0, s1, m0, m1):
            ref[...] = jnp.zeros_like(ref)
        p0[...] = jnp.ones_like(p0)
        p1[...] = jnp.ones_like(p1)

    def step(s_w, m_w, s_r, m_r, p_w, p_r):
        q = q_ref[0]
        tq = q.shape[0]
        qs = jnp.concatenate([q[:, h * HEAD_DIM:(h + 1) * HEAD_DIM] for h in range(GQA_GROUP)], axis=0)
        m_prev = m_r[0:1, :]

        def body(j, carry):
            m_run, acc = carry
            off = pl.multiple_of(j * tk, tk)
            s_new = _dg(k_ref[0, 0, pl.ds(off, tk), :], qs, _NT)
            s_w[j] = s_new
            m_run = jnp.maximum(m_run, jnp.max(s_new.reshape(tk // 8, 8, cols), axis=0))
            p_w[j] = jnp.exp2(s_r[j] - m_prev).astype(BF16)
            acc = acc + _dg(vt_ref[0, 0, j], p_r[j], _NN)
            return m_run, acc

        init = (jnp.full((8, cols), -jnp.inf, F32), jnp.zeros((vt_ref.shape[3], cols), F32))
        m_run, acc = lax.fori_loop(0, nk, body, init, unroll=8)
        m_w[...] = jnp.broadcast_to(jnp.max(m_run, axis=0, keepdims=True), m_w.shape)
        o = acc[:HEAD_DIM] / acc[HEAD_DIM:HEAD_DIM + 1]
        o_ref[0] = jnp.concatenate([o[:, h * tq:(h + 1) * tq].T for h in range(GQA_GROUP)],
                                   axis=1).astype(BF16)

    @pl.when(i % 2 == 0)
    def _():
        step(s0, m0, s1, m1, p1, p0)

    @pl.when(i % 2 == 1)
    def _():
        step(s1, m1, s0, m0, p0, p1)


def _attention(q, k, vt):
    B, L, _ = q.shape
    Tc = k.shape[2]
    tq = ATT_Q_TILE
    n = L // tq
    gw = GQA_GROUP * HEAD_DIM
    cols = GQA_GROUP * tq
    tk = ATT_K_TILE
    nk = Tc // tk
    return pl.pallas_call(
        _attn_kernel,
        grid=(B, ATT_KV_HEADS, n + 2),
        in_specs=[pl.BlockSpec((1, tq, gw), lambda b, g, i: (b, jnp.minimum(i, n - 1), g)),
                  pl.BlockSpec((1, 1, Tc, HEAD_DIM), lambda b, g, i: (b, g, 0, 0)),
                  pl.BlockSpec((1, 1, nk, vt.shape[3], tk), lambda b, g, i: (b, g, 0, 0, 0))],
        out_specs=pl.BlockSpec((1, tq, gw), lambda b, g, i: (b, jnp.clip(i - 2, 0, n - 1), g)),
        out_shape=jax.ShapeDtypeStruct((B, L, D_ATT), BF16),
        scratch_shapes=[pltpu.VMEM((nk, tk, cols), F32), pltpu.VMEM((nk, tk, cols), F32),
                        pltpu.VMEM((nk, tk, cols), BF16), pltpu.VMEM((nk, tk, cols), BF16),
                        pltpu.VMEM((8, cols), F32), pltpu.VMEM((8, cols), F32)],
        compiler_params=_params(("arbitrary", "arbitrary", "arbitrary")),
        name="attention",
    )(q, k, vt)


def _rwkv_prep_kernel(rw_ref, prev_ref, next_ref, mu_ref, w0_ref, dup_ref, a0_ref, aup_ref, gup_ref,
                      kk_ref, ka_ref, rk_ref, bones_ref,
                      sh_o, g_o, bonus_o, dir_o, *, n_ctx, n_tok):
    i = pl.program_id(1)
    p = rw_ref[0]
    tm = p.shape[0]
    t0 = i * tm
    has_prev = jnp.where((t0 == 0) | (t0 == n_ctx), 0.0, 1.0)
    has_next = jnp.where((t0 + tm == n_ctx) | (t0 + tm == n_tok), 0.0, 1.0)
    row = lax.broadcasted_iota(jnp.int32, (tm, 1), 0)
    up = jnp.where(row == 0, prev_ref[0, 7:8, :] * has_prev, pltpu.roll(p, 1, 0))
    dn = jnp.where(row == tm - 1, next_ref[0, 0:1, :] * has_next, pltpu.roll(p, tm - 1, 0))
    s = p + (0.5 * (up + dn) - p) * mu_ref[...]

    D = D_RWKV
    r, k, v = s[:, :D], s[:, D:2 * D], s[:, 2 * D:3 * D]
    lora = s[:, 3 * D:3 * D + D_LORA]
    gl = s[:, 3 * D + D_LORA:]
    bones = bones_ref[...]

    g_o[0] = _mm(_sigmoid(gl), gup_ref[...], passes=3)
    kk = k * kk_ref[...]
    nrm = jnp.sqrt(_mm_exact_rhs(kk * kk, bones))
    kk = kk / jnp.maximum(nrm, L2_EPS)
    wt = jnp.tanh(lora)
    sh_o[0, :, 0:D] = r
    sh_o[0, :, D:2 * D] = v
    sh_o[0, :, 2 * D:3 * D] = kk
    bonus_o[0] = _mm_exact_rhs(r * k * rk_ref[...], bones) * v
    for d in range(2):
        z = w0_ref[d:d + 1, :] + _mm(wt, dup_ref[d], passes=3)
        softplus_neg = jnp.maximum(-z, 0.0) + jnp.log(1.0 + jnp.exp(-jnp.abs(z)))
        w = -softplus_neg - 0.5
        dir_o[d, 0, :, 0:D] = -jnp.exp(w)
        a = _sigmoid(a0_ref[d:d + 1, :] + _mm(lora, aup_ref[d], passes=3))
        dir_o[d, 0, :, D:2 * D] = k * (1.0 + (a - 1.0) * ka_ref[...])
        dir_o[d, 0, :, 2 * D:3 * D] = kk * a


def _rwkv_prep(rw, n_ctx, mu, w0, dup, a0, aup, gup, k_k, k_a, r_k, bones):
    B, Tc, d_rw = rw.shape
    tm = ROW_TILE
    D = D_RWKV
    const2 = lambda b, i: (0, 0)
    const3 = lambda b, i: (0, 0, 0)
    tok = lambda n: pl.BlockSpec((1, tm, n * D), lambda b, i: (b, i, 0))
    tok2 = pl.BlockSpec((2, 1, tm, 3 * D), lambda b, i: (0, b, i, 0))
    s1 = lambda n: jax.ShapeDtypeStruct((B, Tc, n * D), F32)
    s2 = jax.ShapeDtypeStruct((2, B, Tc, 3 * D), F32)
    hb = tm // 8
    return pl.pallas_call(
        functools.partial(_rwkv_prep_kernel, n_ctx=n_ctx, n_tok=Tc),
        grid=(B, Tc // tm),
        in_specs=[pl.BlockSpec((1, tm, d_rw), lambda b, i: (b, i, 0)),
                  pl.BlockSpec((1, 8, d_rw), lambda b, i: (b, jnp.maximum(i * hb - 1, 0), 0)),
                  pl.BlockSpec((1, 8, d_rw), lambda b, i: (b, jnp.minimum((i + 1) * hb, Tc // 8 - 1), 0)),
                  pl.BlockSpec((1, d_rw), const2),
                  pl.BlockSpec((2, D), const2),
                  pl.BlockSpec((2, D_LORA, D), const3),
                  pl.BlockSpec((2, D), const2),
                  pl.BlockSpec((2, D_LORA, D), const3),
                  pl.BlockSpec((D_LORA, D), const2),
                  pl.BlockSpec((1, D), const2),
                  pl.BlockSpec((1, D), const2),
                  pl.BlockSpec((1, D), const2),
                  pl.BlockSpec(bones.shape, const2)],
        out_specs=[tok(3), tok(1), tok(1), tok2],
        out_shape=[s1(3), s1(1), s1(1), s2],
        compiler_params=_params(("arbitrary", "arbitrary")),
        name="rwkv_prep",
    )(rw, rw, rw, mu, w0, dup, a0, aup, gup, k_k, k_a, r_k, bones)


def _wkv_kernel(sh0, dir0, sh1, dir1, y0, y1, s_ref, *, n_heads):
    @pl.when(pl.program_id(1) == 0)
    def _():
        s_ref[...] = jnp.zeros_like(s_ref)

    C = WKV_CHUNK
    n_sub = sh0.shape[1] // C
    lax.fori_loop(0, n_sub, functools.partial(_wkv_chunk, refs=(sh0, dir0, sh1, dir1, y0, y1, s_ref),
                                              n_heads=n_heads, n_sub=n_sub), 0)


def _wkv_chunk(sub, carry, *, refs, n_heads, n_sub):
    sh0, dir0, sh1, dir1, y0, y1, s_ref = refs
    C = WKV_CHUNK
    N = RWKV_HEAD
    D = n_heads * N
    row = lax.broadcasted_iota(jnp.int32, (C, C), 0)
    col = lax.broadcasted_iota(jnp.int32, (C, C), 1)
    cat = lambda xs, ys: tuple(jnp.concatenate([x, y], axis=0) for x, y in zip(xs, ys))
    offs = (pl.multiple_of(sub * C, C), pl.multiple_of((n_sub - 1 - sub) * C, C))
    y_refs = (y0, y1)

    chains = []
    for d, (sh_ref, dir_ref) in enumerate(((sh0, dir0), (sh1, dir1))):
        rows = pl.ds(offs[d], C)
        strict, incl = (row < col, row <= col) if d else (row > col, row >= col)
        lw = dir_ref[0, 0, rows, 0:D]
        cl = _mm_exact_lhs(jnp.where(incl, 1.0, 0.0).astype(BF16), lw)
        ecl = jnp.exp(cl)
        eneg = jnp.exp(-cl)
        last = 0 if d else C - 1
        etot = ecl[last:last + 1, :]
        kt = dir_ref[0, 0, rows, D:2 * D] * eneg
        bt = dir_ref[0, 0, rows, 2 * D:3 * D] * eneg
        full = dict(at=_split2(-(sh_ref[0, rows, 2 * D:3 * D] * jnp.exp(cl - lw))),
                    rt=_split2(sh_ref[0, rows, 0:D] * ecl),
                    kt=_split2(kt), bt=_split2(bt), v=_split2(sh_ref[0, rows, D:2 * D]),
                    kh=_split2(kt * etot), bh=_split2(bt * etot))
        for h in range(n_heads):
            ch = {k: tuple(p[:, h * N:(h + 1) * N] for p in val) for k, val in full.items()}
            ch.update(d=d, h=h, strict=strict, incl=incl, etot=etot[:, h * N:(h + 1) * N], s0=s_ref[d, h])
            chains.append(ch)

    for ch in chains:
        ch["ar"] = cat(ch["at"], ch["rt"])
        ch["s0p"] = _split2(ch["s0"])
    for ch in chains:
        ch["gb"] = _mmp(ch["ar"], ch["bt"], _NT)
        ch["gk"] = _mmp(ch["ar"], ch["kt"], _NT)
        ch["zs"] = _mmp(ch["ar"], ch["s0p"], _NT)
    for ch in chains:
        gk = ch["gk"]
        lk = jnp.concatenate([jnp.where(ch["strict"], gk[:C], 0.0), jnp.where(ch["incl"], gk[C:], 0.0)], axis=0)
        ch["lv"] = _mmp(_split2(lk), ch["v"], _NN)
        ch["u"] = ch["zs"][:C] + ch["lv"][:C]
        ch["pw"] = _split2(jnp.where(ch["strict"], ch["gb"][:C], 0.0))
    n = 1
    while True:
        for ch in chains:
            ch["u"] = ch["u"] + _mmp(ch["pw"], _split2(ch["u"]), _NN)
        n *= 2
        if n >= C:
            break
        for ch in chains:
            ch["pw"] = _split2(_mmp(ch["pw"], ch["pw"], _NN))
    for ch in chains:
        ch["up"] = _split2(ch["u"])
        l_rb = _split2(jnp.where(ch["incl"], ch["gb"][C:], 0.0))
        y = ch["zs"][C:] + ch["lv"][C:] + _mmp(l_rb, ch["up"], _NN)
        y_refs[ch["d"]][0, pl.ds(offs[ch["d"]], C), ch["h"] * N:(ch["h"] + 1) * N] = y
    for ch in chains:
        upd = _mmp(cat(ch["up"], ch["v"]), cat(ch["bh"], ch["kh"]), _TN)
        s_ref[ch["d"], ch["h"]] = ch["s0"] * ch["etot"] + upd
    return carry


def _wkv_scan(shared, per_dir, n_ctx):
    B, T, D3 = shared.shape
    D = D3 // 3
    tb = WKV_BLOCK
    n_blocks = T // tb
    n_cb = n_ctx // tb
    tok = (lambda c: c,
           lambda c: jnp.where(c < n_cb, n_cb - 1 - c, n_blocks - 1 + n_cb - c))
    in_specs, args = [], []
    for d in range(2):
        in_specs += [pl.BlockSpec((1, tb, D3), lambda bi, c, d=d: (bi, tok[d](c), 0)),
                     pl.BlockSpec((1, 1, tb, D3), lambda bi, c, d=d: (d, bi, tok[d](c), 0))]
        args += [shared, per_dir]
    y = jax.ShapeDtypeStruct((B, T, D), F32)
    return pl.pallas_call(
        functools.partial(_wkv_kernel, n_heads=D // RWKV_HEAD),
        grid=(B, n_blocks),
        in_specs=in_specs,
        out_specs=[pl.BlockSpec((1, tb, D), lambda bi, c, d=d: (bi, tok[d](c), 0)) for d in range(2)],
        out_shape=[y, y],
        scratch_shapes=[pltpu.VMEM((2, D // RWKV_HEAD, RWKV_HEAD, RWKV_HEAD), F32)],
        compiler_params=_params(("arbitrary", "arbitrary")),
        name="wkv_scan",
    )(*args)


def _post_kernel(yf_ref, yb_ref, bonus_ref, g_ref, att_ref, x_ref, gt_ref, sh_ref, sc_ref, wo_ref,
                 lxg_ref, lxb_ref, l1g_ref, l1b_ref, rwt_ref, bones_ref, x1_ref, h2_ref, aff_ref):
    bones = bones_ref[...]
    y = yf_ref[0] + yb_ref[0]
    inv = 1.0 / RWKV_HEAD
    mu = _mm_exact_rhs(y, bones) * inv
    yc = y - mu
    var = _mm_exact_rhs(yc * yc, bones) * inv
    yn = yc * lax.rsqrt(var + GN_EPS) * lxg_ref[...] + lxb_ref[...]
    rw_out = (yn + bonus_ref[0]) * g_ref[0]
    mix = _dg(att_ref[0], wo_ref[:D_ATT, :], _NN) + _dg(rw_out.astype(BF16), wo_ref[D_ATT:, :], _NN)
    x1 = _layer_norm(ALPHA * x_ref[0] + gt_ref[0] * mix) * l1g_ref[...] + l1b_ref[...]
    x1_ref[0] = x1
    h2 = _layer_norm(x1) * (1.0 + sc_ref[0]) + sh_ref[0]
    h2_ref[0] = h2.astype(BF16)
    logits = _mm(rwt_ref[...], h2, _NT, passes=6)
    e = jnp.exp(logits - jnp.max(logits, axis=0, keepdims=True))
    aff_ref[0] = e / jnp.sum(e, axis=0, keepdims=True)


def _post(yf, yb, bonus, g, att, x, gt1, sh2, sc2, w_out_bf16, lnx_g, lnx_b, ln1_g, ln1_b, router_wt,
          bones, n_ctx):
    B, L, D = x.shape
    tm = ROW_TILE
    nct = n_ctx // tm
    E = router_wt.shape[0]
    const = lambda b, i: (0, 0)
    cat = pl.BlockSpec((1, tm, D_RWKV), lambda b, i: (b, i + nct, 0))
    mod = pl.BlockSpec((1, 1, D), lambda b, i: (b, 0, 0))
    vec = lambda n: pl.BlockSpec((1, n), const)
    return pl.pallas_call(
        _post_kernel,
        grid=(B, L // tm),
        in_specs=[cat, cat, cat, cat,
                  pl.BlockSpec((1, tm, D_ATT), lambda b, i: (b, i, 0)),
                  pl.BlockSpec((1, tm, D), lambda b, i: (b, i, 0)),
                  mod, mod, mod,
                  pl.BlockSpec(w_out_bf16.shape, const),
                  vec(D_RWKV), vec(D_RWKV), vec(D), vec(D),
                  pl.BlockSpec((E, D), const),
                  pl.BlockSpec(bones.shape, const)],
        out_specs=[pl.BlockSpec((1, tm, D), lambda b, i: (b, i, 0)),
                   pl.BlockSpec((1, tm, D), lambda b, i: (b, i, 0)),
                   pl.BlockSpec((1, E, tm), lambda b, i: (b, 0, i))],
        out_shape=[jax.ShapeDtypeStruct((B, L, D), F32),
                   jax.ShapeDtypeStruct((B, L, D), BF16),
                   jax.ShapeDtypeStruct((B, E, L), F32)],
        compiler_params=_params(("arbitrary", "arbitrary")),
        name="post_mix",
    )(yf, yb, bonus, g, att, x, gt1, sh2, sc2, w_out_bf16, lnx_g, lnx_b, ln1_g, ln1_b, router_wt, bones)


def _cumsum_lanes(x):
    n = x.shape[1]
    lane = lax.broadcasted_iota(jnp.int32, x.shape, 1)
    s = 1
    while s < n:
        x = x + jnp.where(lane >= s, pltpu.roll(x, s, 1), 0)
        s *= 2
    return x


def _route_kernel(aff_ref, pos_ref, gate_ref, cum_ref, *, cap, tile):
    aff = aff_ref[0]
    E = aff.shape[0]
    count = lambda m: jnp.sum(jnp.where(m, 1, 0), axis=1, keepdims=True)
    thr_bits = jnp.zeros((E, 1), jnp.int32)
    for bit in range(29, -1, -1):
        cand = thr_bits | (1 << bit)
        cand_f = lax.bitcast_convert_type(cand, F32)
        thr_bits = jnp.where(count(aff >= cand_f) >= cap, cand, thr_bits)
    thr = lax.bitcast_convert_type(thr_bits, F32)
    above = aff > thr
    tie = aff == thr
    need = cap - count(above)
    tie_rank = _cumsum_lanes(jnp.where(tie, 1, 0))
    sel = above | (tie & (tie_rank <= need))
    slot = _cumsum_lanes(jnp.where(sel, 1, 0)) - 1
    pos = jnp.where(sel, slot, -1)
    gate = jnp.where(sel, aff, 0.0)
    for e in range(E):
        pos_ref[0, e] = pos[e:e + 1, :]
        gate_ref[0, e] = gate[e:e + 1, :]
    lane = lax.broadcasted_iota(jnp.int32, (E, 128), 1)
    chosen = jnp.where(sel, 1, 0)
    run = jnp.zeros((E, 1), jnp.int32)
    cum = jnp.zeros((E, 128), jnp.int32)
    n_tiles = aff.shape[1] // tile
    for i in range(n_tiles):
        cum = jnp.where(lane == i, run, cum)
        run = run + jnp.sum(chosen[:, i * tile:(i + 1) * tile], axis=1, keepdims=True)
    cum_ref[0] = jnp.where(lane == n_tiles, run, cum)


def _route(aff, cap, tile):
    B, E, L = aff.shape
    out = pl.BlockSpec((1, E, 1, L), lambda b: (b, 0, 0, 0))
    return pl.pallas_call(
        functools.partial(_route_kernel, cap=cap, tile=tile),
        grid=(B,),
        in_specs=[pl.BlockSpec((1, E, L), lambda b: (b, 0, 0))],
        out_specs=[out, out, pl.BlockSpec((1, E, 128), lambda b: (b, 0, 0))],
        out_shape=[jax.ShapeDtypeStruct((B, E, 1, L), jnp.int32),
                   jax.ShapeDtypeStruct((B, E, 1, L), F32),
                   jax.ShapeDtypeStruct((B, E, 128), jnp.int32)],
        compiler_params=_params(("arbitrary",)),
        name="route",
    )(aff)


def _expert_kernel(h_ref, pos_ref, wg_ref, wu_ref, wd_ref, y_ref, *, cap):
    pos = pos_ref[0, 0]
    slot = lax.broadcasted_iota(jnp.int32, (cap, pos.shape[1]), 0)
    onehot = jnp.where(pos == slot, 1.0, 0.0).astype(BF16)
    xin = _dg(onehot, h_ref[0], _NN).astype(BF16)
    hg = _dg(xin, wg_ref[0], _NN)
    hu = _dg(xin, wu_ref[0], _NN)
    hid = (hg * _sigmoid(hg) * hu).astype(BF16)
    y_ref[0, 0] = _dg(hid, wd_ref[0], _NN).astype(BF16)


def _experts(h2, pos, wg, wu, wd, cap):
    B, L, D = h2.shape
    E, _, F = wg.shape
    rowv = pl.BlockSpec((1, 1, 1, L), lambda b, e: (b, e, 0, 0))
    return pl.pallas_call(
        functools.partial(_expert_kernel, cap=cap),
        grid=(B, E),
        in_specs=[pl.BlockSpec((1, L, D), lambda b, e: (b, 0, 0)),
                  rowv,
                  pl.BlockSpec((1, D, F), lambda b, e: (e, 0, 0)),
                  pl.BlockSpec((1, D, F), lambda b, e: (e, 0, 0)),
                  pl.BlockSpec((1, F, D), lambda b, e: (e, 0, 0))],
        out_specs=pl.BlockSpec((1, 1, cap, D), lambda b, e: (b, e, 0, 0)),
        out_shape=jax.ShapeDtypeStruct((B, E, cap, D), BF16),
        compiler_params=_params(("arbitrary", "arbitrary")),
        name="experts",
    )(h2, pos, wg, wu, wd)


def _combine_kernel(cum_ref, pos_ref, gate_ref, y_ref, x1_ref, gt_ref, lg_ref, lb_ref, o_ref, acc_ref, *, cap):
    b, i, e = pl.program_id(0), pl.program_id(1), pl.program_id(2)

    @pl.when(e == 0)
    def _():
        acc_ref[...] = jnp.zeros_like(acc_ref)

    base = (b * pl.num_programs(2) + e) * 128
    lo, hi = cum_ref[base + i], cum_ref[base + i + 1]
    window = min(COMBINE_WINDOW, cap)
    align = min(128, window)
    start = pl.multiple_of(jnp.minimum((lo // align) * align, cap - window), align)

    def accumulate(first, width):
        pos = pos_ref[0, 0]
        slot = first + lax.broadcasted_iota(jnp.int32, (width, pos.shape[1]), 0)
        weights = jnp.where(pos == slot, gate_ref[0, 0], 0.0).astype(BF16)
        acc_ref[...] += _dg(weights, y_ref[0, 0, pl.ds(first, width), :], _TN)

    @pl.when((hi > lo) & (hi - start <= window))
    def _():
        accumulate(start, window)

    @pl.when(hi - start > window)
    def _():
        accumulate(0, cap)

    @pl.when(e == pl.num_programs(2) - 1)
    def _():
        z = ALPHA * x1_ref[0] + gt_ref[0] * acc_ref[...]
        o_ref[0] = _layer_norm(z) * lg_ref[...] + lb_ref[...]


def _combine(cum, pos, gate, y, x1, gt2, ln2_g, ln2_b, cap):
    B, L, D = x1.shape
    E = y.shape[1]
    tl = min(COMBINE_TILE, L)
    const = lambda b, i, e, cum: (0, 0)
    grid_spec = pltpu.PrefetchScalarGridSpec(
        num_scalar_prefetch=1,
        grid=(B, L // tl, E),
        in_specs=[pl.BlockSpec((1, 1, 1, tl), lambda b, i, e, cum: (b, e, 0, i)),
                  pl.BlockSpec((1, 1, 1, tl), lambda b, i, e, cum: (b, e, 0, i)),
                  pl.BlockSpec((1, 1, cap, D), lambda b, i, e, cum: (b, e, 0, 0)),
                  pl.BlockSpec((1, tl, D), lambda b, i, e, cum: (b, i, 0)),
                  pl.BlockSpec((1, 1, D), lambda b, i, e, cum: (b, 0, 0)),
                  pl.BlockSpec((1, D), const),
                  pl.BlockSpec((1, D), const)],
        out_specs=pl.BlockSpec((1, tl, D), lambda b, i, e, cum: (b, i, 0)),
        scratch_shapes=[pltpu.VMEM((tl, D), F32)])
    return pl.pallas_call(
        functools.partial(_combine_kernel, cap=cap),
        grid_spec=grid_spec,
        out_shape=jax.ShapeDtypeStruct((B, L, D), F32),
        compiler_params=_params(("arbitrary", "arbitrary", "arbitrary")),
        name="combine",
    )(cum, pos, gate, y, x1, gt2, ln2_g, ln2_b)


def _rope_tables(n_ctx, n_lat):
    lane = jnp.arange(128)
    j = lane % HEAD_DIM
    axis = j // (2 * ROPE_PAIRS)
    upper = (j % (2 * ROPE_PAIRS)) >= ROPE_PAIRS
    inv = ROPE_THETA ** (-(j % ROPE_PAIRS).astype(F32) / ROPE_PAIRS)
    t = jnp.arange(n_lat)
    coord = jnp.where(axis[None, :] == 0, (t // GRID_W)[:, None], (t % GRID_W)[:, None]).astype(F32)
    ang = coord * inv[None, :]
    cos, sin = jnp.cos(ang), jnp.sin(ang)
    s_lo = jnp.where(upper[None, :], 0.0, -sin)
    s_hi = jnp.where(upper[None, :], sin, 0.0)
    pad = lambda a, fill: jnp.concatenate([jnp.full((n_ctx, 128), fill, F32), a], axis=0)
    return pad(cos, 1.0), pad(s_lo, 0.0), pad(s_hi, 0.0)


def _head_ones(n):
    h = jnp.arange(n) // HEAD_DIM
    return (h[:, None] == h[None, :]).astype(BF16)


def kernel(x, c, ctx, c_ctx, w_ada, b_ada, w_in, q_gain, k_gain, tshift_mu, decay_w0, decay_up, iclr_a0,
           iclr_up, gate_up, k_k, k_a, r_k, lnx_g, lnx_b, w_out, ln1_g, ln1_b, router_w, exp_w_gate,
           exp_w_up, exp_w_down, ln2_g, ln2_b):
    B, L, D = x.shape
    n_ctx = ctx.shape[1]
    li = 0
    row = lambda a: a.reshape(1, -1)

    n_rows = -(-(B + 1) // 8) * 8
    c_rows = jnp.concatenate([c, c_ctx[None, :], jnp.zeros((n_rows - B - 1, D), F32)], axis=0)
    mod = _ada_mod(c_rows, w_ada[li], b_ada[li]).reshape(n_rows, N_MOD, 1, D)
    sh1, sc1, gt1, sh2, sc2, gt2 = (mod[:, m] for m in range(N_MOD))

    bones = _head_ones(D_RWKV)
    cos, s_lo, s_hi = _rope_tables(n_ctx, L)
    q, k, v, rw = _inproj(x, ctx, sh1, sc1, w_in[li].astype(BF16), bones,
                          row(jnp.tile(q_gain[li], ATT_HEADS)), row(jnp.tile(k_gain[li], ATT_KV_HEADS)),
                          cos, s_lo, s_hi)
    Tc = n_ctx + L
    vt = v.reshape(B, ATT_KV_HEADS, Tc // ATT_K_TILE, ATT_K_TILE, HEAD_DIM).swapaxes(-1, -2)
    vt = jnp.concatenate([vt, jnp.ones(vt.shape[:3] + (16, ATT_K_TILE), BF16)], axis=3)
    att = _attention(q, k, vt)

    zpad = jnp.zeros((2, D_LORA // 2, D_RWKV), F32)
    dup = jnp.concatenate([decay_up[li], zpad], axis=1)
    aup = jnp.concatenate([zpad, iclr_up[li]], axis=1)
    scan_shared, g, bonus, scan_dir = _rwkv_prep(
        rw, n_ctx, row(tshift_mu[li]), decay_w0[li], dup, iclr_a0[li], aup, gate_up[li],
        row(k_k[li]), row(k_a[li]), row(r_k[li]), bones)
    yf, yb = _wkv_scan(scan_shared, scan_dir, n_ctx)

    x1, h2, aff = _post(yf, yb, bonus, g, att, x, gt1, sh2, sc2, w_out[li].astype(BF16),
                        row(lnx_g[li]), row(lnx_b[li]), row(ln1_g[li]), row(ln1_b[li]),
                        router_w[li].T, bones, n_ctx)
    cap = CAPACITY_FACTOR * L // N_EXPERTS
    pos, gate, cum = _route(aff, cap, min(COMBINE_TILE, L))
    y = _experts(h2, pos, exp_w_gate[li].astype(BF16), exp_w_up[li].astype(BF16),
                 exp_w_down[li].astype(BF16), cap)
    return _combine(cum.reshape(-1), pos, gate, y, x1, gt2, row(ln2_g[li]), row(ln2_b[li]), cap)
```

```python
import functools

import jax
import jax.numpy as jnp
from jax import lax
from jax.experimental import pallas as pl
from jax.experimental.pallas import tpu as pltpu

F32 = jnp.float32
BF16 = jnp.bfloat16

HEAD_DIM = 64
ATT_HEADS = 8
ATT_KV_HEADS = 2
GQA_GROUP = ATT_HEADS // ATT_KV_HEADS
D_ATT = ATT_HEADS * HEAD_DIM
D_ATT_KV = ATT_KV_HEADS * HEAD_DIM
ATT_SCALE = HEAD_DIM ** -0.5
LOG2_E = 1.4426950408889634
ROPE_THETA = 10000.0
ROPE_PAIRS = HEAD_DIM // 4
GRID_W = 64
RWKV_HEAD = 64
D_RWKV = 512
D_LORA = 128
N_EXPERTS = 16
CAPACITY_FACTOR = 2
N_MOD = 6
LN_EPS = 1e-5
RMS_EPS = 1e-6
GN_EPS = 64e-5
L2_EPS = 1e-12
DEPTH = 1
ALPHA = (2.0 * DEPTH) ** 0.25

WKV_CHUNK = 64
WKV_BLOCK = 256
ROW_TILE = 256
ATT_Q_TILE = 128
ATT_K_TILE = 256
COMBINE_TILE = 1024
COMBINE_WINDOW = 256
VMEM_LIMIT = 48 * 1024 * 1024


def _params(sem):
    return pltpu.CompilerParams(dimension_semantics=sem, vmem_limit_bytes=VMEM_LIMIT)


def _split3(x):
    hi = x.astype(BF16)
    r1 = x - hi.astype(F32)
    mid = r1.astype(BF16)
    lo = (r1 - mid.astype(F32)).astype(BF16)
    return hi, mid, lo


def _split2(x):
    hi = x.astype(BF16)
    lo = (x - hi.astype(F32)).astype(BF16)
    return hi, lo


def _dg(a, b, dims):
    return lax.dot_general(a, b, (dims, ((), ())), preferred_element_type=F32)


_NN = ((1,), (0,))
_NT = ((1,), (1,))
_TN = ((0,), (0,))


def _mm(a, b, dims=_NN, passes=3):
    if passes == 1:
        return _dg(a.astype(BF16), b.astype(BF16), dims)
    if passes == 3:
        ah, al = _split2(a)
        bh, bl = _split2(b)
        return _dg(ah, bh, dims) + (_dg(ah, bl, dims) + _dg(al, bh, dims))
    ah, am, al = _split3(a)
    bh, bm, bl = _split3(b)
    return (_dg(ah, bh, dims) + (_dg(ah, bm, dims) + _dg(am, bh, dims))
            + (_dg(am, bm, dims) + _dg(ah, bl, dims) + _dg(al, bh, dims)))


def _pieces(x, passes):
    return (x.astype(BF16),) if passes == 1 else _split2(x)


def _mmp(a, b, dims):
    out = _dg(a[0], b[0], dims)
    if len(a) > 1 and len(b) > 1:
        return out + (_dg(a[0], b[1], dims) + _dg(a[1], b[0], dims))
    if len(a) > 1:
        return out + _dg(a[1], b[0], dims)
    if len(b) > 1:
        return out + _dg(a[0], b[1], dims)
    return out


def _mm_exact_lhs(a_bf16, b, dims=_NN):
    bh, bm, bl = _split3(b)
    return _dg(a_bf16, bh, dims) + (_dg(a_bf16, bm, dims) + _dg(a_bf16, bl, dims))


def _mm_exact_rhs(a, b_bf16, dims=_NN):
    ah, am, al = _split3(a)
    return _dg(ah, b_bf16, dims) + (_dg(am, b_bf16, dims) + _dg(al, b_bf16, dims))


def _sigmoid(x):
    return 1.0 / (1.0 + jnp.exp(-x))


def _layer_norm(x):
    mu = jnp.mean(x, axis=-1, keepdims=True)
    xc = x - mu
    var = jnp.mean(xc * xc, axis=-1, keepdims=True)
    return xc * lax.rsqrt(var + LN_EPS)


def _ada_kernel(c_ref, w_ref, b_ref, o_ref):
    c = c_ref[...]
    o_ref[...] = _mm(c * _sigmoid(c), w_ref[...], passes=6) + b_ref[...]


def _ada_mod(c_rows, w_ada, b_ada):
    R, D = c_rows.shape
    N = w_ada.shape[1]
    tn = 1024
    return pl.pallas_call(
        _ada_kernel,
        grid=(N // tn,),
        in_specs=[pl.BlockSpec((R, D), lambda j: (0, 0)),
                  pl.BlockSpec((D, tn), lambda j: (0, j)),
                  pl.BlockSpec((1, tn), lambda j: (0, j))],
        out_specs=pl.BlockSpec((R, tn), lambda j: (0, j)),
        out_shape=jax.ShapeDtypeStruct((R, N), F32),
        compiler_params=_params(("arbitrary",)),
        name="ada_mod",
    )(c_rows, w_ada, b_ada.reshape(1, N))


def _rope(t, cos, sin_lo, sin_hi, reps):
    w = t.shape[-1]
    tile = lambda a: jnp.concatenate([a] * reps, axis=1) if reps > 1 else a
    return (t * tile(cos) + pltpu.roll(t, w - ROPE_PAIRS, 1) * tile(sin_lo)
            + pltpu.roll(t, ROPE_PAIRS, 1) * tile(sin_hi))


def _inproj_kernel(x_ref, ctx_ref, sh_ref, sc_ref, w_ref, bones_ref, qg_ref, kg_ref,
                   cos_ref, slo_ref, shi_ref, q_ref, k_ref, v_ref, rw_ref, *, n_ctx_tiles):
    i = pl.program_id(1)
    x = jnp.where(i < n_ctx_tiles, ctx_ref[0], x_ref[0])
    h = _layer_norm(x) * (1.0 + sc_ref[0]) + sh_ref[0]
    p = _dg(h.astype(BF16), w_ref[...], _NN)
    bones = bones_ref[...]
    cos, slo, shi = cos_ref[...], slo_ref[...], shi_ref[...]

    q = p[:, :D_ATT]
    ss = _mm_exact_rhs(q * q, bones)
    q = q * lax.rsqrt(ss * (1.0 / HEAD_DIM) + RMS_EPS) * qg_ref[...]
    q = _rope(q, cos, slo, shi, D_ATT // 128) * (ATT_SCALE * LOG2_E)
    q_ref[0] = q.astype(BF16)

    k = p[:, D_ATT:D_ATT + D_ATT_KV]
    ss = _mm_exact_rhs(k * k, bones[:D_ATT_KV, :D_ATT_KV])
    k = k * lax.rsqrt(ss * (1.0 / HEAD_DIM) + RMS_EPS) * kg_ref[...]
    k = _rope(k, cos, slo, shi, D_ATT_KV // 128).astype(BF16)
    v = p[:, D_ATT + D_ATT_KV:D_ATT + 2 * D_ATT_KV].astype(BF16)
    for g in range(ATT_KV_HEADS):
        k_ref[0, g] = k[:, g * HEAD_DIM:(g + 1) * HEAD_DIM]
        v_ref[0, g] = v[:, g * HEAD_DIM:(g + 1) * HEAD_DIM]
    rw_ref[0] = p[:, D_ATT + 2 * D_ATT_KV:]


def _inproj(x, ctx, shift, scale, w_in_bf16, bones, qg, kg, cos, slo, shi):
    B, L, D = x.shape
    n_ctx = ctx.shape[1]
    tm = ROW_TILE
    nct = n_ctx // tm
    Tc = n_ctx + L
    d_in = w_in_bf16.shape[1]
    d_rw = d_in - D_ATT - 2 * D_ATT_KV
    mod_row = lambda b, i: (jnp.where(i < nct, B, b), 0, 0)
    const = lambda b, i: (0, 0)
    return pl.pallas_call(
        functools.partial(_inproj_kernel, n_ctx_tiles=nct),
        grid=(B, Tc // tm),
        in_specs=[pl.BlockSpec((1, tm, D), lambda b, i: (b, jnp.maximum(i - nct, 0), 0)),
                  pl.BlockSpec((1, tm, D), lambda b, i: (b, jnp.minimum(i, nct - 1), 0)),
                  pl.BlockSpec((1, 1, D), mod_row),
                  pl.BlockSpec((1, 1, D), mod_row),
                  pl.BlockSpec((D, d_in), const),
                  pl.BlockSpec(bones.shape, const),
                  pl.BlockSpec((1, D_ATT), const),
                  pl.BlockSpec((1, D_ATT_KV), const),
                  pl.BlockSpec((tm, 128), lambda b, i: (i, 0)),
                  pl.BlockSpec((tm, 128), lambda b, i: (i, 0)),
                  pl.BlockSpec((tm, 128), lambda b, i: (i, 0))],
        out_specs=[pl.BlockSpec((1, tm, D_ATT), lambda b, i: (b, jnp.maximum(i - nct, 0), 0)),
                   pl.BlockSpec((1, ATT_KV_HEADS, tm, HEAD_DIM), lambda b, i: (b, 0, i, 0)),
                   pl.BlockSpec((1, ATT_KV_HEADS, tm, HEAD_DIM), lambda b, i: (b, 0, i, 0)),
                   pl.BlockSpec((1, tm, d_rw), lambda b, i: (b, i, 0))],
        out_shape=[jax.ShapeDtypeStruct((B, L, D_ATT), BF16),
                   jax.ShapeDtypeStruct((B, ATT_KV_HEADS, Tc, HEAD_DIM), BF16),
                   jax.ShapeDtypeStruct((B, ATT_KV_HEADS, Tc, HEAD_DIM), BF16),
                   jax.ShapeDtypeStruct((B, Tc, d_rw), F32)],
        compiler_params=_params(("arbitrary", "arbitrary")),
        name="inproj",
    )(x, ctx, shift, scale, w_in_bf16, bones, qg, kg, cos, slo, shi)


def _attn_kernel(q_ref, k_ref, vt_ref, o_ref, s0, s1, p0, p1, m0, m1):
    i = pl.program_id(2)
    nk, tk, cols = s0.shape

    @pl.when(i == 0)
    def _():
        for ref in (s0, s1, m0, m1):
            ref[...] = jnp.zeros_like(ref)
        p0[...] = jnp.ones_like(p0)
        p1[...] = jnp.ones_like(p1)

    def step(s_w, m_w, s_r, m_r, p_w, p_r):
        q = q_ref[0]
        tq = q.shape[0]
        qs = jnp.concatenate([q[:, h * HEAD_DIM:(h + 1) * HEAD_DIM] for h in range(GQA_GROUP)], axis=0)
        m_prev = m_r[0:1, :]

        def body(j, carry):
            m_run, acc = carry
            off = pl.multiple_of(j * tk, tk)
            s_new = _dg(k_ref[0, 0, pl.ds(off, tk), :], qs, _NT)
            s_w[j] = s_new
            m_run = jnp.maximum(m_run, jnp.max(s_new.reshape(tk // 8, 8, cols), axis=0))
            p_w[j] = jnp.exp2(s_r[j] - m_prev).astype(BF16)
            acc = acc + _dg(vt_ref[0, 0, j], p_r[j], _NN)
            return m_run, acc

        init = (jnp.full((8, cols), -jnp.inf, F32), jnp.zeros((vt_ref.shape[3], cols), F32))
        m_run, acc = lax.fori_loop(0, nk, body, init, unroll=8)
        m_w[...] = jnp.broadcast_to(jnp.max(m_run, axis=0, keepdims=True), m_w.shape)
        o = acc[:HEAD_DIM] / acc[HEAD_DIM:HEAD_DIM + 1]
        o_ref[0] = jnp.concatenate([o[:, h * tq:(h + 1) * tq].T for h in range(GQA_GROUP)],
                                   axis=1).astype(BF16)

    @pl.when(i % 2 == 0)
    def _():
        step(s0, m0, s1, m1, p1, p0)

    @pl.when(i % 2 == 1)
    def _():
        step(s1, m1, s0, m0, p0, p1)


def _attention(q, k, vt):
    B, L, _ = q.shape
    Tc = k.shape[2]
    tq = ATT_Q_TILE
    n = L // tq
    gw = GQA_GROUP * HEAD_DIM
    cols = GQA_GROUP * tq
    tk = ATT_K_TILE
    nk = Tc // tk
    return pl.pallas_call(
        _attn_kernel,
        grid=(B, ATT_KV_HEADS, n + 2),
        in_specs=[pl.BlockSpec((1, tq, gw), lambda b, g, i: (b, jnp.minimum(i, n - 1), g)),
                  pl.BlockSpec((1, 1, Tc, HEAD_DIM), lambda b, g, i: (b, g, 0, 0)),
                  pl.BlockSpec((1, 1, nk, vt.shape[3], tk), lambda b, g, i: (b, g, 0, 0, 0))],
        out_specs=pl.BlockSpec((1, tq, gw), lambda b, g, i: (b, jnp.clip(i - 2, 0, n - 1), g)),
        out_shape=jax.ShapeDtypeStruct((B, L, D_ATT), BF16),
        scratch_shapes=[pltpu.VMEM((nk, tk, cols), F32), pltpu.VMEM((nk, tk, cols), F32),
                        pltpu.VMEM((nk, tk, cols), BF16), pltpu.VMEM((nk, tk, cols), BF16),
                        pltpu.VMEM((8, cols), F32), pltpu.VMEM((8, cols), F32)],
        compiler_params=_params(("arbitrary", "arbitrary", "arbitrary")),
        name="attention",
    )(q, k, vt)


def _rwkv_prep_kernel(rw_ref, prev_ref, next_ref, mu_ref, w0_ref, dup_ref, a0_ref, aup_ref, gup_ref,
                      kk_ref, ka_ref, rk_ref, bones_ref,
                      sh_o, g_o, bonus_o, dir_o, *, n_ctx, n_tok):
    i = pl.program_id(1)
    p = rw_ref[0]
    tm = p.shape[0]
    t0 = i * tm
    has_prev = jnp.where((t0 == 0) | (t0 == n_ctx), 0.0, 1.0)
    has_next = jnp.where((t0 + tm == n_ctx) | (t0 + tm == n_tok), 0.0, 1.0)
    row = lax.broadcasted_iota(jnp.int32, (tm, 1), 0)
    up = jnp.where(row == 0, prev_ref[0, 7:8, :] * has_prev, pltpu.roll(p, 1, 0))
    dn = jnp.where(row == tm - 1, next_ref[0, 0:1, :] * has_next, pltpu.roll(p, tm - 1, 0))
    s = p + (0.5 * (up + dn) - p) * mu_ref[...]

    D = D_RWKV
    r, k, v = s[:, :D], s[:, D:2 * D], s[:, 2 * D:3 * D]
    lora = s[:, 3 * D:3 * D + D_LORA]
    gl = s[:, 3 * D + D_LORA:]
    bones = bones_ref[...]

    g_o[0] = _mm(_sigmoid(gl), gup_ref[...], passes=3)
    kk = k * kk_ref[...]
    nrm = jnp.sqrt(_mm_exact_rhs(kk * kk, bones))
    kk = kk / jnp.maximum(nrm, L2_EPS)
    wt = jnp.tanh(lora)
    sh_o[0, :, 0:D] = r
    sh_o[0, :, D:2 * D] = v
    sh_o[0, :, 2 * D:3 * D] = kk
    bonus_o[0] = _mm_exact_rhs(r * k * rk_ref[...], bones) * v
    for d in range(2):
        z = w0_ref[d:d + 1, :] + _mm(wt, dup_ref[d], passes=3)
        softplus_neg = jnp.maximum(-z, 0.0) + jnp.log(1.0 + jnp.exp(-jnp.abs(z)))
        w = -softplus_neg - 0.5
        dir_o[d, 0, :, 0:D] = -jnp.exp(w)
        a = _sigmoid(a0_ref[d:d + 1, :] + _mm(lora, aup_ref[d], passes=3))
        dir_o[d, 0, :, D:2 * D] = k * (1.0 + (a - 1.0) * ka_ref[...])
        dir_o[d, 0, :, 2 * D:3 * D] = kk * a


def _rwkv_prep(rw, n_ctx, mu, w0, dup, a0, aup, gup, k_k, k_a, r_k, bones):
    B, Tc, d_rw = rw.shape
    tm = ROW_TILE
    D = D_RWKV
    const2 = lambda b, i: (0, 0)
    const3 = lambda b, i: (0, 0, 0)
    tok = lambda n: pl.BlockSpec((1, tm, n * D), lambda b, i: (b, i, 0))
    tok2 = pl.BlockSpec((2, 1, tm, 3 * D), lambda b, i: (0, b, i, 0))
    s1 = lambda n: jax.ShapeDtypeStruct((B, Tc, n * D), F32)
    s2 = jax.ShapeDtypeStruct((2, B, Tc, 3 * D), F32)
    hb = tm // 8
    return pl.pallas_call(
        functools.partial(_rwkv_prep_kernel, n_ctx=n_ctx, n_tok=Tc),
        grid=(B, Tc // tm),
        in_specs=[pl.BlockSpec((1, tm, d_rw), lambda b, i: (b, i, 0)),
                  pl.BlockSpec((1, 8, d_rw), lambda b, i: (b, jnp.maximum(i * hb - 1, 0), 0)),
                  pl.BlockSpec((1, 8, d_rw), lambda b, i: (b, jnp.minimum((i + 1) * hb, Tc // 8 - 1), 0)),
                  pl.BlockSpec((1, d_rw), const2),
                  pl.BlockSpec((2, D), const2),
                  pl.BlockSpec((2, D_LORA, D), const3),
                  pl.BlockSpec((2, D), const2),
                  pl.BlockSpec((2, D_LORA, D), const3),
                  pl.BlockSpec((D_LORA, D), const2),
                  pl.BlockSpec((1, D), const2),
                  pl.BlockSpec((1, D), const2),
                  pl.BlockSpec((1, D), const2),
                  pl.BlockSpec(bones.shape, const2)],
        out_specs=[tok(3), tok(1), tok(1), tok2],
        out_shape=[s1(3), s1(1), s1(1), s2],
        compiler_params=_params(("arbitrary", "arbitrary")),
        name="rwkv_prep",
    )(rw, rw, rw, mu, w0, dup, a0, aup, gup, k_k, k_a, r_k, bones)


def _wkv_kernel(sh0, dir0, sh1, dir1, y0, y1, s_ref, *, n_heads):
    @pl.when(pl.program_id(1) == 0)
    def _():
        s_ref[...] = jnp.zeros_like(s_ref)

    C = WKV_CHUNK
    n_sub = sh0.shape[1] // C
    lax.fori_loop(0, n_sub, functools.partial(_wkv_chunk, refs=(sh0, dir0, sh1, dir1, y0, y1, s_ref),
                                              n_heads=n_heads, n_sub=n_sub), 0)


def _wkv_chunk(sub, carry, *, refs, n_heads, n_sub):
    sh0, dir0, sh1, dir1, y0, y1, s_ref = refs
    C = WKV_CHUNK
    N = RWKV_HEAD
    D = n_heads * N
    row = lax.broadcasted_iota(jnp.int32, (C, 2 * N), 0)
    lane = lax.broadcasted_iota(jnp.int32, (C, 2 * N), 1)
    col = lane % N
    first_half = lane < N
    trow = lax.broadcasted_iota(jnp.int32, (C, C), 0)
    tcol = lax.broadcasted_iota(jnp.int32, (C, C), 1)
    cat = lambda xs, ys: tuple(jnp.concatenate([x, y], axis=0) for x, y in zip(xs, ys))
    offs = (pl.multiple_of(sub * C, C), pl.multiple_of((n_sub - 1 - sub) * C, C))
    y_refs = (y0, y1)

    chains = []
    for d, (sh_ref, dir_ref) in enumerate(((sh0, dir0), (sh1, dir1))):
        rows = pl.ds(offs[d], C)
        strict, incl = (row < col, row <= col) if d else (row > col, row >= col)
        tri = (trow <= tcol) if d else (trow >= tcol)
        lw = dir_ref[0, 0, rows, 0:D]
        cl = _mm_exact_lhs(jnp.where(tri, 1.0, 0.0).astype(BF16), lw)
        ecl = jnp.exp(cl)
        eneg = jnp.exp(-cl)
        last = 0 if d else C - 1
        etot = ecl[last:last + 1, :]
        kt = dir_ref[0, 0, rows, D:2 * D] * eneg
        bt = dir_ref[0, 0, rows, 2 * D:3 * D] * eneg
        full = dict(at=_split2(-(sh_ref[0, rows, 2 * D:3 * D] * jnp.exp(cl - lw))),
                    rt=_split2(sh_ref[0, rows, 0:D] * ecl),
                    kt=_split2(kt), bt=_split2(bt), v=_split2(sh_ref[0, rows, D:2 * D]),
                    kh=_split2(kt * etot), bh=_split2(bt * etot))
        for h in range(n_heads):
            ch = {k: tuple(p[:, h * N:(h + 1) * N] for p in val) for k, val in full.items()}
            ch.update(d=d, h=h, strict=strict, incl=incl, etot=etot[:, h * N:(h + 1) * N], s0=s_ref[d, h])
            chains.append(ch)

    for ch in chains:
        rhs = tuple(jnp.concatenate([k, b, s], axis=0)
                    for k, b, s in zip(ch["kt"], ch["bt"], _split2(ch["s0"])))
        g = _mmp(cat(ch["at"], ch["rt"]), rhs, _NT)
        ch["top"] = jnp.where(ch["strict"], g[:C, :2 * N], 0.0)
        ch["bot"] = jnp.where(ch["incl"], g[C:, :2 * N], 0.0)
        ch["as0"], ch["rs0"] = g[:C, 2 * N:], g[C:, 2 * N:]
    for ch in chains:
        z = ch["as0"] + _mmp(_split2(ch["top"][:, :N]), ch["v"], _NN)
        ch["w"] = jnp.where(first_half, jnp.concatenate([z, jnp.zeros_like(z)], axis=1), ch["top"])
        ch["pw"] = pltpu.roll(ch["top"], N, 1)[:, :N]
    n = 1
    while True:
        for ch in chains:
            ch["prod"] = _mmp(_split2(ch["pw"]), _split2(ch["w"]), _NN)
            ch["w"] = jnp.where(first_half, ch["w"], 0.0) + ch["prod"]
        n *= 2
        if n >= C:
            break
        for ch in chains:
            ch["pw"] = pltpu.roll(ch["prod"], N, 1)[:, :N]
    for ch in chains:
        ch["up"] = _split2(ch["w"][:, :N])
        y = ch["rs0"] + _mmp(_split2(ch["bot"]), cat(ch["v"], ch["up"]), _NN)
        y_refs[ch["d"]][0, pl.ds(offs[ch["d"]], C), ch["h"] * N:(ch["h"] + 1) * N] = y
    for ch in chains:
        upd = _mmp(cat(ch["up"], ch["v"]), cat(ch["bh"], ch["kh"]), _TN)
        s_ref[ch["d"], ch["h"]] = ch["s0"] * ch["etot"] + upd
    return carry


def _wkv_scan(shared, per_dir, n_ctx):
    B, T, D3 = shared.shape
    D = D3 // 3
    tb = WKV_BLOCK
    n_blocks = T // tb
    n_cb = n_ctx // tb
    tok = (lambda c: c,
           lambda c: jnp.where(c < n_cb, n_cb - 1 - c, n_blocks - 1 + n_cb - c))
    in_specs, args = [], []
    for d in range(2):
        in_specs += [pl.BlockSpec((1, tb, D3), lambda bi, c, d=d: (bi, tok[d](c), 0)),
                     pl.BlockSpec((1, 1, tb, D3), lambda bi, c, d=d: (d, bi, tok[d](c), 0))]
        args += [shared, per_dir]
    y = jax.ShapeDtypeStruct((B, T, D), F32)
    return pl.pallas_call(
        functools.partial(_wkv_kernel, n_heads=D // RWKV_HEAD),
        grid=(B, n_blocks),
        in_specs=in_specs,
        out_specs=[pl.BlockSpec((1, tb, D), lambda bi, c, d=d: (bi, tok[d](c), 0)) for d in range(2)],
        out_shape=[y, y],
        scratch_shapes=[pltpu.VMEM((2, D // RWKV_HEAD, RWKV_HEAD, RWKV_HEAD), F32)],
        compiler_params=_params(("arbitrary", "arbitrary")),
        name="wkv_scan",
    )(*args)


def _post_kernel(yf_ref, yb_ref, bonus_ref, g_ref, att_ref, x_ref, gt_ref, sh_ref, sc_ref, wo_ref,
                 lxg_ref, lxb_ref, l1g_ref, l1b_ref, rwt_ref, bones_ref, x1_ref, h2_ref, aff_ref):
    bones = bones_ref[...]
    y = yf_ref[0] + yb_ref[0]
    inv = 1.0 / RWKV_HEAD
    mu = _mm_exact_rhs(y, bones) * inv
    yc = y - mu
    var = _mm_exact_rhs(yc * yc, bones) * inv
    yn = yc * lax.rsqrt(var + GN_EPS) * lxg_ref[...] + lxb_ref[...]
    rw_out = (yn + bonus_ref[0]) * g_ref[0]
    mix = _dg(att_ref[0], wo_ref[:D_ATT, :], _NN) + _dg(rw_out.astype(BF16), wo_ref[D_ATT:, :], _NN)
    x1 = _layer_norm(ALPHA * x_ref[0] + gt_ref[0] * mix) * l1g_ref[...] + l1b_ref[...]
    x1_ref[0] = x1
    h2 = _layer_norm(x1) * (1.0 + sc_ref[0]) + sh_ref[0]
    h2_ref[0] = h2.astype(BF16)
    logits = _mm(rwt_ref[...], h2, _NT, passes=6)
    e = jnp.exp(logits - jnp.max(logits, axis=0, keepdims=True))
    aff_ref[0] = e / jnp.sum(e, axis=0, keepdims=True)


def _post(yf, yb, bonus, g, att, x, gt1, sh2, sc2, w_out_bf16, lnx_g, lnx_b, ln1_g, ln1_b, router_wt,
          bones, n_ctx):
    B, L, D = x.shape
    tm = ROW_TILE
    nct = n_ctx // tm
    E = router_wt.shape[0]
    const = lambda b, i: (0, 0)
    cat = pl.BlockSpec((1, tm, D_RWKV), lambda b, i: (b, i + nct, 0))
    mod = pl.BlockSpec((1, 1, D), lambda b, i: (b, 0, 0))
    vec = lambda n: pl.BlockSpec((1, n), const)
    return pl.pallas_call(
        _post_kernel,
        grid=(B, L // tm),
        in_specs=[cat, cat, cat, cat,
                  pl.BlockSpec((1, tm, D_ATT), lambda b, i: (b, i, 0)),
                  pl.BlockSpec((1, tm, D), lambda b, i: (b, i, 0)),
                  mod, mod, mod,
                  pl.BlockSpec(w_out_bf16.shape, const),
                  vec(D_RWKV), vec(D_RWKV), vec(D), vec(D),
                  pl.BlockSpec((E, D), const),
                  pl.BlockSpec(bones.shape, const)],
        out_specs=[pl.BlockSpec((1, tm, D), lambda b, i: (b, i, 0)),
                   pl.BlockSpec((1, tm, D), lambda b, i: (b, i, 0)),
                   pl.BlockSpec((1, E, tm), lambda b, i: (b, 0, i))],
        out_shape=[jax.ShapeDtypeStruct((B, L, D), F32),
                   jax.ShapeDtypeStruct((B, L, D), BF16),
                   jax.ShapeDtypeStruct((B, E, L), F32)],
        compiler_params=_params(("arbitrary", "arbitrary")),
        name="post_mix",
    )(yf, yb, bonus, g, att, x, gt1, sh2, sc2, w_out_bf16, lnx_g, lnx_b, ln1_g, ln1_b, router_wt, bones)


def _cumsum_lanes(x):
    n = x.shape[1]
    lane = lax.broadcasted_iota(jnp.int32, x.shape, 1)
    s = 1
    while s < n:
        x = x + jnp.where(lane >= s, pltpu.roll(x, s, 1), 0)
        s *= 2
    return x


def _route_kernel(aff_ref, pos_ref, gate_ref, cum_ref, *, cap, tile):
    aff = aff_ref[0]
    E = aff.shape[0]
    count = lambda m: jnp.sum(jnp.where(m, 1, 0), axis=1, keepdims=True)
    thr_bits = jnp.zeros((E, 1), jnp.int32)
    for bit in range(29, -1, -1):
        cand = thr_bits | (1 << bit)
        cand_f = lax.bitcast_convert_type(cand, F32)
        thr_bits = jnp.where(count(aff >= cand_f) >= cap, cand, thr_bits)
    thr = lax.bitcast_convert_type(thr_bits, F32)
    above = aff > thr
    tie = aff == thr
    need = cap - count(above)
    tie_rank = _cumsum_lanes(jnp.where(tie, 1, 0))
    sel = above | (tie & (tie_rank <= need))
    slot = _cumsum_lanes(jnp.where(sel, 1, 0)) - 1
    pos = jnp.where(sel, slot, -1)
    gate = jnp.where(sel, aff, 0.0)
    for e in range(E):
        pos_ref[0, e] = pos[e:e + 1, :]
        gate_ref[0, e] = gate[e:e + 1, :]
    lane = lax.broadcasted_iota(jnp.int32, (E, 128), 1)
    chosen = jnp.where(sel, 1, 0)
    run = jnp.zeros((E, 1), jnp.int32)
    cum = jnp.zeros((E, 128), jnp.int32)
    n_tiles = aff.shape[1] // tile
    for i in range(n_tiles):
        cum = jnp.where(lane == i, run, cum)
        run = run + jnp.sum(chosen[:, i * tile:(i + 1) * tile], axis=1, keepdims=True)
    cum_ref[0] = jnp.where(lane == n_tiles, run, cum)


def _route(aff, cap, tile):
    B, E, L = aff.shape
    out = pl.BlockSpec((1, E, 1, L), lambda b: (b, 0, 0, 0))
    return pl.pallas_call(
        functools.partial(_route_kernel, cap=cap, tile=tile),
        grid=(B,),
        in_specs=[pl.BlockSpec((1, E, L), lambda b: (b, 0, 0))],
        out_specs=[out, out, pl.BlockSpec((1, E, 128), lambda b: (b, 0, 0))],
        out_shape=[jax.ShapeDtypeStruct((B, E, 1, L), jnp.int32),
                   jax.ShapeDtypeStruct((B, E, 1, L), F32),
                   jax.ShapeDtypeStruct((B, E, 128), jnp.int32)],
        compiler_params=_params(("arbitrary",)),
        name="route",
    )(aff)


def _expert_kernel(h_ref, pos_ref, wg_ref, wu_ref, wd_ref, y_ref, *, cap):
    pos = pos_ref[0, 0]
    slot = lax.broadcasted_iota(jnp.int32, (cap, pos.shape[1]), 0)
    onehot = jnp.where(pos == slot, 1.0, 0.0).astype(BF16)
    xin = _dg(onehot, h_ref[0], _NN).astype(BF16)
    hg = _dg(xin, wg_ref[0], _NN)
    hu = _dg(xin, wu_ref[0], _NN)
    hid = (hg * _sigmoid(hg) * hu).astype(BF16)
    y_ref[0, 0] = _dg(hid, wd_ref[0], _NN).astype(BF16)


def _experts(h2, pos, wg, wu, wd, cap):
    B, L, D = h2.shape
    E, _, F = wg.shape
    rowv = pl.BlockSpec((1, 1, 1, L), lambda b, e: (b, e, 0, 0))
    return pl.pallas_call(
        functools.partial(_expert_kernel, cap=cap),
        grid=(B, E),
        in_specs=[pl.BlockSpec((1, L, D), lambda b, e: (b, 0, 0)),
                  rowv,
                  pl.BlockSpec((1, D, F), lambda b, e: (e, 0, 0)),
                  pl.BlockSpec((1, D, F), lambda b, e: (e, 0, 0)),
                  pl.BlockSpec((1, F, D), lambda b, e: (e, 0, 0))],
        out_specs=pl.BlockSpec((1, 1, cap, D), lambda b, e: (b, e, 0, 0)),
        out_shape=jax.ShapeDtypeStruct((B, E, cap, D), BF16),
        compiler_params=_params(("arbitrary", "arbitrary")),
        name="experts",
    )(h2, pos, wg, wu, wd)


def _combine_kernel(cum_ref, pos_ref, gate_ref, y_ref, x1_ref, gt_ref, lg_ref, lb_ref, o_ref, acc_ref, *, cap):
    b, i, e = pl.program_id(0), pl.program_id(1), pl.program_id(2)

    @pl.when(e == 0)
    def _():
        acc_ref[...] = jnp.zeros_like(acc_ref)

    base = (b * pl.num_programs(2) + e) * 128
    lo, hi = cum_ref[base + i], cum_ref[base + i + 1]
    window = min(COMBINE_WINDOW, cap)
    align = min(128, window)
    start = pl.multiple_of(jnp.minimum((lo // align) * align, cap - window), align)

    def accumulate(first, width):
        pos = pos_ref[0, 0]
        slot = first + lax.broadcasted_iota(jnp.int32, (width, pos.shape[1]), 0)
        weights = jnp.where(pos == slot, gate_ref[0, 0], 0.0).astype(BF16)
        acc_ref[...] += _dg(weights, y_ref[0, 0, pl.ds(first, width), :], _TN)

    @pl.when((hi > lo) & (hi - start <= window))
    def _():
        accumulate(start, window)

    @pl.when(hi - start > window)
    def _():
        accumulate(0, cap)

    @pl.when(e == pl.num_programs(2) - 1)
    def _():
        z = ALPHA * x1_ref[0] + gt_ref[0] * acc_ref[...]
        o_ref[0] = _layer_norm(z) * lg_ref[...] + lb_ref[...]


def _combine(cum, pos, gate, y, x1, gt2, ln2_g, ln2_b, cap):
    B, L, D = x1.shape
    E = y.shape[1]
    tl = min(COMBINE_TILE, L)
    const = lambda b, i, e, cum: (0, 0)
    grid_spec = pltpu.PrefetchScalarGridSpec(
        num_scalar_prefetch=1,
        grid=(B, L // tl, E),
        in_specs=[pl.BlockSpec((1, 1, 1, tl), lambda b, i, e, cum: (b, e, 0, i)),
                  pl.BlockSpec((1, 1, 1, tl), lambda b, i, e, cum: (b, e, 0, i)),
                  pl.BlockSpec((1, 1, cap, D), lambda b, i, e, cum: (b, e, 0, 0)),
                  pl.BlockSpec((1, tl, D), lambda b, i, e, cum: (b, i, 0)),
                  pl.BlockSpec((1, 1, D), lambda b, i, e, cum: (b, 0, 0)),
                  pl.BlockSpec((1, D), const),
                  pl.BlockSpec((1, D), const)],
        out_specs=pl.BlockSpec((1, tl, D), lambda b, i, e, cum: (b, i, 0)),
        scratch_shapes=[pltpu.VMEM((tl, D), F32)])
    return pl.pallas_call(
        functools.partial(_combine_kernel, cap=cap),
        grid_spec=grid_spec,
        out_shape=jax.ShapeDtypeStruct((B, L, D), F32),
        compiler_params=_params(("arbitrary", "arbitrary", "arbitrary")),
        name="combine",
    )(cum, pos, gate, y, x1, gt2, ln2_g, ln2_b)


def _rope_tables(n_ctx, n_lat):
    lane = jnp.arange(128)
    j = lane % HEAD_DIM
    axis = j // (2 * ROPE_PAIRS)
    upper = (j % (2 * ROPE_PAIRS)) >= ROPE_PAIRS
    inv = ROPE_THETA ** (-(j % ROPE_PAIRS).astype(F32) / ROPE_PAIRS)
    t = jnp.arange(n_lat)
    coord = jnp.where(axis[None, :] == 0, (t // GRID_W)[:, None], (t % GRID_W)[:, None]).astype(F32)
    ang = coord * inv[None, :]
    cos, sin = jnp.cos(ang), jnp.sin(ang)
    s_lo = jnp.where(upper[None, :], 0.0, -sin)
    s_hi = jnp.where(upper[None, :], sin, 0.0)
    pad = lambda a, fill: jnp.concatenate([jnp.full((n_ctx, 128), fill, F32), a], axis=0)
    return pad(cos, 1.0), pad(s_lo, 0.0), pad(s_hi, 0.0)


def _head_ones(n):
    h = jnp.arange(n) // HEAD_DIM
    return (h[:, None] == h[None, :]).astype(BF16)


def kernel(x, c, ctx, c_ctx, w_ada, b_ada, w_in, q_gain, k_gain, tshift_mu, decay_w0, decay_up, iclr_a0,
           iclr_up, gate_up, k_k, k_a, r_k, lnx_g, lnx_b, w_out, ln1_g, ln1_b, router_w, exp_w_gate,
           exp_w_up, exp_w_down, ln2_g, ln2_b):
    B, L, D = x.shape
    n_ctx = ctx.shape[1]
    li = 0
    row = lambda a: a.reshape(1, -1)

    n_rows = -(-(B + 1) // 8) * 8
    c_rows = jnp.concatenate([c, c_ctx[None, :], jnp.zeros((n_rows - B - 1, D), F32)], axis=0)
    mod = _ada_mod(c_rows, w_ada[li], b_ada[li]).reshape(n_rows, N_MOD, 1, D)
    sh1, sc1, gt1, sh2, sc2, gt2 = (mod[:, m] for m in range(N_MOD))

    bones = _head_ones(D_RWKV)
    cos, s_lo, s_hi = _rope_tables(n_ctx, L)
    q, k, v, rw = _inproj(x, ctx, sh1, sc1, w_in[li].astype(BF16), bones,
                          row(jnp.tile(q_gain[li], ATT_HEADS)), row(jnp.tile(k_gain[li], ATT_KV_HEADS)),
                          cos, s_lo, s_hi)
    Tc = n_ctx + L
    vt = v.reshape(B, ATT_KV_HEADS, Tc // ATT_K_TILE, ATT_K_TILE, HEAD_DIM).swapaxes(-1, -2)
    vt = jnp.concatenate([vt, jnp.ones(vt.shape[:3] + (16, ATT_K_TILE), BF16)], axis=3)
    att = _attention(q, k, vt)

    zpad = jnp.zeros((2, D_LORA // 2, D_RWKV), F32)
    dup = jnp.concatenate([decay_up[li], zpad], axis=1)
    aup = jnp.concatenate([zpad, iclr_up[li]], axis=1)
    scan_shared, g, bonus, scan_dir = _rwkv_prep(
        rw, n_ctx, row(tshift_mu[li]), decay_w0[li], dup, iclr_a0[li], aup, gate_up[li],
        row(k_k[li]), row(k_a[li]), row(r_k[li]), bones)
    yf, yb = _wkv_scan(scan_shared, scan_dir, n_ctx)

    x1, h2, aff = _post(yf, yb, bonus, g, att, x, gt1, sh2, sc2, w_out[li].astype(BF16),
                        row(lnx_g[li]), row(lnx_b[li]), row(ln1_g[li]), row(ln1_b[li]),
                        router_w[li].T, bones, n_ctx)
    cap = CAPACITY_FACTOR * L // N_EXPERTS
    pos, gate, cum = _route(aff, cap, min(COMBINE_TILE, L))
    y = _experts(h2, pos, exp_w_gate[li].astype(BF16), exp_w_up[li].astype(BF16),
                 exp_w_down[li].astype(BF16), cap)
    return _combine(cum.reshape(-1), pos, gate, y, x1, gt2, row(ln2_g[li]), row(ln2_b[li]), cap)
```

```python
import functools

import jax
import jax.numpy as jnp
from jax import lax
from jax.experimental import pallas as pl
from jax.experimental.pallas import tpu as pltpu

F32 = jnp.float32
BF16 = jnp.bfloat16

HEAD_DIM = 64
ATT_HEADS = 8
ATT_KV_HEADS = 2
GQA_GROUP = ATT_HEADS // ATT_KV_HEADS
D_ATT = ATT_HEADS * HEAD_DIM
D_ATT_KV = ATT_KV_HEADS * HEAD_DIM
ATT_SCALE = HEAD_DIM ** -0.5
LOG2_E = 1.4426950408889634
EXP_M05 = 0.6065306597126334
ROPE_THETA = 10000.0
ROPE_PAIRS = HEAD_DIM // 4
GRID_W = 64
RWKV_HEAD = 64
D_RWKV = 512
D_LORA = 128
N_EXPERTS = 16
CAPACITY_FACTOR = 2
N_MOD = 6
LN_EPS = 1e-5
RMS_EPS = 1e-6
GN_EPS = 64e-5
L2_EPS = 1e-12
DEPTH = 1
ALPHA = (2.0 * DEPTH) ** 0.25

WKV_CHUNK = 64
WKV_BLOCK = 256
ROW_TILE = 256
POST_TILE = 512
ATT_Q_TILE = 128
ATT_K_TILE = 256
COMBINE_TILE = 1024
COMBINE_WINDOW = 256
VMEM_LIMIT = 48 * 1024 * 1024


def _params(sem):
    return pltpu.CompilerParams(dimension_semantics=sem, vmem_limit_bytes=VMEM_LIMIT)


def _split3(x):
    hi = x.astype(BF16)
    r1 = x - hi.astype(F32)
    mid = r1.astype(BF16)
    lo = (r1 - mid.astype(F32)).astype(BF16)
    return hi, mid, lo


def _split2(x):
    hi = x.astype(BF16)
    lo = (x - hi.astype(F32)).astype(BF16)
    return hi, lo


def _dg(a, b, dims):
    return lax.dot_general(a, b, (dims, ((), ())), preferred_element_type=F32)


_NN = ((1,), (0,))
_NT = ((1,), (1,))
_TN = ((0,), (0,))


def _mm(a, b, dims=_NN, passes=3):
    if passes == 1:
        return _dg(a.astype(BF16), b.astype(BF16), dims)
    if passes == 3:
        ah, al = _split2(a)
        bh, bl = _split2(b)
        return _dg(ah, bh, dims) + (_dg(ah, bl, dims) + _dg(al, bh, dims))
    ah, am, al = _split3(a)
    bh, bm, bl = _split3(b)
    return (_dg(ah, bh, dims) + (_dg(ah, bm, dims) + _dg(am, bh, dims))
            + (_dg(am, bm, dims) + _dg(ah, bl, dims) + _dg(al, bh, dims)))


def _pieces(x, passes):
    return (x.astype(BF16),) if passes == 1 else _split2(x)


def _mmp(a, b, dims):
    out = _dg(a[0], b[0], dims)
    if len(a) > 1 and len(b) > 1:
        return out + (_dg(a[0], b[1], dims) + _dg(a[1], b[0], dims))
    if len(a) > 1:
        return out + _dg(a[1], b[0], dims)
    if len(b) > 1:
        return out + _dg(a[0], b[1], dims)
    return out


def _mm_exact_lhs(a_bf16, b, dims=_NN):
    bh, bm, bl = _split3(b)
    return _dg(a_bf16, bh, dims) + (_dg(a_bf16, bm, dims) + _dg(a_bf16, bl, dims))


def _mm_exact_rhs(a, b_bf16, dims=_NN):
    ah, al = _split2(a)
    return _dg(ah, b_bf16, dims) + _dg(al, b_bf16, dims)


def _sigmoid(x):
    return 1.0 / (1.0 + jnp.exp(-x))


def _layer_norm(x):
    mu = jnp.mean(x, axis=-1, keepdims=True)
    xc = x - mu
    var = jnp.mean(xc * xc, axis=-1, keepdims=True)
    return xc * lax.rsqrt(var + LN_EPS)


def _ada_kernel(c_ref, w_ref, b_ref, o_ref):
    c = c_ref[...]
    o_ref[...] = _mm(c * _sigmoid(c), w_ref[...], passes=6) + b_ref[...]


def _ada_mod(c_rows, w_ada, b_ada):
    R, D = c_rows.shape
    N = w_ada.shape[1]
    tn = 1024
    return pl.pallas_call(
        _ada_kernel,
        grid=(N // tn,),
        in_specs=[pl.BlockSpec((R, D), lambda j: (0, 0)),
                  pl.BlockSpec((D, tn), lambda j: (0, j)),
                  pl.BlockSpec((1, tn), lambda j: (0, j))],
        out_specs=pl.BlockSpec((R, tn), lambda j: (0, j)),
        out_shape=jax.ShapeDtypeStruct((R, N), F32),
        compiler_params=_params(("arbitrary",)),
        name="ada_mod",
    )(c_rows, w_ada, b_ada.reshape(1, N))


def _rope(t, cos, sin_lo, sin_hi, reps):
    w = t.shape[-1]
    tile = lambda a: jnp.concatenate([a] * reps, axis=1) if reps > 1 else a
    return (t * tile(cos) + pltpu.roll(t, w - ROPE_PAIRS, 1) * tile(sin_lo)
            + pltpu.roll(t, ROPE_PAIRS, 1) * tile(sin_hi))


def _inproj_kernel(x_ref, ctx_ref, sh_ref, sc_ref, w_ref, bones_ref, qg_ref, kg_ref,
                   cos_ref, slo_ref, shi_ref, q_ref, k_ref, v_ref, rw_ref, *, n_ctx_tiles):
    i = pl.program_id(1)
    x = jnp.where(i < n_ctx_tiles, ctx_ref[0], x_ref[0])
    h = _layer_norm(x) * (1.0 + sc_ref[0]) + sh_ref[0]
    p = _dg(h.astype(BF16), w_ref[...], _NN)
    bones = bones_ref[...]
    cos, slo, shi = cos_ref[...], slo_ref[...], shi_ref[...]

    q = p[:, :D_ATT]
    ss = _mm_exact_rhs(q * q, bones)
    q = q * lax.rsqrt(ss * (1.0 / HEAD_DIM) + RMS_EPS) * qg_ref[...]
    q = _rope(q, cos, slo, shi, D_ATT // 128) * (ATT_SCALE * LOG2_E)
    q_ref[0] = q.astype(BF16)

    k = p[:, D_ATT:D_ATT + D_ATT_KV]
    ss = _mm_exact_rhs(k * k, bones[:D_ATT_KV, :D_ATT_KV])
    k = k * lax.rsqrt(ss * (1.0 / HEAD_DIM) + RMS_EPS) * kg_ref[...]
    k = _rope(k, cos, slo, shi, D_ATT_KV // 128).astype(BF16)
    v = p[:, D_ATT + D_ATT_KV:D_ATT + 2 * D_ATT_KV].astype(BF16)
    for g in range(ATT_KV_HEADS):
        k_ref[0, g] = k[:, g * HEAD_DIM:(g + 1) * HEAD_DIM]
        v_ref[0, g] = v[:, g * HEAD_DIM:(g + 1) * HEAD_DIM]
    rw_ref[0] = p[:, D_ATT + 2 * D_ATT_KV:]


def _inproj(x, ctx, shift, scale, w_in_bf16, bones, qg, kg, cos, slo, shi):
    B, L, D = x.shape
    n_ctx = ctx.shape[1]
    tm = ROW_TILE
    nct = n_ctx // tm
    Tc = n_ctx + L
    d_in = w_in_bf16.shape[1]
    d_rw = d_in - D_ATT - 2 * D_ATT_KV
    mod_row = lambda b, i: (jnp.where(i < nct, B, b), 0, 0)
    const = lambda b, i: (0, 0)
    return pl.pallas_call(
        functools.partial(_inproj_kernel, n_ctx_tiles=nct),
        grid=(B, Tc // tm),
        in_specs=[pl.BlockSpec((1, tm, D), lambda b, i: (b, jnp.maximum(i - nct, 0), 0)),
                  pl.BlockSpec((1, tm, D), lambda b, i: (b, jnp.minimum(i, nct - 1), 0)),
                  pl.BlockSpec((1, 1, D), mod_row),
                  pl.BlockSpec((1, 1, D), mod_row),
                  pl.BlockSpec((D, d_in), const),
                  pl.BlockSpec(bones.shape, const),
                  pl.BlockSpec((1, D_ATT), const),
                  pl.BlockSpec((1, D_ATT_KV), const),
                  pl.BlockSpec((tm, 128), lambda b, i: (i, 0)),
                  pl.BlockSpec((tm, 128), lambda b, i: (i, 0)),
                  pl.BlockSpec((tm, 128), lambda b, i: (i, 0))],
        out_specs=[pl.BlockSpec((1, tm, D_ATT), lambda b, i: (b, jnp.maximum(i - nct, 0), 0)),
                   pl.BlockSpec((1, ATT_KV_HEADS, tm, HEAD_DIM), lambda b, i: (b, 0, i, 0)),
                   pl.BlockSpec((1, ATT_KV_HEADS, tm, HEAD_DIM), lambda b, i: (b, 0, i, 0)),
                   pl.BlockSpec((1, tm, d_rw), lambda b, i: (b, i, 0))],
        out_shape=[jax.ShapeDtypeStruct((B, L, D_ATT), BF16),
                   jax.ShapeDtypeStruct((B, ATT_KV_HEADS, Tc, HEAD_DIM), BF16),
                   jax.ShapeDtypeStruct((B, ATT_KV_HEADS, Tc, HEAD_DIM), BF16),
                   jax.ShapeDtypeStruct((B, Tc, d_rw), F32)],
        compiler_params=_params(("arbitrary", "arbitrary")),
        name="inproj",
    )(x, ctx, shift, scale, w_in_bf16, bones, qg, kg, cos, slo, shi)


def _attn_kernel(q_ref, k_ref, vt_ref, o_ref, s0, s1, p0, p1, m0, m1):
    i = pl.program_id(2)
    nk, tk, cols = s0.shape

    @pl.when(i == 0)
    def _():
        for ref in (s0, s1, m0, m1):
            ref[...] = jnp.zeros_like(ref)
        p0[...] = jnp.ones_like(p0)
        p1[...] = jnp.ones_like(p1)

    def step(s_w, m_w, s_r, m_r, p_w, p_r):
        q = q_ref[0]
        tq = q.shape[0]
        qs = jnp.concatenate([q[:, h * HEAD_DIM:(h + 1) * HEAD_DIM] for h in range(GQA_GROUP)], axis=0)
        m_prev = m_r[0:1, :]

        def body(j, carry):
            m_run, acc = carry
            off = pl.multiple_of(j * tk, tk)
            s_new = _dg(k_ref[0, 0, pl.ds(off, tk), :], qs, _NT)
            s_w[j] = s_new
            m_run = jnp.maximum(m_run, jnp.max(s_new.reshape(tk // 8, 8, cols), axis=0))
            p_w[j] = jnp.exp2(s_r[j] - m_prev).astype(BF16)
            acc = acc + _dg(vt_ref[0, 0, j], p_r[j], _NN)
            return m_run, acc

        init = (jnp.full((8, cols), -jnp.inf, F32), jnp.zeros((vt_ref.shape[3], cols), F32))
        m_run, acc = lax.fori_loop(0, nk, body, init, unroll=8)
        m_w[...] = jnp.broadcast_to(jnp.max(m_run, axis=0, keepdims=True), m_w.shape)
        o = acc[:HEAD_DIM] / acc[HEAD_DIM:HEAD_DIM + 1]
        o_ref[0] = jnp.concatenate([o[:, h * tq:(h + 1) * tq].T for h in range(GQA_GROUP)],
                                   axis=1).astype(BF16)

    @pl.when(i % 2 == 0)
    def _():
        step(s0, m0, s1, m1, p1, p0)

    @pl.when(i % 2 == 1)
    def _():
        step(s1, m1, s0, m0, p0, p1)


def _attention(q, k, vt):
    B, L, _ = q.shape
    Tc = k.shape[2]
    tq = ATT_Q_TILE
    n = L // tq
    gw = GQA_GROUP * HEAD_DIM
    cols = GQA_GROUP * tq
    tk = ATT_K_TILE
    nk = Tc // tk
    return pl.pallas_call(
        _attn_kernel,
        grid=(B, ATT_KV_HEADS, n + 2),
        in_specs=[pl.BlockSpec((1, tq, gw), lambda b, g, i: (b, jnp.minimum(i, n - 1), g)),
                  pl.BlockSpec((1, 1, Tc, HEAD_DIM), lambda b, g, i: (b, g, 0, 0)),
                  pl.BlockSpec((1, 1, nk, vt.shape[3], tk), lambda b, g, i: (b, g, 0, 0, 0))],
        out_specs=pl.BlockSpec((1, tq, gw), lambda b, g, i: (b, jnp.clip(i - 2, 0, n - 1), g)),
        out_shape=jax.ShapeDtypeStruct((B, L, D_ATT), BF16),
        scratch_shapes=[pltpu.VMEM((nk, tk, cols), F32), pltpu.VMEM((nk, tk, cols), F32),
                        pltpu.VMEM((nk, tk, cols), BF16), pltpu.VMEM((nk, tk, cols), BF16),
                        pltpu.VMEM((8, cols), F32), pltpu.VMEM((8, cols), F32)],
        compiler_params=_params(("arbitrary", "arbitrary", "arbitrary")),
        name="attention",
    )(q, k, vt)


def _rwkv_prep_kernel(rw_ref, prev_ref, next_ref, mu_ref, w0_ref, dup_ref, a0_ref, aup_ref, gup_ref,
                      kk_ref, ka_ref, rk_ref, bones_ref,
                      sh_o, g_o, bonus_o, dir_o, *, n_ctx, n_tok):
    i = pl.program_id(1)
    p = rw_ref[0]
    tm = p.shape[0]
    t0 = i * tm
    has_prev = jnp.where((t0 == 0) | (t0 == n_ctx), 0.0, 1.0)
    has_next = jnp.where((t0 + tm == n_ctx) | (t0 + tm == n_tok), 0.0, 1.0)
    row = lax.broadcasted_iota(jnp.int32, (tm, 1), 0)
    up = jnp.where(row == 0, prev_ref[0, 7:8, :] * has_prev, pltpu.roll(p, 1, 0))
    dn = jnp.where(row == tm - 1, next_ref[0, 0:1, :] * has_next, pltpu.roll(p, tm - 1, 0))
    s = p + (0.5 * (up + dn) - p) * mu_ref[...]

    D = D_RWKV
    r, k, v = s[:, :D], s[:, D:2 * D], s[:, 2 * D:3 * D]
    lora = s[:, 3 * D:3 * D + D_LORA]
    gl = s[:, 3 * D + D_LORA:]
    bones = bones_ref[...]

    g_o[0] = _mm(_sigmoid(gl), gup_ref[...], passes=3)
    kk = k * kk_ref[...]
    nrm = jnp.sqrt(_mm_exact_rhs(kk * kk, bones))
    kk = kk / jnp.maximum(nrm, L2_EPS)
    wt = jnp.tanh(lora)
    sh_o[0, :, 0:D] = r
    sh_o[0, :, D:2 * D] = v
    sh_o[0, :, 2 * D:3 * D] = kk
    bonus_o[0] = _mm_exact_rhs(r * k * rk_ref[...], bones) * v
    for d in range(2):
        z = w0_ref[d:d + 1, :] + _mm(wt, dup_ref[d], passes=3)
        dir_o[d, 0, :, 0:D] = -EXP_M05 * _sigmoid(z)
        a = _sigmoid(a0_ref[d:d + 1, :] + _mm(lora, aup_ref[d], passes=3))
        dir_o[d, 0, :, D:2 * D] = k * (1.0 + (a - 1.0) * ka_ref[...])
        dir_o[d, 0, :, 2 * D:3 * D] = kk * a


def _rwkv_prep(rw, n_ctx, mu, w0, dup, a0, aup, gup, k_k, k_a, r_k, bones):
    B, Tc, d_rw = rw.shape
    tm = ROW_TILE
    D = D_RWKV
    const2 = lambda b, i: (0, 0)
    const3 = lambda b, i: (0, 0, 0)
    tok = lambda n: pl.BlockSpec((1, tm, n * D), lambda b, i: (b, i, 0))
    tok2 = pl.BlockSpec((2, 1, tm, 3 * D), lambda b, i: (0, b, i, 0))
    s1 = lambda n: jax.ShapeDtypeStruct((B, Tc, n * D), F32)
    s2 = jax.ShapeDtypeStruct((2, B, Tc, 3 * D), F32)
    hb = tm // 8
    return pl.pallas_call(
        functools.partial(_rwkv_prep_kernel, n_ctx=n_ctx, n_tok=Tc),
        grid=(B, Tc // tm),
        in_specs=[pl.BlockSpec((1, tm, d_rw), lambda b, i: (b, i, 0)),
                  pl.BlockSpec((1, 8, d_rw), lambda b, i: (b, jnp.maximum(i * hb - 1, 0), 0)),
                  pl.BlockSpec((1, 8, d_rw), lambda b, i: (b, jnp.minimum((i + 1) * hb, Tc // 8 - 1), 0)),
                  pl.BlockSpec((1, d_rw), const2),
                  pl.BlockSpec((2, D), const2),
                  pl.BlockSpec((2, D_LORA, D), const3),
                  pl.BlockSpec((2, D), const2),
                  pl.BlockSpec((2, D_LORA, D), const3),
                  pl.BlockSpec((D_LORA, D), const2),
                  pl.BlockSpec((1, D), const2),
                  pl.BlockSpec((1, D), const2),
                  pl.BlockSpec((1, D), const2),
                  pl.BlockSpec(bones.shape, const2)],
        out_specs=[tok(3), tok(1), tok(1), tok2],
        out_shape=[s1(3), s1(1), s1(1), s2],
        compiler_params=_params(("arbitrary", "arbitrary")),
        name="rwkv_prep",
    )(rw, rw, rw, mu, w0, dup, a0, aup, gup, k_k, k_a, r_k, bones)


def _wkv_kernel(sh0, dir0, sh1, dir1, y0, y1, s_ref, *, n_heads):
    @pl.when(pl.program_id(1) == 0)
    def _():
        s_ref[...] = jnp.zeros_like(s_ref)

    C = WKV_CHUNK
    n_sub = sh0.shape[1] // C
    lax.fori_loop(0, n_sub, functools.partial(_wkv_chunk, refs=(sh0, dir0, sh1, dir1, y0, y1, s_ref),
                                              n_heads=n_heads, n_sub=n_sub), 0)


def _wkv_chunk(sub, carry, *, refs, n_heads, n_sub):
    sh0, dir0, sh1, dir1, y0, y1, s_ref = refs
    C = WKV_CHUNK
    N = RWKV_HEAD
    D = n_heads * N
    row = lax.broadcasted_iota(jnp.int32, (C, 2 * N), 0)
    lane = lax.broadcasted_iota(jnp.int32, (C, 2 * N), 1)
    col = lane % N
    first_half = lane < N
    trow = lax.broadcasted_iota(jnp.int32, (C, C), 0)
    tcol = lax.broadcasted_iota(jnp.int32, (C, C), 1)
    cat = lambda xs, ys: tuple(jnp.concatenate([x, y], axis=0) for x, y in zip(xs, ys))
    offs = (pl.multiple_of(sub * C, C), pl.multiple_of((n_sub - 1 - sub) * C, C))
    y_refs = (y0, y1)

    chains = []
    for d, (sh_ref, dir_ref) in enumerate(((sh0, dir0), (sh1, dir1))):
        rows = pl.ds(offs[d], C)
        strict, incl = (row < col, row <= col) if d else (row > col, row >= col)
        tri = (trow <= tcol) if d else (trow >= tcol)
        lw = dir_ref[0, 0, rows, 0:D]
        cl = _mm_exact_lhs(jnp.where(tri, 1.0, 0.0).astype(BF16), lw)
        ecl = jnp.exp(cl)
        eneg = jnp.exp(-cl)
        last = 0 if d else C - 1
        etot = ecl[last:last + 1, :]
        kt = dir_ref[0, 0, rows, D:2 * D] * eneg
        bt = dir_ref[0, 0, rows, 2 * D:3 * D] * eneg
        full = dict(at=_split2(-(sh_ref[0, rows, 2 * D:3 * D] * jnp.exp(cl - lw))),
                    rt=_split2(sh_ref[0, rows, 0:D] * ecl),
                    kt=_split2(kt), bt=_split2(bt), v=_split2(sh_ref[0, rows, D:2 * D]),
                    kh=_split2(kt * etot), bh=_split2(bt * etot))
        for h in range(n_heads):
            ch = {k: tuple(p[:, h * N:(h + 1) * N] for p in val) for k, val in full.items()}
            ch.update(d=d, h=h, strict=strict, incl=incl, etot=etot[:, h * N:(h + 1) * N], s0=s_ref[d, h])
            chains.append(ch)

    for ch in chains:
        rhs = tuple(jnp.concatenate([k, b, s], axis=0)
                    for k, b, s in zip(ch["kt"], ch["bt"], _split2(ch["s0"])))
        g = _mmp(cat(ch["at"], ch["rt"]), rhs, _NT)
        ch["top"] = jnp.where(ch["strict"], g[:C, :2 * N], 0.0)
        ch["bot"] = jnp.where(ch["incl"], g[C:, :2 * N], 0.0)
        ch["as0"], ch["rs0"] = g[:C, 2 * N:], g[C:, 2 * N:]
    for ch in chains:
        z = ch["as0"] + _mmp(_split2(ch["top"][:, :N]), ch["v"], _NN)
        ch["w"] = jnp.where(first_half, jnp.concatenate([z, jnp.zeros_like(z)], axis=1), ch["top"])
        ch["pw"] = pltpu.roll(ch["top"], N, 1)[:, :N]
    n = 1
    while True:
        for ch in chains:
            ch["prod"] = _mmp(_split2(ch["pw"]), _split2(ch["w"]), _NN)
            ch["w"] = jnp.where(first_half, ch["w"], 0.0) + ch["prod"]
        n *= 2
        if n >= C:
            break
        for ch in chains:
            ch["pw"] = pltpu.roll(ch["prod"], N, 1)[:, :N]
    for ch in chains:
        ch["up"] = _split2(ch["w"][:, :N])
        y = ch["rs0"] + _mmp(_split2(ch["bot"]), cat(ch["v"], ch["up"]), _NN)
        y_refs[ch["d"]][0, pl.ds(offs[ch["d"]], C), ch["h"] * N:(ch["h"] + 1) * N] = y
    for ch in chains:
        upd = _mmp(cat(ch["up"], ch["v"]), cat(ch["bh"], ch["kh"]), _TN)
        s_ref[ch["d"], ch["h"]] = ch["s0"] * ch["etot"] + upd
    return carry


def _wkv_scan(shared, per_dir, n_ctx):
    B, T, D3 = shared.shape
    D = D3 // 3
    tb = WKV_BLOCK
    n_blocks = T // tb
    n_cb = n_ctx // tb
    tok = (lambda c: c,
           lambda c: jnp.where(c < n_cb, n_cb - 1 - c, n_blocks - 1 + n_cb - c))
    in_specs, args = [], []
    for d in range(2):
        in_specs += [pl.BlockSpec((1, tb, D3), lambda bi, c, d=d: (bi, tok[d](c), 0)),
                     pl.BlockSpec((1, 1, tb, D3), lambda bi, c, d=d: (d, bi, tok[d](c), 0))]
        args += [shared, per_dir]
    y = jax.ShapeDtypeStruct((B, T, D), F32)
    return pl.pallas_call(
        functools.partial(_wkv_kernel, n_heads=D // RWKV_HEAD),
        grid=(B, n_blocks),
        in_specs=in_specs,
        out_specs=[pl.BlockSpec((1, tb, D), lambda bi, c, d=d: (bi, tok[d](c), 0)) for d in range(2)],
        out_shape=[y, y],
        scratch_shapes=[pltpu.VMEM((2, D // RWKV_HEAD, RWKV_HEAD, RWKV_HEAD), F32)],
        compiler_params=_params(("arbitrary", "arbitrary")),
        name="wkv_scan",
    )(*args)


def _post_kernel(*refs, k):
    rows = lambda j: jnp.concatenate([r[0] for r in refs[j * k:(j + 1) * k]], axis=0)
    (att_ref, x_ref, gt_ref, sh_ref, sc_ref, wo_ref, lxg_ref, lxb_ref, l1g_ref, l1b_ref, rwt_ref, bones_ref,
     x1_ref, h2_ref, aff_ref) = refs[4 * k:]
    bones = bones_ref[...]
    y = rows(0) + rows(1)
    inv = 1.0 / RWKV_HEAD
    mu = _mm_exact_rhs(y, bones) * inv
    yc = y - mu
    var = _mm_exact_rhs(yc * yc, bones) * inv
    yn = yc * lax.rsqrt(var + GN_EPS) * lxg_ref[...] + lxb_ref[...]
    rw_out = (yn + rows(2)) * rows(3)
    mix = _dg(att_ref[0], wo_ref[:D_ATT, :], _NN) + _dg(rw_out.astype(BF16), wo_ref[D_ATT:, :], _NN)
    x1 = _layer_norm(ALPHA * x_ref[0] + gt_ref[0] * mix) * l1g_ref[...] + l1b_ref[...]
    x1_ref[0] = x1
    h2 = _layer_norm(x1) * (1.0 + sc_ref[0]) + sh_ref[0]
    h2_ref[0] = h2.astype(BF16)
    logits = _mm(rwt_ref[...], h2, _NT, passes=3)
    e = jnp.exp(logits - jnp.max(logits, axis=0, keepdims=True))
    aff_ref[0] = e / jnp.sum(e, axis=0, keepdims=True)


def _post(yf, yb, bonus, g, att, x, gt1, sh2, sc2, w_out_bf16, lnx_g, lnx_b, ln1_g, ln1_b, router_wt,
          bones, n_ctx):
    B, L, D = x.shape
    tm = min(POST_TILE, L)
    k = tm // ROW_TILE
    nct = n_ctx // ROW_TILE
    E = router_wt.shape[0]
    const = lambda b, i: (0, 0)
    cat = [pl.BlockSpec((1, ROW_TILE, D_RWKV), lambda b, i, j=j: (b, i * k + j + nct, 0)) for j in range(k)]
    mod = pl.BlockSpec((1, 1, D), lambda b, i: (b, 0, 0))
    vec = lambda n: pl.BlockSpec((1, n), const)
    return pl.pallas_call(
        functools.partial(_post_kernel, k=k),
        grid=(B, L // tm),
        in_specs=cat * 4 + [
                  pl.BlockSpec((1, tm, D_ATT), lambda b, i: (b, i, 0)),
                  pl.BlockSpec((1, tm, D), lambda b, i: (b, i, 0)),
                  mod, mod, mod,
                  pl.BlockSpec(w_out_bf16.shape, const),
                  vec(D_RWKV), vec(D_RWKV), vec(D), vec(D),
                  pl.BlockSpec((E, D), const),
                  pl.BlockSpec(bones.shape, const)],
        out_specs=[pl.BlockSpec((1, tm, D), lambda b, i: (b, i, 0)),
                   pl.BlockSpec((1, tm, D), lambda b, i: (b, i, 0)),
                   pl.BlockSpec((1, E, tm), lambda b, i: (b, 0, i))],
        out_shape=[jax.ShapeDtypeStruct((B, L, D), F32),
                   jax.ShapeDtypeStruct((B, L, D), BF16),
                   jax.ShapeDtypeStruct((B, E, L), F32)],
        compiler_params=_params(("arbitrary", "arbitrary")),
        name="post_mix",
    )(*([yf] * k + [yb] * k + [bonus] * k + [g] * k), att, x, gt1, sh2, sc2, w_out_bf16, lnx_g, lnx_b,
      ln1_g, ln1_b, router_wt, bones)


def _cumsum_lanes(x):
    n = x.shape[1]
    lane = lax.broadcasted_iota(jnp.int32, x.shape, 1)
    s = 1
    while s < n:
        x = x + jnp.where(lane >= s, pltpu.roll(x, s, 1), 0)
        s *= 2
    return x


def _route_kernel(aff_ref, pos_ref, gate_ref, cum_ref, *, cap, tile):
    aff = aff_ref[0]
    E = aff.shape[0]
    count = lambda m: jnp.sum(jnp.where(m, 1, 0), axis=1, keepdims=True)
    thr_bits = jnp.zeros((E, 1), jnp.int32)
    for bit in range(29, -1, -1):
        cand = thr_bits | (1 << bit)
        cand_f = lax.bitcast_convert_type(cand, F32)
        thr_bits = jnp.where(count(aff >= cand_f) >= cap, cand, thr_bits)
    thr = lax.bitcast_convert_type(thr_bits, F32)
    above = aff > thr
    tie = aff == thr
    need = cap - count(above)
    tie_rank = _cumsum_lanes(jnp.where(tie, 1, 0))
    sel = above | (tie & (tie_rank <= need))
    slot = _cumsum_lanes(jnp.where(sel, 1, 0)) - 1
    pos = jnp.where(sel, slot, -1)
    gate = jnp.where(sel, aff, 0.0)
    for e in range(E):
        pos_ref[0, e] = pos[e:e + 1, :]
        gate_ref[0, e] = gate[e:e + 1, :]
    lane = lax.broadcasted_iota(jnp.int32, (E, 128), 1)
    chosen = jnp.where(sel, 1, 0)
    run = jnp.zeros((E, 1), jnp.int32)
    cum = jnp.zeros((E, 128), jnp.int32)
    n_tiles = aff.shape[1] // tile
    for i in range(n_tiles):
        cum = jnp.where(lane == i, run, cum)
        run = run + jnp.sum(chosen[:, i * tile:(i + 1) * tile], axis=1, keepdims=True)
    cum_ref[0] = jnp.where(lane == n_tiles, run, cum)


def _route(aff, cap, tile):
    B, E, L = aff.shape
    out = pl.BlockSpec((1, E, 1, L), lambda b: (b, 0, 0, 0))
    return pl.pallas_call(
        functools.partial(_route_kernel, cap=cap, tile=tile),
        grid=(B,),
        in_specs=[pl.BlockSpec((1, E, L), lambda b: (b, 0, 0))],
        out_specs=[out, out, pl.BlockSpec((1, E, 128), lambda b: (b, 0, 0))],
        out_shape=[jax.ShapeDtypeStruct((B, E, 1, L), jnp.int32),
                   jax.ShapeDtypeStruct((B, E, 1, L), F32),
                   jax.ShapeDtypeStruct((B, E, 128), jnp.int32)],
        compiler_params=_params(("arbitrary",)),
        name="route",
    )(aff)


def _expert_kernel(h_ref, pos_ref, wg_ref, wu_ref, wd_ref, y_ref, *, cap):
    pos = pos_ref[0, 0]
    slot = lax.broadcasted_iota(jnp.int32, (cap, pos.shape[1]), 0)
    onehot = jnp.where(pos == slot, 1.0, 0.0).astype(BF16)
    xin = _dg(onehot, h_ref[0], _NN).astype(BF16)
    hg = _dg(xin, wg_ref[0], _NN)
    hu = _dg(xin, wu_ref[0], _NN)
    hid = (hg * _sigmoid(hg) * hu).astype(BF16)
    y_ref[0, 0] = _dg(hid, wd_ref[0], _NN).astype(BF16)


def _experts(h2, pos, wg, wu, wd, cap):
    B, L, D = h2.shape
    E, _, F = wg.shape
    rowv = pl.BlockSpec((1, 1, 1, L), lambda b, e: (b, e, 0, 0))
    return pl.pallas_call(
        functools.partial(_expert_kernel, cap=cap),
        grid=(B, E),
        in_specs=[pl.BlockSpec((1, L, D), lambda b, e: (b, 0, 0)),
                  rowv,
                  pl.BlockSpec((1, D, F), lambda b, e: (e, 0, 0)),
                  pl.BlockSpec((1, D, F), lambda b, e: (e, 0, 0)),
                  pl.BlockSpec((1, F, D), lambda b, e: (e, 0, 0))],
        out_specs=pl.BlockSpec((1, 1, cap, D), lambda b, e: (b, e, 0, 0)),
        out_shape=jax.ShapeDtypeStruct((B, E, cap, D), BF16),
        compiler_params=_params(("arbitrary", "arbitrary")),
        name="experts",
    )(h2, pos, wg, wu, wd)


def _combine_kernel(cum_ref, pos_ref, gate_ref, y_ref, x1_ref, gt_ref, lg_ref, lb_ref, o_ref, acc_ref, *, cap):
    b, i, e = pl.program_id(0), pl.program_id(1), pl.program_id(2)

    @pl.when(e == 0)
    def _():
        acc_ref[...] = jnp.zeros_like(acc_ref)

    base = (b * pl.num_programs(2) + e) * 128
    lo, hi = cum_ref[base + i], cum_ref[base + i + 1]
    window = min(COMBINE_WINDOW, cap)
    align = min(128, window)
    start = pl.multiple_of(jnp.minimum((lo // align) * align, cap - window), align)

    def accumulate(first, width):
        pos = pos_ref[0, 0]
        slot = first + lax.broadcasted_iota(jnp.int32, (width, pos.shape[1]), 0)
        weights = jnp.where(pos == slot, gate_ref[0, 0], 0.0).astype(BF16)
        acc_ref[...] += _dg(weights, y_ref[0, 0, pl.ds(first, width), :], _TN)

    @pl.when((hi > lo) & (hi - start <= window))
    def _():
        accumulate(start, window)

    @pl.when(hi - start > window)
    def _():
        accumulate(0, cap)

    @pl.when(e == pl.num_programs(2) - 1)
    def _():
        z = ALPHA * x1_ref[0] + gt_ref[0] * acc_ref[...]
        o_ref[0] = _layer_norm(z) * lg_ref[...] + lb_ref[...]


def _combine(cum, pos, gate, y, x1, gt2, ln2_g, ln2_b, cap):
    B, L, D = x1.shape
    E = y.shape[1]
    tl = min(COMBINE_TILE, L)
    const = lambda b, i, e, cum: (0, 0)
    grid_spec = pltpu.PrefetchScalarGridSpec(
        num_scalar_prefetch=1,
        grid=(B, L // tl, E),
        in_specs=[pl.BlockSpec((1, 1, 1, tl), lambda b, i, e, cum: (b, e, 0, i)),
                  pl.BlockSpec((1, 1, 1, tl), lambda b, i, e, cum: (b, e, 0, i)),
                  pl.BlockSpec((1, 1, cap, D), lambda b, i, e, cum: (b, e, 0, 0)),
                  pl.BlockSpec((1, tl, D), lambda b, i, e, cum: (b, i, 0)),
                  pl.BlockSpec((1, 1, D), lambda b, i, e, cum: (b, 0, 0)),
                  pl.BlockSpec((1, D), const),
                  pl.BlockSpec((1, D), const)],
        out_specs=pl.BlockSpec((1, tl, D), lambda b, i, e, cum: (b, i, 0)),
        scratch_shapes=[pltpu.VMEM((tl, D), F32)])
    return pl.pallas_call(
        functools.partial(_combine_kernel, cap=cap),
        grid_spec=grid_spec,
        out_shape=jax.ShapeDtypeStruct((B, L, D), F32),
        compiler_params=_params(("arbitrary", "arbitrary", "arbitrary")),
        name="combine",
    )(cum, pos, gate, y, x1, gt2, ln2_g, ln2_b)


def _rope_tables(n_ctx, n_lat):
    lane = jnp.arange(128)
    j = lane % HEAD_DIM
    axis = j // (2 * ROPE_PAIRS)
    upper = (j % (2 * ROPE_PAIRS)) >= ROPE_PAIRS
    inv = ROPE_THETA ** (-(j % ROPE_PAIRS).astype(F32) / ROPE_PAIRS)
    t = jnp.arange(n_lat)
    coord = jnp.where(axis[None, :] == 0, (t // GRID_W)[:, None], (t % GRID_W)[:, None]).astype(F32)
    ang = coord * inv[None, :]
    cos, sin = jnp.cos(ang), jnp.sin(ang)
    s_lo = jnp.where(upper[None, :], 0.0, -sin)
    s_hi = jnp.where(upper[None, :], sin, 0.0)
    pad = lambda a, fill: jnp.concatenate([jnp.full((n_ctx, 128), fill, F32), a], axis=0)
    return pad(cos, 1.0), pad(s_lo, 0.0), pad(s_hi, 0.0)


def _head_ones(n):
    h = jnp.arange(n) // HEAD_DIM
    return (h[:, None] == h[None, :]).astype(BF16)


def kernel(x, c, ctx, c_ctx, w_ada, b_ada, w_in, q_gain, k_gain, tshift_mu, decay_w0, decay_up, iclr_a0,
           iclr_up, gate_up, k_k, k_a, r_k, lnx_g, lnx_b, w_out, ln1_g, ln1_b, router_w, exp_w_gate,
           exp_w_up, exp_w_down, ln2_g, ln2_b):
    B, L, D = x.shape
    n_ctx = ctx.shape[1]
    li = 0
    row = lambda a: a.reshape(1, -1)

    n_rows = -(-(B + 1) // 8) * 8
    c_rows = jnp.concatenate([c, c_ctx[None, :], jnp.zeros((n_rows - B - 1, D), F32)], axis=0)
    mod = _ada_mod(c_rows, w_ada[li], b_ada[li]).reshape(n_rows, N_MOD, 1, D)
    sh1, sc1, gt1, sh2, sc2, gt2 = (mod[:, m] for m in range(N_MOD))

    bones = _head_ones(D_RWKV)
    cos, s_lo, s_hi = _rope_tables(n_ctx, L)
    q, k, v, rw = _inproj(x, ctx, sh1, sc1, w_in[li].astype(BF16), bones,
                          row(jnp.tile(q_gain[li], ATT_HEADS)), row(jnp.tile(k_gain[li], ATT_KV_HEADS)),
                          cos, s_lo, s_hi)
    Tc = n_ctx + L
    vt = v.reshape(B, ATT_KV_HEADS, Tc // ATT_K_TILE, ATT_K_TILE, HEAD_DIM).swapaxes(-1, -2)
    vt = jnp.concatenate([vt, jnp.ones(vt.shape[:3] + (16, ATT_K_TILE), BF16)], axis=3)
    att = _attention(q, k, vt)

    zpad = jnp.zeros((2, D_LORA // 2, D_RWKV), F32)
    dup = jnp.concatenate([decay_up[li], zpad], axis=1)
    aup = jnp.concatenate([zpad, iclr_up[li]], axis=1)
    scan_shared, g, bonus, scan_dir = _rwkv_prep(
        rw, n_ctx, row(tshift_mu[li]), decay_w0[li], dup, iclr_a0[li], aup, gate_up[li],
        row(k_k[li]), row(k_a[li]), row(r_k[li]), bones)
    yf, yb = _wkv_scan(scan_shared, scan_dir, n_ctx)

    x1, h2, aff = _post(yf, yb, bonus, g, att, x, gt1, sh2, sc2, w_out[li].astype(BF16),
                        row(lnx_g[li]), row(lnx_b[li]), row(ln1_g[li]), row(ln1_b[li]),
                        router_w[li].T, bones, n_ctx)
    cap = CAPACITY_FACTOR * L // N_EXPERTS
    pos, gate, cum = _route(aff, cap, min(COMBINE_TILE, L))
    y = _experts(h2, pos, exp_w_gate[li].astype(BF16), exp_w_up[li].astype(BF16),
                 exp_w_down[li].astype(BF16), cap)
    return _combine(cum.reshape(-1), pos, gate, y, x1, gt2, row(ln2_g[li]), row(ln2_b[li]), cap)
```

```python
import functools

import jax
import jax.numpy as jnp
from jax import lax
from jax.experimental import pallas as pl
from jax.experimental.pallas import tpu as pltpu

F32 = jnp.float32
BF16 = jnp.bfloat16

HEAD_DIM = 64
ATT_HEADS = 8
ATT_KV_HEADS = 2
GQA_GROUP = ATT_HEADS // ATT_KV_HEADS
D_ATT = ATT_HEADS * HEAD_DIM
D_ATT_KV = ATT_KV_HEADS * HEAD_DIM
ATT_SCALE = HEAD_DIM ** -0.5
LOG2_E = 1.4426950408889634
EXP_M05 = 0.6065306597126334
ROPE_THETA = 10000.0
ROPE_PAIRS = HEAD_DIM // 4
GRID_W = 64
RWKV_HEAD = 64
D_RWKV = 512
D_LORA = 128
N_EXPERTS = 16
CAPACITY_FACTOR = 2
N_MOD = 6
LN_EPS = 1e-5
RMS_EPS = 1e-6
GN_EPS = 64e-5
L2_EPS = 1e-12
DEPTH = 1
ALPHA = (2.0 * DEPTH) ** 0.25

WKV_CHUNK = 64
WKV_BLOCK = 256
ROW_TILE = 256
POST_TILE = 512
ATT_Q_TILE = 128
ATT_K_TILE = 256
EXPERT_F_SPLITS = 2
COMBINE_TILE = 1024
COMBINE_WINDOW = 256
VMEM_LIMIT = 48 * 1024 * 1024


def _params(sem):
    return pltpu.CompilerParams(dimension_semantics=sem, vmem_limit_bytes=VMEM_LIMIT)


def _split3(x):
    hi = x.astype(BF16)
    r1 = x - hi.astype(F32)
    mid = r1.astype(BF16)
    lo = (r1 - mid.astype(F32)).astype(BF16)
    return hi, mid, lo


def _split2(x):
    hi = x.astype(BF16)
    lo = (x - hi.astype(F32)).astype(BF16)
    return hi, lo


def _hi(x):
    return (x.astype(BF16),)


def _dg(a, b, dims):
    return lax.dot_general(a, b, (dims, ((), ())), preferred_element_type=F32)


_NN = ((1,), (0,))
_NT = ((1,), (1,))
_TN = ((0,), (0,))


def _mm(a, b, dims=_NN, passes=3):
    if passes == 1:
        return _dg(a.astype(BF16), b.astype(BF16), dims)
    if passes == 3:
        ah, al = _split2(a)
        bh, bl = _split2(b)
        return _dg(ah, bh, dims) + (_dg(ah, bl, dims) + _dg(al, bh, dims))
    ah, am, al = _split3(a)
    bh, bm, bl = _split3(b)
    return (_dg(ah, bh, dims) + (_dg(ah, bm, dims) + _dg(am, bh, dims))
            + (_dg(am, bm, dims) + _dg(ah, bl, dims) + _dg(al, bh, dims)))


def _pieces(x, passes):
    return (x.astype(BF16),) if passes == 1 else _split2(x)


def _mmp(a, b, dims):
    out = _dg(a[0], b[0], dims)
    if len(a) > 1 and len(b) > 1:
        return out + (_dg(a[0], b[1], dims) + _dg(a[1], b[0], dims))
    if len(a) > 1:
        return out + _dg(a[1], b[0], dims)
    if len(b) > 1:
        return out + _dg(a[0], b[1], dims)
    return out


def _mm_exact_lhs(a_bf16, b, dims=_NN):
    bh, bm, bl = _split3(b)
    return _dg(a_bf16, bh, dims) + (_dg(a_bf16, bm, dims) + _dg(a_bf16, bl, dims))


def _mm_exact_rhs(a, b_bf16, dims=_NN):
    ah, al = _split2(a)
    return _dg(ah, b_bf16, dims) + _dg(al, b_bf16, dims)


def _sigmoid(x):
    return 1.0 / (1.0 + jnp.exp(-x))


def _layer_norm(x):
    mu = jnp.mean(x, axis=-1, keepdims=True)
    xc = x - mu
    var = jnp.mean(xc * xc, axis=-1, keepdims=True)
    return xc * lax.rsqrt(var + LN_EPS)


def _ada_kernel(c_ref, w_ref, b_ref, o_ref):
    c = c_ref[...]
    o_ref[...] = _mm(c * _sigmoid(c), w_ref[...], passes=6) + b_ref[...]


def _ada_mod(c_rows, w_ada, b_ada):
    R, D = c_rows.shape
    N = w_ada.shape[1]
    tn = 1024
    return pl.pallas_call(
        _ada_kernel,
        grid=(N // tn,),
        in_specs=[pl.BlockSpec((R, D), lambda j: (0, 0)),
                  pl.BlockSpec((D, tn), lambda j: (0, j)),
                  pl.BlockSpec((1, tn), lambda j: (0, j))],
        out_specs=pl.BlockSpec((R, tn), lambda j: (0, j)),
        out_shape=jax.ShapeDtypeStruct((R, N), F32),
        compiler_params=_params(("arbitrary",)),
        name="ada_mod",
    )(c_rows, w_ada, b_ada.reshape(1, N))


def _rope(t, cos, sin_lo, sin_hi, reps):
    w = t.shape[-1]
    tile = lambda a: jnp.concatenate([a] * reps, axis=1) if reps > 1 else a
    return (t * tile(cos) + pltpu.roll(t, w - ROPE_PAIRS, 1) * tile(sin_lo)
            + pltpu.roll(t, ROPE_PAIRS, 1) * tile(sin_hi))


def _inproj_kernel(x_ref, ctx_ref, sh_ref, sc_ref, w_ref, bones_ref, qg_ref, kg_ref,
                   cos_ref, slo_ref, shi_ref, q_ref, k_ref, v_ref, rw_ref, *, n_ctx_tiles):
    i = pl.program_id(1)
    x = jnp.where(i < n_ctx_tiles, ctx_ref[0], x_ref[0])
    h = _layer_norm(x) * (1.0 + sc_ref[0]) + sh_ref[0]
    p = _dg(h.astype(BF16), w_ref[...], _NN)
    bones = bones_ref[...]
    cos, slo, shi = cos_ref[...], slo_ref[...], shi_ref[...]

    q = p[:, :D_ATT]
    ss = _mm_exact_rhs(q * q, bones)
    q = q * lax.rsqrt(ss * (1.0 / HEAD_DIM) + RMS_EPS) * qg_ref[...]
    q = _rope(q, cos, slo, shi, D_ATT // 128) * (ATT_SCALE * LOG2_E)
    q_ref[0] = q.astype(BF16)

    k = p[:, D_ATT:D_ATT + D_ATT_KV]
    ss = _mm_exact_rhs(k * k, bones[:D_ATT_KV, :D_ATT_KV])
    k = k * lax.rsqrt(ss * (1.0 / HEAD_DIM) + RMS_EPS) * kg_ref[...]
    k = _rope(k, cos, slo, shi, D_ATT_KV // 128).astype(BF16)
    v = p[:, D_ATT + D_ATT_KV:D_ATT + 2 * D_ATT_KV].astype(BF16)
    for g in range(ATT_KV_HEADS):
        k_ref[0, g] = k[:, g * HEAD_DIM:(g + 1) * HEAD_DIM]
        v_ref[0, g] = v[:, g * HEAD_DIM:(g + 1) * HEAD_DIM]
    rw_ref[0] = p[:, D_ATT + 2 * D_ATT_KV:]


def _inproj(x, ctx, shift, scale, w_in_bf16, bones, qg, kg, cos, slo, shi):
    B, L, D = x.shape
    n_ctx = ctx.shape[1]
    tm = ROW_TILE
    nct = n_ctx // tm
    Tc = n_ctx + L
    d_in = w_in_bf16.shape[1]
    d_rw = d_in - D_ATT - 2 * D_ATT_KV
    mod_row = lambda b, i: (jnp.where(i < nct, B, b), 0, 0)
    const = lambda b, i: (0, 0)
    return pl.pallas_call(
        functools.partial(_inproj_kernel, n_ctx_tiles=nct),
        grid=(B, Tc // tm),
        in_specs=[pl.BlockSpec((1, tm, D), lambda b, i: (b, jnp.maximum(i - nct, 0), 0)),
                  pl.BlockSpec((1, tm, D), lambda b, i: (b, jnp.minimum(i, nct - 1), 0)),
                  pl.BlockSpec((1, 1, D), mod_row),
                  pl.BlockSpec((1, 1, D), mod_row),
                  pl.BlockSpec((D, d_in), const),
                  pl.BlockSpec(bones.shape, const),
                  pl.BlockSpec((1, D_ATT), const),
                  pl.BlockSpec((1, D_ATT_KV), const),
                  pl.BlockSpec((tm, 128), lambda b, i: (i, 0)),
                  pl.BlockSpec((tm, 128), lambda b, i: (i, 0)),
                  pl.BlockSpec((tm, 128), lambda b, i: (i, 0))],
        out_specs=[pl.BlockSpec((1, tm, D_ATT), lambda b, i: (b, jnp.maximum(i - nct, 0), 0)),
                   pl.BlockSpec((1, ATT_KV_HEADS, tm, HEAD_DIM), lambda b, i: (b, 0, i, 0)),
                   pl.BlockSpec((1, ATT_KV_HEADS, tm, HEAD_DIM), lambda b, i: (b, 0, i, 0)),
                   pl.BlockSpec((1, tm, d_rw), lambda b, i: (b, i, 0))],
        out_shape=[jax.ShapeDtypeStruct((B, L, D_ATT), BF16),
                   jax.ShapeDtypeStruct((B, ATT_KV_HEADS, Tc, HEAD_DIM), BF16),
                   jax.ShapeDtypeStruct((B, ATT_KV_HEADS, Tc, HEAD_DIM), BF16),
                   jax.ShapeDtypeStruct((B, Tc, d_rw), F32)],
        compiler_params=_params(("arbitrary", "arbitrary")),
        name="inproj",
    )(x, ctx, shift, scale, w_in_bf16, bones, qg, kg, cos, slo, shi)


def _attn_kernel(q_ref, k_ref, vt_ref, o_ref, s0, s1, p0, p1, m0, m1):
    i = pl.program_id(2)
    nk, tk, cols = s0.shape

    @pl.when(i == 0)
    def _():
        for ref in (s0, s1, m0, m1):
            ref[...] = jnp.zeros_like(ref)
        p0[...] = jnp.ones_like(p0)
        p1[...] = jnp.ones_like(p1)

    def step(s_w, m_w, s_r, m_r, p_w, p_r):
        q = q_ref[0]
        tq = q.shape[0]
        qs = jnp.concatenate([q[:, h * HEAD_DIM:(h + 1) * HEAD_DIM] for h in range(GQA_GROUP)], axis=0)
        m_prev = m_r[0:1, :]

        def body(j, carry):
            m_run, acc = carry
            off = pl.multiple_of(j * tk, tk)
            s_new = _dg(k_ref[0, 0, pl.ds(off, tk), :], qs, _NT)
            s_w[j] = s_new
            m_run = jnp.maximum(m_run, jnp.max(s_new.reshape(tk // 8, 8, cols), axis=0))
            p_w[j] = jnp.exp2(s_r[j] - m_prev).astype(BF16)
            acc = acc + _dg(vt_ref[0, 0, j], p_r[j], _NN)
            return m_run, acc

        init = (jnp.full((8, cols), -jnp.inf, F32), jnp.zeros((vt_ref.shape[3], cols), F32))
        m_run, acc = lax.fori_loop(0, nk, body, init, unroll=8)
        m_w[...] = jnp.broadcast_to(jnp.max(m_run, axis=0, keepdims=True), m_w.shape)
        o = acc[:HEAD_DIM] / acc[HEAD_DIM:HEAD_DIM + 1]
        o_ref[0] = jnp.concatenate([o[:, h * tq:(h + 1) * tq].T for h in range(GQA_GROUP)],
                                   axis=1).astype(BF16)

    @pl.when(i % 2 == 0)
    def _():
        step(s0, m0, s1, m1, p1, p0)

    @pl.when(i % 2 == 1)
    def _():
        step(s1, m1, s0, m0, p0, p1)


def _attention(q, k, vt):
    B, L, _ = q.shape
    Tc = k.shape[2]
    tq = ATT_Q_TILE
    n = L // tq
    gw = GQA_GROUP * HEAD_DIM
    cols = GQA_GROUP * tq
    tk = ATT_K_TILE
    nk = Tc // tk
    return pl.pallas_call(
        _attn_kernel,
        grid=(B, ATT_KV_HEADS, n + 2),
        in_specs=[pl.BlockSpec((1, tq, gw), lambda b, g, i: (b, jnp.minimum(i, n - 1), g)),
                  pl.BlockSpec((1, 1, Tc, HEAD_DIM), lambda b, g, i: (b, g, 0, 0)),
                  pl.BlockSpec((1, 1, nk, vt.shape[3], tk), lambda b, g, i: (b, g, 0, 0, 0))],
        out_specs=pl.BlockSpec((1, tq, gw), lambda b, g, i: (b, jnp.clip(i - 2, 0, n - 1), g)),
        out_shape=jax.ShapeDtypeStruct((B, L, D_ATT), BF16),
        scratch_shapes=[pltpu.VMEM((nk, tk, cols), F32), pltpu.VMEM((nk, tk, cols), F32),
                        pltpu.VMEM((nk, tk, cols), BF16), pltpu.VMEM((nk, tk, cols), BF16),
                        pltpu.VMEM((8, cols), F32), pltpu.VMEM((8, cols), F32)],
        compiler_params=_params(("arbitrary", "arbitrary", "arbitrary")),
        name="attention",
    )(q, k, vt)


def _rwkv_prep_kernel(rw_ref, prev_ref, next_ref, mu_ref, w0_ref, dup_ref, a0_ref, aup_ref, gup_ref,
                      kk_ref, ka_ref, rk_ref, bones_ref,
                      sh_o, g_o, bonus_o, dir_o, *, n_ctx, n_tok):
    i = pl.program_id(1)
    p = rw_ref[0]
    tm = p.shape[0]
    t0 = i * tm
    has_prev = jnp.where((t0 == 0) | (t0 == n_ctx), 0.0, 1.0)
    has_next = jnp.where((t0 + tm == n_ctx) | (t0 + tm == n_tok), 0.0, 1.0)
    row = lax.broadcasted_iota(jnp.int32, (tm, 1), 0)
    up = jnp.where(row == 0, prev_ref[0, 7:8, :] * has_prev, pltpu.roll(p, 1, 0))
    dn = jnp.where(row == tm - 1, next_ref[0, 0:1, :] * has_next, pltpu.roll(p, tm - 1, 0))
    s = p + (0.5 * (up + dn) - p) * mu_ref[...]

    D = D_RWKV
    r, k, v = s[:, :D], s[:, D:2 * D], s[:, 2 * D:3 * D]
    lora = s[:, 3 * D:3 * D + D_LORA]
    gl = s[:, 3 * D + D_LORA:]
    bones = bones_ref[...]

    g_o[0] = _mm(_sigmoid(gl), gup_ref[...], passes=3)
    kk = k * kk_ref[...]
    nrm = jnp.sqrt(_mm_exact_rhs(kk * kk, bones))
    kk = kk / jnp.maximum(nrm, L2_EPS)
    wt = jnp.tanh(lora)
    sh_o[0, :, 0:D] = r
    sh_o[0, :, D:2 * D] = v
    sh_o[0, :, 2 * D:3 * D] = kk
    bonus_o[0] = _mm_exact_rhs(r * k * rk_ref[...], bones) * v
    for d in range(2):
        z = w0_ref[d:d + 1, :] + _mm(wt, dup_ref[d], passes=3)
        dir_o[d, 0, :, 0:D] = -EXP_M05 * _sigmoid(z)
        a = _sigmoid(a0_ref[d:d + 1, :] + _mm(lora, aup_ref[d], passes=3))
        dir_o[d, 0, :, D:2 * D] = k * (1.0 + (a - 1.0) * ka_ref[...])
        dir_o[d, 0, :, 2 * D:3 * D] = kk * a


def _rwkv_prep(rw, n_ctx, mu, w0, dup, a0, aup, gup, k_k, k_a, r_k, bones):
    B, Tc, d_rw = rw.shape
    tm = ROW_TILE
    D = D_RWKV
    const2 = lambda b, i: (0, 0)
    const3 = lambda b, i: (0, 0, 0)
    tok = lambda n: pl.BlockSpec((1, tm, n * D), lambda b, i: (b, i, 0))
    tok2 = pl.BlockSpec((2, 1, tm, 3 * D), lambda b, i: (0, b, i, 0))
    s1 = lambda n: jax.ShapeDtypeStruct((B, Tc, n * D), F32)
    s2 = jax.ShapeDtypeStruct((2, B, Tc, 3 * D), F32)
    hb = tm // 8
    return pl.pallas_call(
        functools.partial(_rwkv_prep_kernel, n_ctx=n_ctx, n_tok=Tc),
        grid=(B, Tc // tm),
        in_specs=[pl.BlockSpec((1, tm, d_rw), lambda b, i: (b, i, 0)),
                  pl.BlockSpec((1, 8, d_rw), lambda b, i: (b, jnp.maximum(i * hb - 1, 0), 0)),
                  pl.BlockSpec((1, 8, d_rw), lambda b, i: (b, jnp.minimum((i + 1) * hb, Tc // 8 - 1), 0)),
                  pl.BlockSpec((1, d_rw), const2),
                  pl.BlockSpec((2, D), const2),
                  pl.BlockSpec((2, D_LORA, D), const3),
                  pl.BlockSpec((2, D), const2),
                  pl.BlockSpec((2, D_LORA, D), const3),
                  pl.BlockSpec((D_LORA, D), const2),
                  pl.BlockSpec((1, D), const2),
                  pl.BlockSpec((1, D), const2),
                  pl.BlockSpec((1, D), const2),
                  pl.BlockSpec(bones.shape, const2)],
        out_specs=[tok(3), tok(1), tok(1), tok2],
        out_shape=[s1(3), s1(1), s1(1), s2],
        compiler_params=_params(("arbitrary", "arbitrary")),
        name="rwkv_prep",
    )(rw, rw, rw, mu, w0, dup, a0, aup, gup, k_k, k_a, r_k, bones)


def _wkv_kernel(sh0, dir0, sh1, dir1, y0, y1, s_ref, *, n_heads):
    @pl.when(pl.program_id(1) == 0)
    def _():
        s_ref[...] = jnp.zeros_like(s_ref)

    C = WKV_CHUNK
    n_sub = sh0.shape[1] // C
    lax.fori_loop(0, n_sub, functools.partial(_wkv_chunk, refs=(sh0, dir0, sh1, dir1, y0, y1, s_ref),
                                              n_heads=n_heads, n_sub=n_sub), 0)


def _wkv_chunk(sub, carry, *, refs, n_heads, n_sub):
    sh0, dir0, sh1, dir1, y0, y1, s_ref = refs
    C = WKV_CHUNK
    N = RWKV_HEAD
    D = n_heads * N
    row = lax.broadcasted_iota(jnp.int32, (C, 2 * N), 0)
    lane = lax.broadcasted_iota(jnp.int32, (C, 2 * N), 1)
    col = lane % N
    first_half = lane < N
    trow = lax.broadcasted_iota(jnp.int32, (C, C), 0)
    tcol = lax.broadcasted_iota(jnp.int32, (C, C), 1)
    cat = lambda xs, ys: tuple(jnp.concatenate([x, y], axis=0) for x, y in zip(xs, ys))
    offs = (pl.multiple_of(sub * C, C), pl.multiple_of((n_sub - 1 - sub) * C, C))
    y_refs = (y0, y1)

    chains = []
    for d, (sh_ref, dir_ref) in enumerate(((sh0, dir0), (sh1, dir1))):
        rows = pl.ds(offs[d], C)
        strict, incl = (row < col, row <= col) if d else (row > col, row >= col)
        tri = (trow <= tcol) if d else (trow >= tcol)
        lw = dir_ref[0, 0, rows, 0:D]
        cl = _mm_exact_lhs(jnp.where(tri, 1.0, 0.0).astype(BF16), lw)
        ecl = jnp.exp(cl)
        eneg = jnp.exp(-cl)
        last = 0 if d else C - 1
        etot = ecl[last:last + 1, :]
        kt = dir_ref[0, 0, rows, D:2 * D] * eneg
        bt = dir_ref[0, 0, rows, 2 * D:3 * D] * eneg
        full = dict(at=_split2(-(sh_ref[0, rows, 2 * D:3 * D] * jnp.exp(cl - lw))),
                    rt=_split2(sh_ref[0, rows, 0:D] * ecl),
                    v=_split2(sh_ref[0, rows, D:2 * D]),
                    kt=_hi(kt), bt=_hi(bt), kh=_hi(kt * etot), bh=_hi(bt * etot))
        for h in range(n_heads):
            ch = {k: tuple(p[:, h * N:(h + 1) * N] for p in val) for k, val in full.items()}
            ch.update(d=d, h=h, strict=strict, incl=incl, etot=etot[:, h * N:(h + 1) * N], s0=s_ref[d, h])
            chains.append(ch)

    for ch in chains:
        rhs = cat(cat(ch["kt"], ch["bt"]), _hi(ch["s0"]))
        g = _mmp(cat(ch["at"], ch["rt"]), rhs, _NT)
        ch["top"] = jnp.where(ch["strict"], g[:C, :2 * N], 0.0)
        ch["bot"] = jnp.where(ch["incl"], g[C:, :2 * N], 0.0)
        ch["as0"], ch["rs0"] = g[:C, 2 * N:], g[C:, 2 * N:]
    for ch in chains:
        z = ch["as0"] + _mmp(_split2(ch["top"][:, :N]), ch["v"][:1], _NN)
        ch["w"] = jnp.where(first_half, jnp.concatenate([z, jnp.zeros_like(z)], axis=1), ch["top"])
        ch["pw"] = pltpu.roll(ch["top"], N, 1)[:, :N]
    n = 1
    while True:
        for ch in chains:
            ch["prod"] = _mmp(_split2(ch["pw"]), _split2(ch["w"]), _NN)
            ch["w"] = jnp.where(first_half, ch["w"], 0.0) + ch["prod"]
        n *= 2
        if n >= C:
            break
        for ch in chains:
            ch["pw"] = pltpu.roll(ch["prod"], N, 1)[:, :N]
    for ch in chains:
        ch["up"] = _split2(ch["w"][:, :N])
        y = ch["rs0"] + _mmp(_split2(ch["bot"]), cat(ch["v"][:1], ch["up"][:1]), _NN)
        y_refs[ch["d"]][0, pl.ds(offs[ch["d"]], C), ch["h"] * N:(ch["h"] + 1) * N] = y
    for ch in chains:
        upd = _mmp(cat(ch["up"], ch["v"]), cat(ch["bh"], ch["kh"]), _TN)
        s_ref[ch["d"], ch["h"]] = ch["s0"] * ch["etot"] + upd
    return carry


def _wkv_scan(shared, per_dir, n_ctx):
    B, T, D3 = shared.shape
    D = D3 // 3
    tb = WKV_BLOCK
    n_blocks = T // tb
    n_cb = n_ctx // tb
    tok = (lambda c: c,
           lambda c: jnp.where(c < n_cb, n_cb - 1 - c, n_blocks - 1 + n_cb - c))
    in_specs, args = [], []
    for d in range(2):
        in_specs += [pl.BlockSpec((1, tb, D3), lambda bi, c, d=d: (bi, tok[d](c), 0)),
                     pl.BlockSpec((1, 1, tb, D3), lambda bi, c, d=d: (d, bi, tok[d](c), 0))]
        args += [shared, per_dir]
    y = jax.ShapeDtypeStruct((B, T, D), F32)
    return pl.pallas_call(
        functools.partial(_wkv_kernel, n_heads=D // RWKV_HEAD),
        grid=(B, n_blocks),
        in_specs=in_specs,
        out_specs=[pl.BlockSpec((1, tb, D), lambda bi, c, d=d: (bi, tok[d](c), 0)) for d in range(2)],
        out_shape=[y, y],
        scratch_shapes=[pltpu.VMEM((2, D // RWKV_HEAD, RWKV_HEAD, RWKV_HEAD), F32)],
        compiler_params=_params(("arbitrary", "arbitrary")),
        name="wkv_scan",
    )(*args)


def _post_kernel(*refs, k):
    rows = lambda j: jnp.concatenate([r[0] for r in refs[j * k:(j + 1) * k]], axis=0)
    (att_ref, x_ref, gt_ref, sh_ref, sc_ref, wo_ref, lxg_ref, lxb_ref, l1g_ref, l1b_ref, rwt_ref, bones_ref,
     x1_ref, h2_ref, aff_ref) = refs[4 * k:]
    bones = bones_ref[...]
    y = rows(0) + rows(1)
    inv = 1.0 / RWKV_HEAD
    mu = _mm_exact_rhs(y, bones) * inv
    yc = y - mu
    var = _mm_exact_rhs(yc * yc, bones) * inv
    yn = yc * lax.rsqrt(var + GN_EPS) * lxg_ref[...] + lxb_ref[...]
    rw_out = (yn + rows(2)) * rows(3)
    mix = _dg(att_ref[0], wo_ref[:D_ATT, :], _NN) + _dg(rw_out.astype(BF16), wo_ref[D_ATT:, :], _NN)
    x1 = _layer_norm(ALPHA * x_ref[0] + gt_ref[0] * mix) * l1g_ref[...] + l1b_ref[...]
    x1_ref[0] = x1
    h2 = _layer_norm(x1) * (1.0 + sc_ref[0]) + sh_ref[0]
    h2_ref[0] = h2.astype(BF16)
    logits = _mm(rwt_ref[...], h2, _NT, passes=3)
    e = jnp.exp(logits - jnp.max(logits, axis=0, keepdims=True))
    aff_ref[0] = e / jnp.sum(e, axis=0, keepdims=True)


def _post(yf, yb, bonus, g, att, x, gt1, sh2, sc2, w_out_bf16, lnx_g, lnx_b, ln1_g, ln1_b, router_wt,
          bones, n_ctx):
    B, L, D = x.shape
    tm = min(POST_TILE, L)
    k = tm // ROW_TILE
    nct = n_ctx // ROW_TILE
    E = router_wt.shape[0]
    const = lambda b, i: (0, 0)
    cat = [pl.BlockSpec((1, ROW_TILE, D_RWKV), lambda b, i, j=j: (b, i * k + j + nct, 0)) for j in range(k)]
    mod = pl.BlockSpec((1, 1, D), lambda b, i: (b, 0, 0))
    vec = lambda n: pl.BlockSpec((1, n), const)
    return pl.pallas_call(
        functools.partial(_post_kernel, k=k),
        grid=(B, L // tm),
        in_specs=cat * 4 + [
                  pl.BlockSpec((1, tm, D_ATT), lambda b, i: (b, i, 0)),
                  pl.BlockSpec((1, tm, D), lambda b, i: (b, i, 0)),
                  mod, mod, mod,
                  pl.BlockSpec(w_out_bf16.shape, const),
                  vec(D_RWKV), vec(D_RWKV), vec(D), vec(D),
                  pl.BlockSpec((E, D), const),
                  pl.BlockSpec(bones.shape, const)],
        out_specs=[pl.BlockSpec((1, tm, D), lambda b, i: (b, i, 0)),
                   pl.BlockSpec((1, tm, D), lambda b, i: (b, i, 0)),
                   pl.BlockSpec((1, E, tm), lambda b, i: (b, 0, i))],
        out_shape=[jax.ShapeDtypeStruct((B, L, D), F32),
                   jax.ShapeDtypeStruct((B, L, D), BF16),
                   jax.ShapeDtypeStruct((B, E, L), F32)],
        compiler_params=_params(("arbitrary", "arbitrary")),
        name="post_mix",
    )(*([yf] * k + [yb] * k + [bonus] * k + [g] * k), att, x, gt1, sh2, sc2, w_out_bf16, lnx_g, lnx_b,
      ln1_g, ln1_b, router_wt, bones)


def _cumsum_lanes(x):
    n = x.shape[1]
    lane = lax.broadcasted_iota(jnp.int32, x.shape, 1)
    s = 1
    while s < n:
        x = x + jnp.where(lane >= s, pltpu.roll(x, s, 1), 0)
        s *= 2
    return x


def _route_kernel(aff_ref, pos_ref, gate_ref, cum_ref, *, cap, tile):
    aff = aff_ref[0]
    E = aff.shape[0]
    count = lambda m: jnp.sum(jnp.where(m, 1, 0), axis=1, keepdims=True)
    thr_bits = jnp.zeros((E, 1), jnp.int32)
    for bit in range(29, -1, -1):
        cand = thr_bits | (1 << bit)
        cand_f = lax.bitcast_convert_type(cand, F32)
        thr_bits = jnp.where(count(aff >= cand_f) >= cap, cand, thr_bits)
    thr = lax.bitcast_convert_type(thr_bits, F32)
    above = aff > thr
    tie = aff == thr
    need = cap - count(above)
    tie_rank = _cumsum_lanes(jnp.where(tie, 1, 0))
    sel = above | (tie & (tie_rank <= need))
    slot = _cumsum_lanes(jnp.where(sel, 1, 0)) - 1
    pos = jnp.where(sel, slot, -1)
    gate = jnp.where(sel, aff, 0.0)
    for e in range(E):
        pos_ref[0, e] = pos[e:e + 1, :]
        gate_ref[0, e] = gate[e:e + 1, :]
    lane = lax.broadcasted_iota(jnp.int32, (E, 128), 1)
    chosen = jnp.where(sel, 1, 0)
    run = jnp.zeros((E, 1), jnp.int32)
    cum = jnp.zeros((E, 128), jnp.int32)
    n_tiles = aff.shape[1] // tile
    for i in range(n_tiles):
        cum = jnp.where(lane == i, run, cum)
        run = run + jnp.sum(chosen[:, i * tile:(i + 1) * tile], axis=1, keepdims=True)
    cum_ref[0] = jnp.where(lane == n_tiles, run, cum)


def _route(aff, cap, tile):
    B, E, L = aff.shape
    out = pl.BlockSpec((1, E, 1, L), lambda b: (b, 0, 0, 0))
    return pl.pallas_call(
        functools.partial(_route_kernel, cap=cap, tile=tile),
        grid=(B,),
        in_specs=[pl.BlockSpec((1, E, L), lambda b: (b, 0, 0))],
        out_specs=[out, out, pl.BlockSpec((1, E, 128), lambda b: (b, 0, 0))],
        out_shape=[jax.ShapeDtypeStruct((B, E, 1, L), jnp.int32),
                   jax.ShapeDtypeStruct((B, E, 1, L), F32),
                   jax.ShapeDtypeStruct((B, E, 128), jnp.int32)],
        compiler_params=_params(("arbitrary",)),
        name="route",
    )(aff)


def _expert_kernel(h_ref, pos_ref, wg_ref, wu_ref, wd_ref, y_ref, x_ref, acc_ref, *, cap):
    f = pl.program_id(2)

    @pl.when(f == 0)
    def _():
        pos = pos_ref[0, 0]
        slot = lax.broadcasted_iota(jnp.int32, (cap, pos.shape[1]), 0)
        onehot = jnp.where(pos == slot, 1.0, 0.0).astype(BF16)
        x_ref[...] = _dg(onehot, h_ref[0], _NN).astype(BF16)
        acc_ref[...] = jnp.zeros_like(acc_ref)

    xin = x_ref[...]
    hg = _dg(xin, wg_ref[0].astype(BF16), _NN)
    hu = _dg(xin, wu_ref[0].astype(BF16), _NN)
    hid = (hg * _sigmoid(hg) * hu).astype(BF16)
    acc_ref[...] += _dg(hid, wd_ref[0].astype(BF16), _NN)

    @pl.when(f == pl.num_programs(2) - 1)
    def _():
        y_ref[0, 0] = acc_ref[...].astype(BF16)


def _experts(h2, pos, wg, wu, wd, cap):
    B, L, D = h2.shape
    E, _, F = wg.shape
    fh = F // EXPERT_F_SPLITS
    return pl.pallas_call(
        functools.partial(_expert_kernel, cap=cap),
        grid=(B, E, EXPERT_F_SPLITS),
        in_specs=[pl.BlockSpec((1, L, D), lambda b, e, f: (b, 0, 0)),
                  pl.BlockSpec((1, 1, 1, L), lambda b, e, f: (b, e, 0, 0)),
                  pl.BlockSpec((1, D, fh), lambda b, e, f: (e, 0, f)),
                  pl.BlockSpec((1, D, fh), lambda b, e, f: (e, 0, f)),
                  pl.BlockSpec((1, fh, D), lambda b, e, f: (e, f, 0))],
        out_specs=pl.BlockSpec((1, 1, cap, D), lambda b, e, f: (b, e, 0, 0)),
        out_shape=jax.ShapeDtypeStruct((B, E, cap, D), BF16),
        scratch_shapes=[pltpu.VMEM((cap, D), BF16), pltpu.VMEM((cap, D), F32)],
        compiler_params=_params(("arbitrary", "arbitrary", "arbitrary")),
        name="experts",
    )(h2, pos, wg, wu, wd)


def _combine_kernel(cum_ref, pos_ref, gate_ref, y_ref, x1_ref, gt_ref, lg_ref, lb_ref, o_ref, acc_ref, *, cap):
    b, i, e = pl.program_id(0), pl.program_id(1), pl.program_id(2)

    @pl.when(e == 0)
    def _():
        acc_ref[...] = jnp.zeros_like(acc_ref)

    base = (b * pl.num_programs(2) + e) * 128
    lo, hi = cum_ref[base + i], cum_ref[base + i + 1]
    window = min(COMBINE_WINDOW, cap)
    align = min(128, window)
    start = pl.multiple_of(jnp.minimum((lo // align) * align, cap - window), align)

    def accumulate(first, width):
        pos = pos_ref[0, 0]
        slot = first + lax.broadcasted_iota(jnp.int32, (width, pos.shape[1]), 0)
        weights = jnp.where(pos == slot, gate_ref[0, 0], 0.0).astype(BF16)
        acc_ref[...] += _dg(weights, y_ref[0, 0, pl.ds(first, width), :], _TN)

    @pl.when((hi > lo) & (hi - start <= window))
    def _():
        accumulate(start, window)

    @pl.when(hi - start > window)
    def _():
        accumulate(0, cap)

    @pl.when(e == pl.num_programs(2) - 1)
    def _():
        z = ALPHA * x1_ref[0] + gt_ref[0] * acc_ref[...]
        o_ref[0] = _layer_norm(z) * lg_ref[...] + lb_ref[...]


def _combine(cum, pos, gate, y, x1, gt2, ln2_g, ln2_b, cap):
    B, L, D = x1.shape
    E = y.shape[1]
    tl = min(COMBINE_TILE, L)
    const = lambda b, i, e, cum: (0, 0)
    grid_spec = pltpu.PrefetchScalarGridSpec(
        num_scalar_prefetch=1,
        grid=(B, L // tl, E),
        in_specs=[pl.BlockSpec((1, 1, 1, tl), lambda b, i, e, cum: (b, e, 0, i)),
                  pl.BlockSpec((1, 1, 1, tl), lambda b, i, e, cum: (b, e, 0, i)),
                  pl.BlockSpec((1, 1, cap, D), lambda b, i, e, cum: (b, e, 0, 0)),
                  pl.BlockSpec((1, tl, D), lambda b, i, e, cum: (b, i, 0)),
                  pl.BlockSpec((1, 1, D), lambda b, i, e, cum: (b, 0, 0)),
                  pl.BlockSpec((1, D), const),
                  pl.BlockSpec((1, D), const)],
        out_specs=pl.BlockSpec((1, tl, D), lambda b, i, e, cum: (b, i, 0)),
        scratch_shapes=[pltpu.VMEM((tl, D), F32)])
    return pl.pallas_call(
        functools.partial(_combine_kernel, cap=cap),
        grid_spec=grid_spec,
        out_shape=jax.ShapeDtypeStruct((B, L, D), F32),
        compiler_params=_params(("arbitrary", "arbitrary", "arbitrary")),
        name="combine",
    )(cum, pos, gate, y, x1, gt2, ln2_g, ln2_b)


def _rope_tables(n_ctx, n_lat):
    lane = jnp.arange(128)
    j = lane % HEAD_DIM
    axis = j // (2 * ROPE_PAIRS)
    upper = (j % (2 * ROPE_PAIRS)) >= ROPE_PAIRS
    inv = ROPE_THETA ** (-(j % ROPE_PAIRS).astype(F32) / ROPE_PAIRS)
    t = jnp.arange(n_lat)
    coord = jnp.where(axis[None, :] == 0, (t // GRID_W)[:, None], (t % GRID_W)[:, None]).astype(F32)
    ang = coord * inv[None, :]
    cos, sin = jnp.cos(ang), jnp.sin(ang)
    s_lo = jnp.where(upper[None, :], 0.0, -sin)
    s_hi = jnp.where(upper[None, :], sin, 0.0)
    pad = lambda a, fill: jnp.concatenate([jnp.full((n_ctx, 128), fill, F32), a], axis=0)
    return pad(cos, 1.0), pad(s_lo, 0.0), pad(s_hi, 0.0)


def _head_ones(n):
    h = jnp.arange(n) // HEAD_DIM
    return (h[:, None] == h[None, :]).astype(BF16)


def kernel(x, c, ctx, c_ctx, w_ada, b_ada, w_in, q_gain, k_gain, tshift_mu, decay_w0, decay_up, iclr_a0,
           iclr_up, gate_up, k_k, k_a, r_k, lnx_g, lnx_b, w_out, ln1_g, ln1_b, router_w, exp_w_gate,
           exp_w_up, exp_w_down, ln2_g, ln2_b):
    B, L, D = x.shape
    n_ctx = ctx.shape[1]
    li = 0
    row = lambda a: a.reshape(1, -1)

    n_rows = -(-(B + 1) // 8) * 8
    c_rows = jnp.concatenate([c, c_ctx[None, :], jnp.zeros((n_rows - B - 1, D), F32)], axis=0)
    mod = _ada_mod(c_rows, w_ada[li], b_ada[li]).reshape(n_rows, N_MOD, 1, D)
    sh1, sc1, gt1, sh2, sc2, gt2 = (mod[:, m] for m in range(N_MOD))

    bones = _head_ones(D_RWKV)
    cos, s_lo, s_hi = _rope_tables(n_ctx, L)
    q, k, v, rw = _inproj(x, ctx, sh1, sc1, w_in[li].astype(BF16), bones,
                          row(jnp.tile(q_gain[li], ATT_HEADS)), row(jnp.tile(k_gain[li], ATT_KV_HEADS)),
                          cos, s_lo, s_hi)
    Tc = n_ctx + L
    vt = v.reshape(B, ATT_KV_HEADS, Tc // ATT_K_TILE, ATT_K_TILE, HEAD_DIM).swapaxes(-1, -2)
    vt = jnp.concatenate([vt, jnp.ones(vt.shape[:3] + (16, ATT_K_TILE), BF16)], axis=3)
    att = _attention(q, k, vt)

    zpad = jnp.zeros((2, D_LORA // 2, D_RWKV), F32)
    dup = jnp.concatenate([decay_up[li], zpad], axis=1)
    aup = jnp.concatenate([zpad, iclr_up[li]], axis=1)
    scan_shared, g, bonus, scan_dir = _rwkv_prep(
        rw, n_ctx, row(tshift_mu[li]), decay_w0[li], dup, iclr_a0[li], aup, gate_up[li],
        row(k_k[li]), row(k_a[li]), row(r_k[li]), bones)
    yf, yb = _wkv_scan(scan_shared, scan_dir, n_ctx)

    x1, h2, aff = _post(yf, yb, bonus, g, att, x, gt1, sh2, sc2, w_out[li].astype(BF16),
                        row(lnx_g[li]), row(lnx_b[li]), row(ln1_g[li]), row(ln1_b[li]),
                        router_w[li].T, bones, n_ctx)
    cap = CAPACITY_FACTOR * L // N_EXPERTS
    pos, gate, cum = _route(aff, cap, min(COMBINE_TILE, L))
    y = _experts(h2, pos, exp_w_gate[li], exp_w_up[li], exp_w_down[li], cap)
    return _combine(cum.reshape(-1), pos, gate, y, x1, gt2, row(ln2_g[li]), row(ln2_b[li]), cap)
```

```python
import functools

import jax
import jax.numpy as jnp
from jax import lax
from jax.experimental import pallas as pl
from jax.experimental.pallas import tpu as pltpu

F32 = jnp.float32
BF16 = jnp.bfloat16

HEAD_DIM = 64
ATT_HEADS = 8
ATT_KV_HEADS = 2
GQA_GROUP = ATT_HEADS // ATT_KV_HEADS
D_ATT = ATT_HEADS * HEAD_DIM
D_ATT_KV = ATT_KV_HEADS * HEAD_DIM
ATT_SCALE = HEAD_DIM ** -0.5
LOG2_E = 1.4426950408889634
EXP_M05 = 0.6065306597126334
ROPE_THETA = 10000.0
ROPE_PAIRS = HEAD_DIM // 4
GRID_W = 64
RWKV_HEAD = 64
D_RWKV = 512
D_LORA = 128
N_EXPERTS = 16
CAPACITY_FACTOR = 2
N_MOD = 6
LN_EPS = 1e-5
RMS_EPS = 1e-6
GN_EPS = 64e-5
L2_EPS = 1e-12
DEPTH = 1
ALPHA = (2.0 * DEPTH) ** 0.25

WKV_CHUNK = 64
WKV_BLOCK = 256
ROW_TILE = 256
POST_TILE = 512
ATT_Q_TILE = 128
ATT_K_TILE = 256
COMBINE_TILE = 1024
COMBINE_WINDOW = 256
VMEM_LIMIT = 48 * 1024 * 1024


def _params(sem):
    return pltpu.CompilerParams(dimension_semantics=sem, vmem_limit_bytes=VMEM_LIMIT)


def _split3(x):
    hi = x.astype(BF16)
    r1 = x - hi.astype(F32)
    mid = r1.astype(BF16)
    lo = (r1 - mid.astype(F32)).astype(BF16)
    return hi, mid, lo


def _split2(x):
    hi = x.astype(BF16)
    lo = (x - hi.astype(F32)).astype(BF16)
    return hi, lo


def _hi(x):
    return (x.astype(BF16),)


def _dg(a, b, dims):
    return lax.dot_general(a, b, (dims, ((), ())), preferred_element_type=F32)


_NN = ((1,), (0,))
_NT = ((1,), (1,))
_TN = ((0,), (0,))


def _mm(a, b, dims=_NN, passes=3):
    if passes == 1:
        return _dg(a.astype(BF16), b.astype(BF16), dims)
    if passes == 3:
        ah, al = _split2(a)
        bh, bl = _split2(b)
        return _dg(ah, bh, dims) + (_dg(ah, bl, dims) + _dg(al, bh, dims))
    ah, am, al = _split3(a)
    bh, bm, bl = _split3(b)
    return (_dg(ah, bh, dims) + (_dg(ah, bm, dims) + _dg(am, bh, dims))
            + (_dg(am, bm, dims) + _dg(ah, bl, dims) + _dg(al, bh, dims)))


def _pieces(x, passes):
    return (x.astype(BF16),) if passes == 1 else _split2(x)


def _mmp(a, b, dims):
    out = _dg(a[0], b[0], dims)
    if len(a) > 1 and len(b) > 1:
        return out + (_dg(a[0], b[1], dims) + _dg(a[1], b[0], dims))
    if len(a) > 1:
        return out + _dg(a[1], b[0], dims)
    if len(b) > 1:
        return out + _dg(a[0], b[1], dims)
    return out


def _mm_exact_lhs(a_bf16, b, dims=_NN):
    bh, bm, bl = _split3(b)
    return _dg(a_bf16, bh, dims) + (_dg(a_bf16, bm, dims) + _dg(a_bf16, bl, dims))


def _mm_exact_rhs(a, b_bf16, dims=_NN):
    ah, al = _split2(a)
    return _dg(ah, b_bf16, dims) + _dg(al, b_bf16, dims)


def _sigmoid(x):
    return 1.0 / (1.0 + jnp.exp(-x))


def _layer_norm(x):
    mu = jnp.mean(x, axis=-1, keepdims=True)
    xc = x - mu
    var = jnp.mean(xc * xc, axis=-1, keepdims=True)
    return xc * lax.rsqrt(var + LN_EPS)


def _ada_kernel(c_ref, w_ref, b_ref, o_ref):
    c = c_ref[...]
    o_ref[...] = _mm(c * _sigmoid(c), w_ref[...], passes=6) + b_ref[...]


def _ada_mod(c_rows, w_ada, b_ada):
    R, D = c_rows.shape
    N = w_ada.shape[1]
    tn = 1024
    return pl.pallas_call(
        _ada_kernel,
        grid=(N // tn,),
        in_specs=[pl.BlockSpec((R, D), lambda j: (0, 0)),
                  pl.BlockSpec((D, tn), lambda j: (0, j)),
                  pl.BlockSpec((1, tn), lambda j: (0, j))],
        out_specs=pl.BlockSpec((R, tn), lambda j: (0, j)),
        out_shape=jax.ShapeDtypeStruct((R, N), F32),
        compiler_params=_params(("arbitrary",)),
        name="ada_mod",
    )(c_rows, w_ada, b_ada.reshape(1, N))


def _rope(t, cos, sin_lo, sin_hi, reps):
    w = t.shape[-1]
    tile = lambda a: jnp.concatenate([a] * reps, axis=1) if reps > 1 else a
    return (t * tile(cos) + pltpu.roll(t, w - ROPE_PAIRS, 1) * tile(sin_lo)
            + pltpu.roll(t, ROPE_PAIRS, 1) * tile(sin_hi))


def _inproj_kernel(x_ref, ctx_ref, sh_ref, sc_ref, w_ref, bones_ref, qg_ref, kg_ref,
                   cos_ref, slo_ref, shi_ref, q_ref, k_ref, v_ref, rw_ref, *, n_ctx_tiles):
    i = pl.program_id(1)
    x = jnp.where(i < n_ctx_tiles, ctx_ref[0], x_ref[0])
    h = _layer_norm(x) * (1.0 + sc_ref[0]) + sh_ref[0]
    p = _dg(h.astype(BF16), w_ref[...], _NN)
    bones = bones_ref[...]
    cos, slo, shi = cos_ref[...], slo_ref[...], shi_ref[...]

    q = p[:, :D_ATT]
    ss = _mm_exact_rhs(q * q, bones)
    q = q * lax.rsqrt(ss * (1.0 / HEAD_DIM) + RMS_EPS) * qg_ref[...]
    q = _rope(q, cos, slo, shi, D_ATT // 128) * (ATT_SCALE * LOG2_E)
    q_ref[0] = q.astype(BF16)

    k = p[:, D_ATT:D_ATT + D_ATT_KV]
    ss = _mm_exact_rhs(k * k, bones[:D_ATT_KV, :D_ATT_KV])
    k = k * lax.rsqrt(ss * (1.0 / HEAD_DIM) + RMS_EPS) * kg_ref[...]
    k = _rope(k, cos, slo, shi, D_ATT_KV // 128).astype(BF16)
    v = p[:, D_ATT + D_ATT_KV:D_ATT + 2 * D_ATT_KV].astype(BF16)
    for g in range(ATT_KV_HEADS):
        k_ref[0, g] = k[:, g * HEAD_DIM:(g + 1) * HEAD_DIM]
        v_ref[0, g] = v[:, g * HEAD_DIM:(g + 1) * HEAD_DIM]
    rw_ref[0] = p[:, D_ATT + 2 * D_ATT_KV:]


def _inproj(x, ctx, shift, scale, w_in_bf16, bones, qg, kg, cos, slo, shi):
    B, L, D = x.shape
    n_ctx = ctx.shape[1]
    tm = ROW_TILE
    nct = n_ctx // tm
    Tc = n_ctx + L
    d_in = w_in_bf16.shape[1]
    d_rw = d_in - D_ATT - 2 * D_ATT_KV
    mod_row = lambda b, i: (jnp.where(i < nct, B, b), 0, 0)
    const = lambda b, i: (0, 0)
    return pl.pallas_call(
        functools.partial(_inproj_kernel, n_ctx_tiles=nct),
        grid=(B, Tc // tm),
        in_specs=[pl.BlockSpec((1, tm, D), lambda b, i: (b, jnp.maximum(i - nct, 0), 0)),
                  pl.BlockSpec((1, tm, D), lambda b, i: (b, jnp.minimum(i, nct - 1), 0)),
                  pl.BlockSpec((1, 1, D), mod_row),
                  pl.BlockSpec((1, 1, D), mod_row),
                  pl.BlockSpec((D, d_in), const),
                  pl.BlockSpec(bones.shape, const),
                  pl.BlockSpec((1, D_ATT), const),
                  pl.BlockSpec((1, D_ATT_KV), const),
                  pl.BlockSpec((tm, 128), lambda b, i: (i, 0)),
                  pl.BlockSpec((tm, 128), lambda b, i: (i, 0)),
                  pl.BlockSpec((tm, 128), lambda b, i: (i, 0))],
        out_specs=[pl.BlockSpec((1, tm, D_ATT), lambda b, i: (b, jnp.maximum(i - nct, 0), 0)),
                   pl.BlockSpec((1, ATT_KV_HEADS, tm, HEAD_DIM), lambda b, i: (b, 0, i, 0)),
                   pl.BlockSpec((1, ATT_KV_HEADS, tm, HEAD_DIM), lambda b, i: (b, 0, i, 0)),
                   pl.BlockSpec((1, tm, d_rw), lambda b, i: (b, i, 0))],
        out_shape=[jax.ShapeDtypeStruct((B, L, D_ATT), BF16),
                   jax.ShapeDtypeStruct((B, ATT_KV_HEADS, Tc, HEAD_DIM), BF16),
                   jax.ShapeDtypeStruct((B, ATT_KV_HEADS, Tc, HEAD_DIM), BF16),
                   jax.ShapeDtypeStruct((B, Tc, d_rw), F32)],
        compiler_params=_params(("arbitrary", "arbitrary")),
        name="inproj",
    )(x, ctx, shift, scale, w_in_bf16, bones, qg, kg, cos, slo, shi)


def _attn_kernel(q_ref, k_ref, vt_ref, o_ref, s0, s1, p0, p1, m0, m1):
    i = pl.program_id(0)
    nk, tk, cols = s0.shape

    @pl.when(i == 0)
    def _():
        for ref in (s0, s1, m0, m1):
            ref[...] = jnp.zeros_like(ref)
        p0[...] = jnp.ones_like(p0)
        p1[...] = jnp.ones_like(p1)

    def step(s_w, m_w, s_r, m_r, p_w, p_r):
        q = q_ref[0]
        tq = q.shape[0]
        qs = jnp.concatenate([q[:, h * HEAD_DIM:(h + 1) * HEAD_DIM] for h in range(GQA_GROUP)], axis=0)
        m_prev = m_r[0:1, :]

        def body(j, carry):
            m_run, acc = carry
            off = pl.multiple_of(j * tk, tk)
            s_new = _dg(k_ref[0, 0, pl.ds(off, tk), :], qs, _NT)
            s_w[j] = s_new
            m_run = jnp.maximum(m_run, jnp.max(s_new.reshape(tk // 8, 8, cols), axis=0))
            p_w[j] = jnp.exp2(s_r[j] - m_prev).astype(BF16)
            acc = acc + _dg(vt_ref[0, 0, j], p_r[j], _NN)
            return m_run, acc

        init = (jnp.full((8, cols), -jnp.inf, F32), jnp.zeros((vt_ref.shape[3], cols), F32))
        m_run, acc = lax.fori_loop(0, nk, body, init, unroll=8)
        m_w[...] = jnp.broadcast_to(jnp.max(m_run, axis=0, keepdims=True), m_w.shape)
        o = acc[:HEAD_DIM] / acc[HEAD_DIM:HEAD_DIM + 1]
        o_ref[0] = jnp.concatenate([o[:, h * tq:(h + 1) * tq].T for h in range(GQA_GROUP)],
                                   axis=1).astype(BF16)

    @pl.when(i % 2 == 0)
    def _():
        step(s0, m0, s1, m1, p1, p0)

    @pl.when(i % 2 == 1)
    def _():
        step(s1, m1, s0, m0, p0, p1)


def _attention(q, k, vt):
    B, L, _ = q.shape
    Tc = k.shape[2]
    tq = ATT_Q_TILE
    n = L // tq
    gw = GQA_GROUP * HEAD_DIM
    cols = GQA_GROUP * tq
    tk = ATT_K_TILE
    nk = Tc // tk
    n_tiles = B * ATT_KV_HEADS * n

    def tile(s):
        t = jnp.clip(s, 0, n_tiles - 1)
        return t // (ATT_KV_HEADS * n), (t // n) % ATT_KV_HEADS, t % n

    def q_map(s):
        b, g, i = tile(s)
        return b, i, g

    def out_map(s):
        b, g, i = tile(s - 2)
        return b, i, g

    return pl.pallas_call(
        _attn_kernel,
        grid=(n_tiles + 2,),
        in_specs=[pl.BlockSpec((1, tq, gw), q_map),
                  pl.BlockSpec((1, 1, Tc, HEAD_DIM), lambda s: tile(s)[:2] + (0, 0)),
                  pl.BlockSpec((1, 1, nk, vt.shape[3], tk), lambda s: tile(s - 2)[:2] + (0, 0, 0))],
        out_specs=pl.BlockSpec((1, tq, gw), out_map),
        out_shape=jax.ShapeDtypeStruct((B, L, D_ATT), BF16),
        scratch_shapes=[pltpu.VMEM((nk, tk, cols), F32), pltpu.VMEM((nk, tk, cols), F32),
                        pltpu.VMEM((nk, tk, cols), BF16), pltpu.VMEM((nk, tk, cols), BF16),
                        pltpu.VMEM((8, cols), F32), pltpu.VMEM((8, cols), F32)],
        compiler_params=_params(("arbitrary",)),
        name="attention",
    )(q, k, vt)


def _rwkv_prep_kernel(rw_ref, prev_ref, next_ref, mu_ref, w0_ref, dup_ref, a0_ref, aup_ref, gup_ref,
                      kk_ref, ka_ref, rk_ref, bones_ref,
                      sh_o, g_o, bonus_o, dir_o, *, n_ctx, n_tok):
    i = pl.program_id(1)
    p = rw_ref[0]
    tm = p.shape[0]
    t0 = i * tm
    has_prev = jnp.where((t0 == 0) | (t0 == n_ctx), 0.0, 1.0)
    has_next = jnp.where((t0 + tm == n_ctx) | (t0 + tm == n_tok), 0.0, 1.0)
    row = lax.broadcasted_iota(jnp.int32, (tm, 1), 0)
    up = jnp.where(row == 0, prev_ref[0, 7:8, :] * has_prev, pltpu.roll(p, 1, 0))
    dn = jnp.where(row == tm - 1, next_ref[0, 0:1, :] * has_next, pltpu.roll(p, tm - 1, 0))
    s = p + (0.5 * (up + dn) - p) * mu_ref[...]

    D = D_RWKV
    r, k, v = s[:, :D], s[:, D:2 * D], s[:, 2 * D:3 * D]
    lora = s[:, 3 * D:3 * D + D_LORA]
    gl = s[:, 3 * D + D_LORA:]
    bones = bones_ref[...]

    g_o[0] = _mm(_sigmoid(gl), gup_ref[...], passes=3)
    kk = k * kk_ref[...]
    nrm = jnp.sqrt(_mm_exact_rhs(kk * kk, bones))
    kk = kk / jnp.maximum(nrm, L2_EPS)
    wt = jnp.tanh(lora)
    sh_o[0, :, 0:D] = r
    sh_o[0, :, D:2 * D] = v
    sh_o[0, :, 2 * D:3 * D] = kk
    bonus_o[0] = _mm_exact_rhs(r * k * rk_ref[...], bones) * v
    for d in range(2):
        z = w0_ref[d:d + 1, :] + _mm(wt, dup_ref[d], passes=3)
        dir_o[d, 0, :, 0:D] = -EXP_M05 * _sigmoid(z)
        a = _sigmoid(a0_ref[d:d + 1, :] + _mm(lora, aup_ref[d], passes=3))
        dir_o[d, 0, :, D:2 * D] = k * (1.0 + (a - 1.0) * ka_ref[...])
        dir_o[d, 0, :, 2 * D:3 * D] = kk * a


def _rwkv_prep(rw, n_ctx, mu, w0, dup, a0, aup, gup, k_k, k_a, r_k, bones):
    B, Tc, d_rw = rw.shape
    tm = ROW_TILE
    D = D_RWKV
    const2 = lambda b, i: (0, 0)
    const3 = lambda b, i: (0, 0, 0)
    tok = lambda n: pl.BlockSpec((1, tm, n * D), lambda b, i: (b, i, 0))
    tok2 = pl.BlockSpec((2, 1, tm, 3 * D), lambda b, i: (0, b, i, 0))
    s1 = lambda n: jax.ShapeDtypeStruct((B, Tc, n * D), F32)
    s2 = jax.ShapeDtypeStruct((2, B, Tc, 3 * D), F32)
    hb = tm // 8
    return pl.pallas_call(
        functools.partial(_rwkv_prep_kernel, n_ctx=n_ctx, n_tok=Tc),
        grid=(B, Tc // tm),
        in_specs=[pl.BlockSpec((1, tm, d_rw), lambda b, i: (b, i, 0)),
                  pl.BlockSpec((1, 8, d_rw), lambda b, i: (b, jnp.maximum(i * hb - 1, 0), 0)),
                  pl.BlockSpec((1, 8, d_rw), lambda b, i: (b, jnp.minimum((i + 1) * hb, Tc // 8 - 1), 0)),
                  pl.BlockSpec((1, d_rw), const2),
                  pl.BlockSpec((2, D), const2),
                  pl.BlockSpec((2, D_LORA, D), const3),
                  pl.BlockSpec((2, D), const2),
                  pl.BlockSpec((2, D_LORA, D), const3),
                  pl.BlockSpec((D_LORA, D), const2),
                  pl.BlockSpec((1, D), const2),
                  pl.BlockSpec((1, D), const2),
                  pl.BlockSpec((1, D), const2),
                  pl.BlockSpec(bones.shape, const2)],
        out_specs=[tok(3), tok(1), tok(1), tok2],
        out_shape=[s1(3), s1(1), s1(1), s2],
        compiler_params=_params(("arbitrary", "arbitrary")),
        name="rwkv_prep",
    )(rw, rw, rw, mu, w0, dup, a0, aup, gup, k_k, k_a, r_k, bones)


def _wkv_kernel(sh0, dir0, sh1, dir1, y0, y1, s_ref, *, n_heads):
    @pl.when(pl.program_id(1) == 0)
    def _():
        s_ref[...] = jnp.zeros_like(s_ref)

    C = WKV_CHUNK
    n_sub = sh0.shape[1] // C
    lax.fori_loop(0, n_sub, functools.partial(_wkv_chunk, refs=(sh0, dir0, sh1, dir1, y0, y1, s_ref),
                                              n_heads=n_heads, n_sub=n_sub), 0, unroll=2)


def _wkv_chunk(sub, carry, *, refs, n_heads, n_sub):
    sh0, dir0, sh1, dir1, y0, y1, s_ref = refs
    C = WKV_CHUNK
    N = RWKV_HEAD
    D = n_heads * N
    row = lax.broadcasted_iota(jnp.int32, (C, 2 * N), 0)
    lane = lax.broadcasted_iota(jnp.int32, (C, 2 * N), 1)
    col = lane % N
    first_half = lane < N
    trow = lax.broadcasted_iota(jnp.int32, (C, C), 0)
    tcol = lax.broadcasted_iota(jnp.int32, (C, C), 1)
    cat = lambda xs, ys: tuple(jnp.concatenate([x, y], axis=0) for x, y in zip(xs, ys))
    offs = (pl.multiple_of(sub * C, C), pl.multiple_of((n_sub - 1 - sub) * C, C))
    y_refs = (y0, y1)

    chains = []
    for d, (sh_ref, dir_ref) in enumerate(((sh0, dir0), (sh1, dir1))):
        rows = pl.ds(offs[d], C)
        strict, incl = (row < col, row <= col) if d else (row > col, row >= col)
        tri = (trow <= tcol) if d else (trow >= tcol)
        lw = dir_ref[0, 0, rows, 0:D]
        cl = _mm_exact_lhs(jnp.where(tri, 1.0, 0.0).astype(BF16), lw)
        ecl = jnp.exp(cl)
        eneg = jnp.exp(-cl)
        last = 0 if d else C - 1
        etot = ecl[last:last + 1, :]
        kt = dir_ref[0, 0, rows, D:2 * D] * eneg
        bt = dir_ref[0, 0, rows, 2 * D:3 * D] * eneg
        full = dict(at=_split2(-(sh_ref[0, rows, 2 * D:3 * D] * jnp.exp(cl - lw))),
                    rt=_split2(sh_ref[0, rows, 0:D] * ecl),
                    v=_split2(sh_ref[0, rows, D:2 * D]),
                    kt=_hi(kt), bt=_hi(bt), kh=_hi(kt * etot), bh=_hi(bt * etot))
        for h in range(n_heads):
            ch = {k: tuple(p[:, h * N:(h + 1) * N] for p in val) for k, val in full.items()}
            ch.update(d=d, h=h, strict=strict, incl=incl, etot=etot[:, h * N:(h + 1) * N], s0=s_ref[d, h])
            chains.append(ch)

    for ch in chains:
        rhs = cat(cat(ch["kt"], ch["bt"]), _hi(ch["s0"]))
        g = _mmp(cat(ch["at"], ch["rt"]), rhs, _NT)
        ch["top"] = jnp.where(ch["strict"], g[:C, :2 * N], 0.0)
        ch["bot"] = jnp.where(ch["incl"], g[C:, :2 * N], 0.0)
        ch["as0"], ch["rs0"] = g[:C, 2 * N:], g[C:, 2 * N:]
    for ch in chains:
        z = ch["as0"] + _mmp(_split2(ch["top"][:, :N]), ch["v"][:1], _NN)
        ch["w"] = jnp.where(first_half, jnp.concatenate([z, jnp.zeros_like(z)], axis=1), ch["top"])
        ch["pw"] = pltpu.roll(ch["top"], N, 1)[:, :N]
    n = 1
    while True:
        for ch in chains:
            ch["prod"] = _mmp(_split2(ch["pw"]), _split2(ch["w"]), _NN)
            ch["w"] = jnp.where(first_half, ch["w"], 0.0) + ch["prod"]
        n *= 2
        if n >= C:
            break
        for ch in chains:
            ch["pw"] = pltpu.roll(ch["prod"], N, 1)[:, :N]
    for ch in chains:
        ch["up"] = _split2(ch["w"][:, :N])
        y = ch["rs0"] + _mmp(_split2(ch["bot"]), cat(ch["v"][:1], ch["up"][:1]), _NN)
        y_refs[ch["d"]][0, pl.ds(offs[ch["d"]], C), ch["h"] * N:(ch["h"] + 1) * N] = y
    for ch in chains:
        upd = _mmp(cat(ch["up"], ch["v"]), cat(ch["bh"], ch["kh"]), _TN)
        s_ref[ch["d"], ch["h"]] = ch["s0"] * ch["etot"] + upd
    return carry


def _wkv_scan(shared, per_dir, n_ctx):
    B, T, D3 = shared.shape
    D = D3 // 3
    tb = WKV_BLOCK
    n_blocks = T // tb
    n_cb = n_ctx // tb
    tok = (lambda c: c,
           lambda c: jnp.where(c < n_cb, n_cb - 1 - c, n_blocks - 1 + n_cb - c))
    in_specs, args = [], []
    for d in range(2):
        in_specs += [pl.BlockSpec((1, tb, D3), lambda bi, c, d=d: (bi, tok[d](c), 0)),
                     pl.BlockSpec((1, 1, tb, D3), lambda bi, c, d=d: (d, bi, tok[d](c), 0))]
        args += [shared, per_dir]
    y = jax.ShapeDtypeStruct((B, T, D), F32)
    return pl.pallas_call(
        functools.partial(_wkv_kernel, n_heads=D // RWKV_HEAD),
        grid=(B, n_blocks),
        in_specs=in_specs,
        out_specs=[pl.BlockSpec((1, tb, D), lambda bi, c, d=d: (bi, tok[d](c), 0)) for d in range(2)],
        out_shape=[y, y],
        scratch_shapes=[pltpu.VMEM((2, D // RWKV_HEAD, RWKV_HEAD, RWKV_HEAD), F32)],
        compiler_params=_params(("arbitrary", "arbitrary")),
        name="wkv_scan",
    )(*args)


def _post_kernel(*refs, k):
    rows = lambda j: jnp.concatenate([r[0] for r in refs[j * k:(j + 1) * k]], axis=0)
    (att_ref, x_ref, gt_ref, sh_ref, sc_ref, wo_ref, lxg_ref, lxb_ref, l1g_ref, l1b_ref, rwt_ref, bones_ref,
     x1_ref, h2_ref, aff_ref) = refs[4 * k:]
    bones = bones_ref[...]
    y = rows(0) + rows(1)
    inv = 1.0 / RWKV_HEAD
    mu = _mm_exact_rhs(y, bones) * inv
    yc = y - mu
    var = _mm_exact_rhs(yc * yc, bones) * inv
    yn = yc * lax.rsqrt(var + GN_EPS) * lxg_ref[...] + lxb_ref[...]
    rw_out = (yn + rows(2)) * rows(3)
    mix = _dg(att_ref[0], wo_ref[:D_ATT, :], _NN) + _dg(rw_out.astype(BF16), wo_ref[D_ATT:, :], _NN)
    x1 = _layer_norm(ALPHA * x_ref[0] + gt_ref[0] * mix) * l1g_ref[...] + l1b_ref[...]
    x1_ref[0] = x1
    h2 = _layer_norm(x1) * (1.0 + sc_ref[0]) + sh_ref[0]
    h2_ref[0] = h2.astype(BF16)
    logits = _mm(rwt_ref[...], h2, _NT, passes=3)
    e = jnp.exp(logits - jnp.max(logits, axis=0, keepdims=True))
    aff_ref[0] = e / jnp.sum(e, axis=0, keepdims=True)


def _post(yf, yb, bonus, g, att, x, gt1, sh2, sc2, w_out_bf16, lnx_g, lnx_b, ln1_g, ln1_b, router_wt,
          bones, n_ctx):
    B, L, D = x.shape
    tm = min(POST_TILE, L)
    k = tm // ROW_TILE
    nct = n_ctx // ROW_TILE
    E = router_wt.shape[0]
    const = lambda b, i: (0, 0)
    cat = [pl.BlockSpec((1, ROW_TILE, D_RWKV), lambda b, i, j=j: (b, i * k + j + nct, 0)) for j in range(k)]
    mod = pl.BlockSpec((1, 1, D), lambda b, i: (b, 0, 0))
    vec = lambda n: pl.BlockSpec((1, n), const)
    return pl.pallas_call(
        functools.partial(_post_kernel, k=k),
        grid=(B, L // tm),
        in_specs=cat * 4 + [
                  pl.BlockSpec((1, tm, D_ATT), lambda b, i: (b, i, 0)),
                  pl.BlockSpec((1, tm, D), lambda b, i: (b, i, 0)),
                  mod, mod, mod,
                  pl.BlockSpec(w_out_bf16.shape, const),
                  vec(D_RWKV), vec(D_RWKV), vec(D), vec(D),
                  pl.BlockSpec((E, D), const),
                  pl.BlockSpec(bones.shape, const)],
        out_specs=[pl.BlockSpec((1, tm, D), lambda b, i: (b, i, 0)),
                   pl.BlockSpec((1, tm, D), lambda b, i: (b, i, 0)),
                   pl.BlockSpec((1, E, tm), lambda b, i: (b, 0, i))],
        out_shape=[jax.ShapeDtypeStruct((B, L, D), F32),
                   jax.ShapeDtypeStruct((B, L, D), BF16),
                   jax.ShapeDtypeStruct((B, E, L), F32)],
        compiler_params=_params(("arbitrary", "arbitrary")),
        name="post_mix",
    )(*([yf] * k + [yb] * k + [bonus] * k + [g] * k), att, x, gt1, sh2, sc2, w_out_bf16, lnx_g, lnx_b,
      ln1_g, ln1_b, router_wt, bones)


def _cumsum_lanes(x):
    n = x.shape[1]
    lane = lax.broadcasted_iota(jnp.int32, x.shape, 1)
    s = 1
    while s < n:
        x = x + jnp.where(lane >= s, pltpu.roll(x, s, 1), 0)
        s *= 2
    return x


def _route_kernel(aff_ref, pos_ref, gate_ref, cum_ref, *, cap, tile):
    aff = aff_ref[0]
    E = aff.shape[0]
    count = lambda m: jnp.sum(jnp.where(m, 1, 0), axis=1, keepdims=True)
    thr_bits = jnp.zeros((E, 1), jnp.int32)
    for bit in range(29, -1, -1):
        cand = thr_bits | (1 << bit)
        cand_f = lax.bitcast_convert_type(cand, F32)
        thr_bits = jnp.where(count(aff >= cand_f) >= cap, cand, thr_bits)
    thr = lax.bitcast_convert_type(thr_bits, F32)
    above = aff > thr
    tie = aff == thr
    need = cap - count(above)
    tie_rank = _cumsum_lanes(jnp.where(tie, 1, 0))
    sel = above | (tie & (tie_rank <= need))
    slot = _cumsum_lanes(jnp.where(sel, 1, 0)) - 1
    pos = jnp.where(sel, slot, -1)
    gate = jnp.where(sel, aff, 0.0)
    for e in range(E):
        pos_ref[0, e] = pos[e:e + 1, :]
        gate_ref[0, e] = gate[e:e + 1, :]
    lane = lax.broadcasted_iota(jnp.int32, (E, 128), 1)
    chosen = jnp.where(sel, 1, 0)
    run = jnp.zeros((E, 1), jnp.int32)
    cum = jnp.zeros((E, 128), jnp.int32)
    n_tiles = aff.shape[1] // tile
    for i in range(n_tiles):
        cum = jnp.where(lane == i, run, cum)
        run = run + jnp.sum(chosen[:, i * tile:(i + 1) * tile], axis=1, keepdims=True)
    cum_ref[0] = jnp.where(lane == n_tiles, run, cum)


def _route(aff, cap, tile):
    B, E, L = aff.shape
    out = pl.BlockSpec((1, E, 1, L), lambda b: (b, 0, 0, 0))
    return pl.pallas_call(
        functools.partial(_route_kernel, cap=cap, tile=tile),
        grid=(B,),
        in_specs=[pl.BlockSpec((1, E, L), lambda b: (b, 0, 0))],
        out_specs=[out, out, pl.BlockSpec((1, E, 128), lambda b: (b, 0, 0))],
        out_shape=[jax.ShapeDtypeStruct((B, E, 1, L), jnp.int32),
                   jax.ShapeDtypeStruct((B, E, 1, L), F32),
                   jax.ShapeDtypeStruct((B, E, 128), jnp.int32)],
        compiler_params=_params(("arbitrary",)),
        name="route",
    )(aff)


def _expert_kernel(h_ref, pos_ref, wg_ref, wu_ref, wd_ref, y_ref, *, cap):
    pos = pos_ref[0, 0]
    slot = lax.broadcasted_iota(jnp.int32, (cap, pos.shape[1]), 0)
    onehot = jnp.where(pos == slot, 1.0, 0.0).astype(BF16)
    xin = _dg(onehot, h_ref[0], _NN).astype(BF16)
    hg = _dg(xin, wg_ref[0], _NN)
    hu = _dg(xin, wu_ref[0], _NN)
    hid = (hg * _sigmoid(hg) * hu).astype(BF16)
    y_ref[0, 0] = _dg(hid, wd_ref[0], _NN).astype(BF16)


def _experts(h2, pos, wg, wu, wd, cap):
    B, L, D = h2.shape
    E, _, F = wg.shape
    rowv = pl.BlockSpec((1, 1, 1, L), lambda b, e: (b, e, 0, 0))
    return pl.pallas_call(
        functools.partial(_expert_kernel, cap=cap),
        grid=(B, E),
        in_specs=[pl.BlockSpec((1, L, D), lambda b, e: (b, 0, 0)),
                  rowv,
                  pl.BlockSpec((1, D, F), lambda b, e: (e, 0, 0)),
                  pl.BlockSpec((1, D, F), lambda b, e: (e, 0, 0)),
                  pl.BlockSpec((1, F, D), lambda b, e: (e, 0, 0))],
        out_specs=pl.BlockSpec((1, 1, cap, D), lambda b, e: (b, e, 0, 0)),
        out_shape=jax.ShapeDtypeStruct((B, E, cap, D), BF16),
        compiler_params=_params(("arbitrary", "arbitrary")),
        name="experts",
    )(h2, pos, wg, wu, wd)


def _combine_kernel(cum_ref, pos_ref, gate_ref, y_ref, x1_ref, gt_ref, lg_ref, lb_ref, o_ref, acc_ref, *, cap):
    b, i, e = pl.program_id(0), pl.program_id(1), pl.program_id(2)

    @pl.when(e == 0)
    def _():
        acc_ref[...] = jnp.zeros_like(acc_ref)

    base = (b * pl.num_programs(2) + e) * 128
    lo, hi = cum_ref[base + i], cum_ref[base + i + 1]
    window = min(COMBINE_WINDOW, cap)
    align = min(128, window)
    start = pl.multiple_of(jnp.minimum((lo // align) * align, cap - window), align)

    def accumulate(first, width):
        pos = pos_ref[0, 0]
        slot = first + lax.broadcasted_iota(jnp.int32, (width, pos.shape[1]), 0)
        weights = jnp.where(pos == slot, gate_ref[0, 0], 0.0).astype(BF16)
        acc_ref[...] += _dg(weights, y_ref[0, 0, pl.ds(first, width), :], _TN)

    @pl.when((hi > lo) & (hi - start <= window))
    def _():
        accumulate(start, window)

    @pl.when(hi - start > window)
    def _():
        accumulate(0, cap)

    @pl.when(e == pl.num_programs(2) - 1)
    def _():
        z = ALPHA * x1_ref[0] + gt_ref[0] * acc_ref[...]
        o_ref[0] = _layer_norm(z) * lg_ref[...] + lb_ref[...]


def _combine(cum, pos, gate, y, x1, gt2, ln2_g, ln2_b, cap):
    B, L, D = x1.shape
    E = y.shape[1]
    tl = min(COMBINE_TILE, L)
    const = lambda b, i, e, cum: (0, 0)
    grid_spec = pltpu.PrefetchScalarGridSpec(
        num_scalar_prefetch=1,
        grid=(B, L // tl, E),
        in_specs=[pl.BlockSpec((1, 1, 1, tl), lambda b, i, e, cum: (b, e, 0, i)),
                  pl.BlockSpec((1, 1, 1, tl), lambda b, i, e, cum: (b, e, 0, i)),
                  pl.BlockSpec((1, 1, cap, D), lambda b, i, e, cum: (b, e, 0, 0)),
                  pl.BlockSpec((1, tl, D), lambda b, i, e, cum: (b, i, 0)),
                  pl.BlockSpec((1, 1, D), lambda b, i, e, cum: (b, 0, 0)),
                  pl.BlockSpec((1, D), const),
                  pl.BlockSpec((1, D), const)],
        out_specs=pl.BlockSpec((1, tl, D), lambda b, i, e, cum: (b, i, 0)),
        scratch_shapes=[pltpu.VMEM((tl, D), F32)])
    return pl.pallas_call(
        functools.partial(_combine_kernel, cap=cap),
        grid_spec=grid_spec,
        out_shape=jax.ShapeDtypeStruct((B, L, D), F32),
        compiler_params=_params(("arbitrary", "arbitrary", "arbitrary")),
        name="combine",
    )(cum, pos, gate, y, x1, gt2, ln2_g, ln2_b)


def _rope_tables(n_ctx, n_lat):
    lane = jnp.arange(128)
    j = lane % HEAD_DIM
    axis = j // (2 * ROPE_PAIRS)
    upper = (j % (2 * ROPE_PAIRS)) >= ROPE_PAIRS
    inv = ROPE_THETA ** (-(j % ROPE_PAIRS).astype(F32) / ROPE_PAIRS)
    t = jnp.arange(n_lat)
    coord = jnp.where(axis[None, :] == 0, (t // GRID_W)[:, None], (t % GRID_W)[:, None]).astype(F32)
    ang = coord * inv[None, :]
    cos, sin = jnp.cos(ang), jnp.sin(ang)
    s_lo = jnp.where(upper[None, :], 0.0, -sin)
    s_hi = jnp.where(upper[None, :], sin, 0.0)
    pad = lambda a, fill: jnp.concatenate([jnp.full((n_ctx, 128), fill, F32), a], axis=0)
    return pad(cos, 1.0), pad(s_lo, 0.0), pad(s_hi, 0.0)


def _head_ones(n):
    h = jnp.arange(n) // HEAD_DIM
    return (h[:, None] == h[None, :]).astype(BF16)


def kernel(x, c, ctx, c_ctx, w_ada, b_ada, w_in, q_gain, k_gain, tshift_mu, decay_w0, decay_up, iclr_a0,
           iclr_up, gate_up, k_k, k_a, r_k, lnx_g, lnx_b, w_out, ln1_g, ln1_b, router_w, exp_w_gate,
           exp_w_up, exp_w_down, ln2_g, ln2_b):
    B, L, D = x.shape
    n_ctx = ctx.shape[1]
    li = 0
    row = lambda a: a.reshape(1, -1)

    n_rows = -(-(B + 1) // 8) * 8
    c_rows = jnp.concatenate([c, c_ctx[None, :], jnp.zeros((n_rows - B - 1, D), F32)], axis=0)
    mod = _ada_mod(c_rows, w_ada[li], b_ada[li]).reshape(n_rows, N_MOD, 1, D)
    sh1, sc1, gt1, sh2, sc2, gt2 = (mod[:, m] for m in range(N_MOD))

    bones = _head_ones(D_RWKV)
    cos, s_lo, s_hi = _rope_tables(n_ctx, L)
    q, k, v, rw = _inproj(x, ctx, sh1, sc1, w_in[li].astype(BF16), bones,
                          row(jnp.tile(q_gain[li], ATT_HEADS)), row(jnp.tile(k_gain[li], ATT_KV_HEADS)),
                          cos, s_lo, s_hi)
    Tc = n_ctx + L
    vt = v.reshape(B, ATT_KV_HEADS, Tc // ATT_K_TILE, ATT_K_TILE, HEAD_DIM).swapaxes(-1, -2)
    vt = jnp.concatenate([vt, jnp.ones(vt.shape[:3] + (16, ATT_K_TILE), BF16)], axis=3)
    att = _attention(q, k, vt)

    zpad = jnp.zeros((2, D_LORA // 2, D_RWKV), F32)
    dup = jnp.concatenate([decay_up[li], zpad], axis=1)
    aup = jnp.concatenate([zpad, iclr_up[li]], axis=1)
    scan_shared, g, bonus, scan_dir = _rwkv_prep(
        rw, n_ctx, row(tshift_mu[li]), decay_w0[li], dup, iclr_a0[li], aup, gate_up[li],
        row(k_k[li]), row(k_a[li]), row(r_k[li]), bones)
    yf, yb = _wkv_scan(scan_shared, scan_dir, n_ctx)

    x1, h2, aff = _post(yf, yb, bonus, g, att, x, gt1, sh2, sc2, w_out[li].astype(BF16),
                        row(lnx_g[li]), row(lnx_b[li]), row(ln1_g[li]), row(ln1_b[li]),
                        router_w[li].T, bones, n_ctx)
    cap = CAPACITY_FACTOR * L // N_EXPERTS
    pos, gate, cum = _route(aff, cap, min(COMBINE_TILE, L))
    y = _experts(h2, pos, exp_w_gate[li].astype(BF16), exp_w_up[li].astype(BF16),
                 exp_w_down[li].astype(BF16), cap)
    return _combine(cum.reshape(-1), pos, gate, y, x1, gt2, row(ln2_g[li]), row(ln2_b[li]), cap)
```

```python
import functools

import jax
import jax.numpy as jnp
from jax import lax
from jax.experimental import pallas as pl
from jax.experimental.pallas import tpu as pltpu

F32 = jnp.float32
BF16 = jnp.bfloat16

HEAD_DIM = 64
ATT_HEADS = 8
ATT_KV_HEADS = 2
GQA_GROUP = ATT_HEADS // ATT_KV_HEADS
D_ATT = ATT_HEADS * HEAD_DIM
D_ATT_KV = ATT_KV_HEADS * HEAD_DIM
ATT_SCALE = HEAD_DIM ** -0.5
LOG2_E = 1.4426950408889634
EXP_M05 = 0.6065306597126334
ROPE_THETA = 10000.0
ROPE_PAIRS = HEAD_DIM // 4
GRID_W = 64
RWKV_HEAD = 64
D_RWKV = 512
D_LORA = 128
N_EXPERTS = 16
CAPACITY_FACTOR = 2
N_MOD = 6
LN_EPS = 1e-5
RMS_EPS = 1e-6
GN_EPS = 64e-5
L2_EPS = 1e-12
DEPTH = 1
ALPHA = (2.0 * DEPTH) ** 0.25

WKV_CHUNK = 64
WKV_BLOCK = 256
WKV_SAMPLES = 2
ROW_TILE = 256
POST_TILE = 512
ATT_Q_TILE = 128
ATT_K_TILE = 256
COMBINE_TILE = 1024
COMBINE_WINDOW = 256
VMEM_LIMIT = 48 * 1024 * 1024


def _params(sem):
    return pltpu.CompilerParams(dimension_semantics=sem, vmem_limit_bytes=VMEM_LIMIT)


def _split3(x):
    hi = x.astype(BF16)
    r1 = x - hi.astype(F32)
    mid = r1.astype(BF16)
    lo = (r1 - mid.astype(F32)).astype(BF16)
    return hi, mid, lo


def _split2(x):
    hi = x.astype(BF16)
    lo = (x - hi.astype(F32)).astype(BF16)
    return hi, lo


def _hi(x):
    return (x.astype(BF16),)


def _dg(a, b, dims):
    return lax.dot_general(a, b, (dims, ((), ())), preferred_element_type=F32)


_NN = ((1,), (0,))
_NT = ((1,), (1,))
_TN = ((0,), (0,))


def _mm(a, b, dims=_NN, passes=3):
    if passes == 1:
        return _dg(a.astype(BF16), b.astype(BF16), dims)
    if passes == 3:
        ah, al = _split2(a)
        bh, bl = _split2(b)
        return _dg(ah, bh, dims) + (_dg(ah, bl, dims) + _dg(al, bh, dims))
    ah, am, al = _split3(a)
    bh, bm, bl = _split3(b)
    return (_dg(ah, bh, dims) + (_dg(ah, bm, dims) + _dg(am, bh, dims))
            + (_dg(am, bm, dims) + _dg(ah, bl, dims) + _dg(al, bh, dims)))


def _pieces(x, passes):
    return (x.astype(BF16),) if passes == 1 else _split2(x)


def _mmp(a, b, dims):
    out = _dg(a[0], b[0], dims)
    if len(a) > 1 and len(b) > 1:
        return out + (_dg(a[0], b[1], dims) + _dg(a[1], b[0], dims))
    if len(a) > 1:
        return out + _dg(a[1], b[0], dims)
    if len(b) > 1:
        return out + _dg(a[0], b[1], dims)
    return out


def _mm_exact_lhs(a_bf16, b, dims=_NN):
    bh, bm, bl = _split3(b)
    return _dg(a_bf16, bh, dims) + (_dg(a_bf16, bm, dims) + _dg(a_bf16, bl, dims))


def _mm_exact_rhs(a, b_bf16, dims=_NN):
    ah, al = _split2(a)
    return _dg(ah, b_bf16, dims) + _dg(al, b_bf16, dims)


def _sigmoid(x):
    return 1.0 / (1.0 + jnp.exp(-x))


def _layer_norm(x):
    mu = jnp.mean(x, axis=-1, keepdims=True)
    xc = x - mu
    var = jnp.mean(xc * xc, axis=-1, keepdims=True)
    return xc * lax.rsqrt(var + LN_EPS)


def _ada_kernel(c_ref, w_ref, b_ref, o_ref):
    c = c_ref[...]
    o_ref[...] = _mm(c * _sigmoid(c), w_ref[...], passes=6) + b_ref[...]


def _ada_mod(c_rows, w_ada, b_ada):
    R, D = c_rows.shape
    N = w_ada.shape[1]
    tn = 1024
    return pl.pallas_call(
        _ada_kernel,
        grid=(N // tn,),
        in_specs=[pl.BlockSpec((R, D), lambda j: (0, 0)),
                  pl.BlockSpec((D, tn), lambda j: (0, j)),
                  pl.BlockSpec((1, tn), lambda j: (0, j))],
        out_specs=pl.BlockSpec((R, tn), lambda j: (0, j)),
        out_shape=jax.ShapeDtypeStruct((R, N), F32),
        compiler_params=_params(("arbitrary",)),
        name="ada_mod",
    )(c_rows, w_ada, b_ada.reshape(1, N))


def _rope(t, cos, sin_lo, sin_hi, reps):
    w = t.shape[-1]
    tile = lambda a: jnp.concatenate([a] * reps, axis=1) if reps > 1 else a
    return (t * tile(cos) + pltpu.roll(t, w - ROPE_PAIRS, 1) * tile(sin_lo)
            + pltpu.roll(t, ROPE_PAIRS, 1) * tile(sin_hi))


def _inproj_kernel(x_ref, ctx_ref, sh_ref, sc_ref, w_ref, bones_ref, qg_ref, kg_ref,
                   cos_ref, slo_ref, shi_ref, q_ref, k_ref, v_ref, rw_ref, *, n_ctx_tiles):
    i = pl.program_id(1)
    x = jnp.where(i < n_ctx_tiles, ctx_ref[0], x_ref[0])
    h = _layer_norm(x) * (1.0 + sc_ref[0]) + sh_ref[0]
    p = _dg(h.astype(BF16), w_ref[...], _NN)
    bones = bones_ref[...]
    cos, slo, shi = cos_ref[...], slo_ref[...], shi_ref[...]

    q = p[:, :D_ATT]
    ss = _mm_exact_rhs(q * q, bones)
    q = q * lax.rsqrt(ss * (1.0 / HEAD_DIM) + RMS_EPS) * qg_ref[...]
    q = _rope(q, cos, slo, shi, D_ATT // 128) * (ATT_SCALE * LOG2_E)
    q_ref[0] = q.astype(BF16)

    k = p[:, D_ATT:D_ATT + D_ATT_KV]
    ss = _mm_exact_rhs(k * k, bones[:D_ATT_KV, :D_ATT_KV])
    k = k * lax.rsqrt(ss * (1.0 / HEAD_DIM) + RMS_EPS) * kg_ref[...]
    k = _rope(k, cos, slo, shi, D_ATT_KV // 128).astype(BF16)
    v = p[:, D_ATT + D_ATT_KV:D_ATT + 2 * D_ATT_KV].astype(BF16)
    for g in range(ATT_KV_HEADS):
        k_ref[0, g] = k[:, g * HEAD_DIM:(g + 1) * HEAD_DIM]
        v_ref[0, g] = v[:, g * HEAD_DIM:(g + 1) * HEAD_DIM]
    rw_ref[0] = p[:, D_ATT + 2 * D_ATT_KV:]


def _inproj(x, ctx, shift, scale, w_in_bf16, bones, qg, kg, cos, slo, shi):
    B, L, D = x.shape
    n_ctx = ctx.shape[1]
    tm = ROW_TILE
    nct = n_ctx // tm
    Tc = n_ctx + L
    d_in = w_in_bf16.shape[1]
    d_rw = d_in - D_ATT - 2 * D_ATT_KV
    mod_row = lambda b, i: (jnp.where(i < nct, B, b), 0, 0)
    const = lambda b, i: (0, 0)
    return pl.pallas_call(
        functools.partial(_inproj_kernel, n_ctx_tiles=nct),
        grid=(B, Tc // tm),
        in_specs=[pl.BlockSpec((1, tm, D), lambda b, i: (b, jnp.maximum(i - nct, 0), 0)),
                  pl.BlockSpec((1, tm, D), lambda b, i: (b, jnp.minimum(i, nct - 1), 0)),
                  pl.BlockSpec((1, 1, D), mod_row),
                  pl.BlockSpec((1, 1, D), mod_row),
                  pl.BlockSpec((D, d_in), const),
                  pl.BlockSpec(bones.shape, const),
                  pl.BlockSpec((1, D_ATT), const),
                  pl.BlockSpec((1, D_ATT_KV), const),
                  pl.BlockSpec((tm, 128), lambda b, i: (i, 0)),
                  pl.BlockSpec((tm, 128), lambda b, i: (i, 0)),
                  pl.BlockSpec((tm, 128), lambda b, i: (i, 0))],
        out_specs=[pl.BlockSpec((1, tm, D_ATT), lambda b, i: (b, jnp.maximum(i - nct, 0), 0)),
                   pl.BlockSpec((1, ATT_KV_HEADS, tm, HEAD_DIM), lambda b, i: (b, 0, i, 0)),
                   pl.BlockSpec((1, ATT_KV_HEADS, tm, HEAD_DIM), lambda b, i: (b, 0, i, 0)),
                   pl.BlockSpec((1, tm, d_rw), lambda b, i: (b, i, 0))],
        out_shape=[jax.ShapeDtypeStruct((B, L, D_ATT), BF16),
                   jax.ShapeDtypeStruct((B, ATT_KV_HEADS, Tc, HEAD_DIM), BF16),
                   jax.ShapeDtypeStruct((B, ATT_KV_HEADS, Tc, HEAD_DIM), BF16),
                   jax.ShapeDtypeStruct((B, Tc, d_rw), F32)],
        compiler_params=_params(("arbitrary", "arbitrary")),
        name="inproj",
    )(x, ctx, shift, scale, w_in_bf16, bones, qg, kg, cos, slo, shi)


def _attn_kernel(q_ref, k_ref, vt_ref, o_ref, s0, s1, p0, p1, m0, m1):
    i = pl.program_id(0)
    nk, tk, cols = s0.shape

    @pl.when(i == 0)
    def _():
        for ref in (s0, s1, m0, m1):
            ref[...] = jnp.zeros_like(ref)
        p0[...] = jnp.ones_like(p0)
        p1[...] = jnp.ones_like(p1)

    def step(s_w, m_w, s_r, m_r, p_w, p_r):
        q = q_ref[0]
        tq = q.shape[0]
        qs = jnp.concatenate([q[:, h * HEAD_DIM:(h + 1) * HEAD_DIM] for h in range(GQA_GROUP)], axis=0)
        m_prev = m_r[0:1, :]

        def body(j, carry):
            m_run, acc = carry
            off = pl.multiple_of(j * tk, tk)
            s_new = _dg(k_ref[0, 0, pl.ds(off, tk), :], qs, _NT)
            s_w[j] = s_new
            m_run = jnp.maximum(m_run, jnp.max(s_new.reshape(tk // 8, 8, cols), axis=0))
            p_w[j] = jnp.exp2(s_r[j] - m_prev).astype(BF16)
            acc = acc + _dg(vt_ref[0, 0, j], p_r[j], _NN)
            return m_run, acc

        init = (jnp.full((8, cols), -jnp.inf, F32), jnp.zeros((vt_ref.shape[3], cols), F32))
        m_run, acc = lax.fori_loop(0, nk, body, init, unroll=8)
        m_w[...] = jnp.broadcast_to(jnp.max(m_run, axis=0, keepdims=True), m_w.shape)
        o = acc[:HEAD_DIM] / acc[HEAD_DIM:HEAD_DIM + 1]
        o_ref[0] = jnp.concatenate([o[:, h * tq:(h + 1) * tq].T for h in range(GQA_GROUP)],
                                   axis=1).astype(BF16)

    @pl.when(i % 2 == 0)
    def _():
        step(s0, m0, s1, m1, p1, p0)

    @pl.when(i % 2 == 1)
    def _():
        step(s1, m1, s0, m0, p0, p1)


def _attention(q, k, vt):
    B, L, _ = q.shape
    Tc = k.shape[2]
    tq = ATT_Q_TILE
    n = L // tq
    gw = GQA_GROUP * HEAD_DIM
    cols = GQA_GROUP * tq
    tk = ATT_K_TILE
    nk = Tc // tk
    n_tiles = B * ATT_KV_HEADS * n

    def tile(s):
        t = jnp.clip(s, 0, n_tiles - 1)
        return t // (ATT_KV_HEADS * n), (t // n) % ATT_KV_HEADS, t % n

    def q_map(s):
        b, g, i = tile(s)
        return b, i, g

    def out_map(s):
        b, g, i = tile(s - 2)
        return b, i, g

    return pl.pallas_call(
        _attn_kernel,
        grid=(n_tiles + 2,),
        in_specs=[pl.BlockSpec((1, tq, gw), q_map),
                  pl.BlockSpec((1, 1, Tc, HEAD_DIM), lambda s: tile(s)[:2] + (0, 0)),
                  pl.BlockSpec((1, 1, nk, vt.shape[3], tk), lambda s: tile(s - 2)[:2] + (0, 0, 0))],
        out_specs=pl.BlockSpec((1, tq, gw), out_map),
        out_shape=jax.ShapeDtypeStruct((B, L, D_ATT), BF16),
        scratch_shapes=[pltpu.VMEM((nk, tk, cols), F32), pltpu.VMEM((nk, tk, cols), F32),
                        pltpu.VMEM((nk, tk, cols), BF16), pltpu.VMEM((nk, tk, cols), BF16),
                        pltpu.VMEM((8, cols), F32), pltpu.VMEM((8, cols), F32)],
        compiler_params=_params(("arbitrary",)),
        name="attention",
    )(q, k, vt)


def _rwkv_prep_kernel(rw_ref, prev_ref, next_ref, mu_ref, w0_ref, dup_ref, a0_ref, aup_ref, gup_ref,
                      kk_ref, ka_ref, rk_ref, bones_ref,
                      sh_o, g_o, bonus_o, dir_o, *, n_ctx, n_tok):
    i = pl.program_id(1)
    p = rw_ref[0]
    tm = p.shape[0]
    t0 = i * tm
    has_prev = jnp.where((t0 == 0) | (t0 == n_ctx), 0.0, 1.0)
    has_next = jnp.where((t0 + tm == n_ctx) | (t0 + tm == n_tok), 0.0, 1.0)
    row = lax.broadcasted_iota(jnp.int32, (tm, 1), 0)
    up = jnp.where(row == 0, prev_ref[0, 7:8, :] * has_prev, pltpu.roll(p, 1, 0))
    dn = jnp.where(row == tm - 1, next_ref[0, 0:1, :] * has_next, pltpu.roll(p, tm - 1, 0))
    s = p + (0.5 * (up + dn) - p) * mu_ref[...]

    D = D_RWKV
    r, k, v = s[:, :D], s[:, D:2 * D], s[:, 2 * D:3 * D]
    lora = s[:, 3 * D:3 * D + D_LORA]
    gl = s[:, 3 * D + D_LORA:]
    bones = bones_ref[...]

    g_o[0] = _mm(_sigmoid(gl), gup_ref[...], passes=3)
    kk = k * kk_ref[...]
    nrm = jnp.sqrt(_mm_exact_rhs(kk * kk, bones))
    kk = kk / jnp.maximum(nrm, L2_EPS)
    wt = jnp.tanh(lora)
    sh_o[0, :, 0:D] = r
    sh_o[0, :, D:2 * D] = v
    sh_o[0, :, 2 * D:3 * D] = kk
    bonus_o[0] = _mm_exact_rhs(r * k * rk_ref[...], bones) * v
    for d in range(2):
        z = w0_ref[d:d + 1, :] + _mm(wt, dup_ref[d], passes=3)
        dir_o[d, 0, :, 0:D] = -EXP_M05 * _sigmoid(z)
        a = _sigmoid(a0_ref[d:d + 1, :] + _mm(lora, aup_ref[d], passes=3))
        dir_o[d, 0, :, D:2 * D] = k * (1.0 + (a - 1.0) * ka_ref[...])
        dir_o[d, 0, :, 2 * D:3 * D] = kk * a


def _rwkv_prep(rw, n_ctx, mu, w0, dup, a0, aup, gup, k_k, k_a, r_k, bones):
    B, Tc, d_rw = rw.shape
    tm = ROW_TILE
    D = D_RWKV
    const2 = lambda b, i: (0, 0)
    const3 = lambda b, i: (0, 0, 0)
    tok = lambda n: pl.BlockSpec((1, tm, n * D), lambda b, i: (b, i, 0))
    tok2 = pl.BlockSpec((2, 1, tm, 3 * D), lambda b, i: (0, b, i, 0))
    s1 = lambda n: jax.ShapeDtypeStruct((B, Tc, n * D), F32)
    s2 = jax.ShapeDtypeStruct((2, B, Tc, 3 * D), F32)
    hb = tm // 8
    return pl.pallas_call(
        functools.partial(_rwkv_prep_kernel, n_ctx=n_ctx, n_tok=Tc),
        grid=(B, Tc // tm),
        in_specs=[pl.BlockSpec((1, tm, d_rw), lambda b, i: (b, i, 0)),
                  pl.BlockSpec((1, 8, d_rw), lambda b, i: (b, jnp.maximum(i * hb - 1, 0), 0)),
                  pl.BlockSpec((1, 8, d_rw), lambda b, i: (b, jnp.minimum((i + 1) * hb, Tc // 8 - 1), 0)),
                  pl.BlockSpec((1, d_rw), const2),
                  pl.BlockSpec((2, D), const2),
                  pl.BlockSpec((2, D_LORA, D), const3),
                  pl.BlockSpec((2, D), const2),
                  pl.BlockSpec((2, D_LORA, D), const3),
                  pl.BlockSpec((D_LORA, D), const2),
                  pl.BlockSpec((1, D), const2),
                  pl.BlockSpec((1, D), const2),
                  pl.BlockSpec((1, D), const2),
                  pl.BlockSpec(bones.shape, const2)],
        out_specs=[tok(3), tok(1), tok(1), tok2],
        out_shape=[s1(3), s1(1), s1(1), s2],
        compiler_params=_params(("arbitrary", "arbitrary")),
        name="rwkv_prep",
    )(rw, rw, rw, mu, w0, dup, a0, aup, gup, k_k, k_a, r_k, bones)


def _wkv_kernel(sh0, dir0, sh1, dir1, y0, y1, s_ref, *, n_heads):
    @pl.when(pl.program_id(1) == 0)
    def _():
        s_ref[...] = jnp.zeros_like(s_ref)

    C = WKV_CHUNK
    n_sub = sh0.shape[1] // C
    lax.fori_loop(0, n_sub, functools.partial(_wkv_chunk, refs=(sh0, dir0, sh1, dir1, y0, y1, s_ref),
                                              n_heads=n_heads, n_sub=n_sub), 0, unroll=2)


def _wkv_chunk(sub, carry, *, refs, n_heads, n_sub):
    sh0, dir0, sh1, dir1, y0, y1, s_ref = refs
    C = WKV_CHUNK
    N = RWKV_HEAD
    D = n_heads * N
    row = lax.broadcasted_iota(jnp.int32, (C, 2 * N), 0)
    lane = lax.broadcasted_iota(jnp.int32, (C, 2 * N), 1)
    col = lane % N
    first_half = lane < N
    trow = lax.broadcasted_iota(jnp.int32, (C, C), 0)
    tcol = lax.broadcasted_iota(jnp.int32, (C, C), 1)
    cat = lambda xs, ys: tuple(jnp.concatenate([x, y], axis=0) for x, y in zip(xs, ys))
    offs = (pl.multiple_of(sub * C, C), pl.multiple_of((n_sub - 1 - sub) * C, C))
    y_refs = (y0, y1)

    chains = []
    for nb, d in [(nb, d) for nb in range(sh0.shape[0]) for d in range(2)]:
        sh_ref, dir_ref = ((sh0, dir0), (sh1, dir1))[d]
        rows = pl.ds(offs[d], C)
        strict, incl = (row < col, row <= col) if d else (row > col, row >= col)
        tri = (trow <= tcol) if d else (trow >= tcol)
        lw = dir_ref[0, nb, rows, 0:D]
        cl = _mm_exact_lhs(jnp.where(tri, 1.0, 0.0).astype(BF16), lw)
        ecl = jnp.exp(cl)
        eneg = jnp.exp(-cl)
        last = 0 if d else C - 1
        etot = ecl[last:last + 1, :]
        kt = dir_ref[0, nb, rows, D:2 * D] * eneg
        bt = dir_ref[0, nb, rows, 2 * D:3 * D] * eneg
        full = dict(at=_split2(-(sh_ref[nb, rows, 2 * D:3 * D] * jnp.exp(cl - lw))),
                    rt=_split2(sh_ref[nb, rows, 0:D] * ecl),
                    v=_split2(sh_ref[nb, rows, D:2 * D]),
                    kt=_hi(kt), bt=_hi(bt), kh=_hi(kt * etot), bh=_hi(bt * etot))
        for h in range(n_heads):
            ch = {k: tuple(p[:, h * N:(h + 1) * N] for p in val) for k, val in full.items()}
            ch.update(nb=nb, d=d, h=h, strict=strict, incl=incl, etot=etot[:, h * N:(h + 1) * N],
                      s0=s_ref[nb, d, h])
            chains.append(ch)

    for ch in chains:
        rhs = cat(cat(ch["kt"], ch["bt"]), _hi(ch["s0"]))
        g = _mmp(cat(ch["at"], ch["rt"]), rhs, _NT)
        ch["top"] = jnp.where(ch["strict"], g[:C, :2 * N], 0.0)
        ch["bot"] = jnp.where(ch["incl"], g[C:, :2 * N], 0.0)
        ch["as0"], ch["rs0"] = g[:C, 2 * N:], g[C:, 2 * N:]
    for ch in chains:
        z = ch["as0"] + _mmp(_split2(ch["top"][:, :N]), ch["v"][:1], _NN)
        ch["w"] = jnp.where(first_half, jnp.concatenate([z, jnp.zeros_like(z)], axis=1), ch["top"])
        ch["pw"] = pltpu.roll(ch["top"], N, 1)[:, :N]
    n = 1
    while True:
        for ch in chains:
            ch["prod"] = _mmp(_split2(ch["pw"]), _split2(ch["w"]), _NN)
            ch["w"] = jnp.where(first_half, ch["w"], 0.0) + ch["prod"]
        n *= 2
        if n >= C:
            break
        for ch in chains:
            ch["pw"] = pltpu.roll(ch["prod"], N, 1)[:, :N]
    for ch in chains:
        ch["up"] = _split2(ch["w"][:, :N])
        y = ch["rs0"] + _mmp(_split2(ch["bot"]), cat(ch["v"][:1], ch["up"][:1]), _NN)
        y_refs[ch["d"]][ch["nb"], pl.ds(offs[ch["d"]], C), ch["h"] * N:(ch["h"] + 1) * N] = y
    for ch in chains:
        upd = _mmp(cat(ch["up"], ch["v"]), cat(ch["bh"], ch["kh"]), _TN)
        s_ref[ch["nb"], ch["d"], ch["h"]] = ch["s0"] * ch["etot"] + upd
    return carry


def _wkv_scan(shared, per_dir, n_ctx):
    B, T, D3 = shared.shape
    D = D3 // 3
    tb = WKV_BLOCK
    nb = WKV_SAMPLES if B % WKV_SAMPLES == 0 else 1
    n_blocks = T // tb
    n_cb = n_ctx // tb
    tok = (lambda c: c,
           lambda c: jnp.where(c < n_cb, n_cb - 1 - c, n_blocks - 1 + n_cb - c))
    in_specs, args = [], []
    for d in range(2):
        in_specs += [pl.BlockSpec((nb, tb, D3), lambda bi, c, d=d: (bi, tok[d](c), 0)),
                     pl.BlockSpec((1, nb, tb, D3), lambda bi, c, d=d: (d, bi, tok[d](c), 0))]
        args += [shared, per_dir]
    y = jax.ShapeDtypeStruct((B, T, D), F32)
    return pl.pallas_call(
        functools.partial(_wkv_kernel, n_heads=D // RWKV_HEAD),
        grid=(B // nb, n_blocks),
        in_specs=in_specs,
        out_specs=[pl.BlockSpec((nb, tb, D), lambda bi, c, d=d: (bi, tok[d](c), 0)) for d in range(2)],
        out_shape=[y, y],
        scratch_shapes=[pltpu.VMEM((nb, 2, D // RWKV_HEAD, RWKV_HEAD, RWKV_HEAD), F32)],
        compiler_params=_params(("arbitrary", "arbitrary")),
        name="wkv_scan",
    )(*args)


def _post_kernel(*refs, k):
    rows = lambda j: jnp.concatenate([r[0] for r in refs[j * k:(j + 1) * k]], axis=0)
    (att_ref, x_ref, gt_ref, sh_ref, sc_ref, wo_ref, lxg_ref, lxb_ref, l1g_ref, l1b_ref, rwt_ref, bones_ref,
     x1_ref, h2_ref, aff_ref) = refs[4 * k:]
    bones = bones_ref[...]
    y = rows(0) + rows(1)
    inv = 1.0 / RWKV_HEAD
    mu = _mm_exact_rhs(y, bones) * inv
    yc = y - mu
    var = _mm_exact_rhs(yc * yc, bones) * inv
    yn = yc * lax.rsqrt(var + GN_EPS) * lxg_ref[...] + lxb_ref[...]
    rw_out = (yn + rows(2)) * rows(3)
    mix = _dg(att_ref[0], wo_ref[:D_ATT, :], _NN) + _dg(rw_out.astype(BF16), wo_ref[D_ATT:, :], _NN)
    x1 = _layer_norm(ALPHA * x_ref[0] + gt_ref[0] * mix) * l1g_ref[...] + l1b_ref[...]
    x1_ref[0] = x1
    h2 = _layer_norm(x1) * (1.0 + sc_ref[0]) + sh_ref[0]
    h2_ref[0] = h2.astype(BF16)
    logits = _mm(rwt_ref[...], h2, _NT, passes=3)
    e = jnp.exp(logits - jnp.max(logits, axis=0, keepdims=True))
    aff_ref[0] = e / jnp.sum(e, axis=0, keepdims=True)


def _post(yf, yb, bonus, g, att, x, gt1, sh2, sc2, w_out_bf16, lnx_g, lnx_b, ln1_g, ln1_b, router_wt,
          bones, n_ctx):
    B, L, D = x.shape
    tm = min(POST_TILE, L)
    k = tm // ROW_TILE
    nct = n_ctx // ROW_TILE
    E = router_wt.shape[0]
    const = lambda b, i: (0, 0)
    cat = [pl.BlockSpec((1, ROW_TILE, D_RWKV), lambda b, i, j=j: (b, i * k + j + nct, 0)) for j in range(k)]
    mod = pl.BlockSpec((1, 1, D), lambda b, i: (b, 0, 0))
    vec = lambda n: pl.BlockSpec((1, n), const)
    return pl.pallas_call(
        functools.partial(_post_kernel, k=k),
        grid=(B, L // tm),
        in_specs=cat * 4 + [
                  pl.BlockSpec((1, tm, D_ATT), lambda b, i: (b, i, 0)),
                  pl.BlockSpec((1, tm, D), lambda b, i: (b, i, 0)),
                  mod, mod, mod,
                  pl.BlockSpec(w_out_bf16.shape, const),
                  vec(D_RWKV), vec(D_RWKV), vec(D), vec(D),
                  pl.BlockSpec((E, D), const),
                  pl.BlockSpec(bones.shape, const)],
        out_specs=[pl.BlockSpec((1, tm, D), lambda b, i: (b, i, 0)),
                   pl.BlockSpec((1, tm, D), lambda b, i: (b, i, 0)),
                   pl.BlockSpec((1, E, tm), lambda b, i: (b, 0, i))],
        out_shape=[jax.ShapeDtypeStruct((B, L, D), F32),
                   jax.ShapeDtypeStruct((B, L, D), BF16),
                   jax.ShapeDtypeStruct((B, E, L), F32)],
        compiler_params=_params(("arbitrary", "arbitrary")),
        name="post_mix",
    )(*([yf] * k + [yb] * k + [bonus] * k + [g] * k), att, x, gt1, sh2, sc2, w_out_bf16, lnx_g, lnx_b,
      ln1_g, ln1_b, router_wt, bones)


def _cumsum_lanes(x):
    n = x.shape[1]
    lane = lax.broadcasted_iota(jnp.int32, x.shape, 1)
    s = 1
    while s < n:
        x = x + jnp.where(lane >= s, pltpu.roll(x, s, 1), 0)
        s *= 2
    return x


def _route_kernel(aff_ref, pos_ref, gate_ref, cum_ref, *, cap, tile):
    aff = aff_ref[0]
    E = aff.shape[0]
    count = lambda m: jnp.sum(jnp.where(m, 1, 0), axis=1, keepdims=True)
    thr_bits = jnp.zeros((E, 1), jnp.int32)
    for bit in range(29, -1, -1):
        cand = thr_bits | (1 << bit)
        cand_f = lax.bitcast_convert_type(cand, F32)
        thr_bits = jnp.where(count(aff >= cand_f) >= cap, cand, thr_bits)
    thr = lax.bitcast_convert_type(thr_bits, F32)
    above = aff > thr
    tie = aff == thr
    need = cap - count(above)
    tie_rank = _cumsum_lanes(jnp.where(tie, 1, 0))
    sel = above | (tie & (tie_rank <= need))
    slot = _cumsum_lanes(jnp.where(sel, 1, 0)) - 1
    pos = jnp.where(sel, slot, -1)
    gate = jnp.where(sel, aff, 0.0)
    for e in range(E):
        pos_ref[0, e] = pos[e:e + 1, :]
        gate_ref[0, e] = gate[e:e + 1, :]
    lane = lax.broadcasted_iota(jnp.int32, (E, 128), 1)
    chosen = jnp.where(sel, 1, 0)
    run = jnp.zeros((E, 1), jnp.int32)
    cum = jnp.zeros((E, 128), jnp.int32)
    n_tiles = aff.shape[1] // tile
    for i in range(n_tiles):
        cum = jnp.where(lane == i, run, cum)
        run = run + jnp.sum(chosen[:, i * tile:(i + 1) * tile], axis=1, keepdims=True)
    cum_ref[0] = jnp.where(lane == n_tiles, run, cum)


def _route(aff, cap, tile):
    B, E, L = aff.shape
    out = pl.BlockSpec((1, E, 1, L), lambda b: (b, 0, 0, 0))
    return pl.pallas_call(
        functools.partial(_route_kernel, cap=cap, tile=tile),
        grid=(B,),
        in_specs=[pl.BlockSpec((1, E, L), lambda b: (b, 0, 0))],
        out_specs=[out, out, pl.BlockSpec((1, E, 128), lambda b: (b, 0, 0))],
        out_shape=[jax.ShapeDtypeStruct((B, E, 1, L), jnp.int32),
                   jax.ShapeDtypeStruct((B, E, 1, L), F32),
                   jax.ShapeDtypeStruct((B, E, 128), jnp.int32)],
        compiler_params=_params(("arbitrary",)),
        name="route",
    )(aff)


def _expert_kernel(h_ref, pos_ref, wg_ref, wu_ref, wd_ref, y_ref, *, cap):
    pos = pos_ref[0, 0]
    slot = lax.broadcasted_iota(jnp.int32, (cap, pos.shape[1]), 0)
    onehot = jnp.where(pos == slot, 1.0, 0.0).astype(BF16)
    xin = _dg(onehot, h_ref[0], _NN).astype(BF16)
    hg = _dg(xin, wg_ref[0], _NN)
    hu = _dg(xin, wu_ref[0], _NN)
    hid = (hg * _sigmoid(hg) * hu).astype(BF16)
    y_ref[0, 0] = _dg(hid, wd_ref[0], _NN).astype(BF16)


def _experts(h2, pos, wg, wu, wd, cap):
    B, L, D = h2.shape
    E, _, F = wg.shape
    rowv = pl.BlockSpec((1, 1, 1, L), lambda b, e: (b, e, 0, 0))
    return pl.pallas_call(
        functools.partial(_expert_kernel, cap=cap),
        grid=(B, E),
        in_specs=[pl.BlockSpec((1, L, D), lambda b, e: (b, 0, 0)),
                  rowv,
                  pl.BlockSpec((1, D, F), lambda b, e: (e, 0, 0)),
                  pl.BlockSpec((1, D, F), lambda b, e: (e, 0, 0)),
                  pl.BlockSpec((1, F, D), lambda b, e: (e, 0, 0))],
        out_specs=pl.BlockSpec((1, 1, cap, D), lambda b, e: (b, e, 0, 0)),
        out_shape=jax.ShapeDtypeStruct((B, E, cap, D), BF16),
        compiler_params=_params(("arbitrary", "arbitrary")),
        name="experts",
    )(h2, pos, wg, wu, wd)


def _combine_kernel(cum_ref, pos_ref, gate_ref, y_ref, x1_ref, gt_ref, lg_ref, lb_ref, o_ref, acc_ref, *, cap):
    b, i, e = pl.program_id(0), pl.program_id(1), pl.program_id(2)

    @pl.when(e == 0)
    def _():
        acc_ref[...] = jnp.zeros_like(acc_ref)

    base = (b * pl.num_programs(2) + e) * 128
    lo, hi = cum_ref[base + i], cum_ref[base + i + 1]
    window = min(COMBINE_WINDOW, cap)
    align = min(128, window)
    start = pl.multiple_of(jnp.minimum((lo // align) * align, cap - window), align)

    def accumulate(first, width):
        pos = pos_ref[0, 0]
        slot = first + lax.broadcasted_iota(jnp.int32, (width, pos.shape[1]), 0)
        weights = jnp.where(pos == slot, gate_ref[0, 0], 0.0).astype(BF16)
        acc_ref[...] += _dg(weights, y_ref[0, 0, pl.ds(first, width), :], _TN)

    @pl.when((hi > lo) & (hi - start <= window))
    def _():
        accumulate(start, window)

    @pl.when(hi - start > window)
    def _():
        accumulate(0, cap)

    @pl.when(e == pl.num_programs(2) - 1)
    def _():
        z = ALPHA * x1_ref[0] + gt_ref[0] * acc_ref[...]
        o_ref[0] = _layer_norm(z) * lg_ref[...] + lb_ref[...]


def _combine(cum, pos, gate, y, x1, gt2, ln2_g, ln2_b, cap):
    B, L, D = x1.shape
    E = y.shape[1]
    tl = min(COMBINE_TILE, L)
    const = lambda b, i, e, cum: (0, 0)
    grid_spec = pltpu.PrefetchScalarGridSpec(
        num_scalar_prefetch=1,
        grid=(B, L // tl, E),
        in_specs=[pl.BlockSpec((1, 1, 1, tl), lambda b, i, e, cum: (b, e, 0, i)),
                  pl.BlockSpec((1, 1, 1, tl), lambda b, i, e, cum: (b, e, 0, i)),
                  pl.BlockSpec((1, 1, cap, D), lambda b, i, e, cum: (b, e, 0, 0)),
                  pl.BlockSpec((1, tl, D), lambda b, i, e, cum: (b, i, 0)),
                  pl.BlockSpec((1, 1, D), lambda b, i, e, cum: (b, 0, 0)),
                  pl.BlockSpec((1, D), const),
                  pl.BlockSpec((1, D), const)],
        out_specs=pl.BlockSpec((1, tl, D), lambda b, i, e, cum: (b, i, 0)),
        scratch_shapes=[pltpu.VMEM((tl, D), F32)])
    return pl.pallas_call(
        functools.partial(_combine_kernel, cap=cap),
        grid_spec=grid_spec,
        out_shape=jax.ShapeDtypeStruct((B, L, D), F32),
        compiler_params=_params(("arbitrary", "arbitrary", "arbitrary")),
        name="combine",
    )(cum, pos, gate, y, x1, gt2, ln2_g, ln2_b)


def _rope_tables(n_ctx, n_lat):
    lane = jnp.arange(128)
    j = lane % HEAD_DIM
    axis = j // (2 * ROPE_PAIRS)
    upper = (j % (2 * ROPE_PAIRS)) >= ROPE_PAIRS
    inv = ROPE_THETA ** (-(j % ROPE_PAIRS).astype(F32) / ROPE_PAIRS)
    t = jnp.arange(n_lat)
    coord = jnp.where(axis[None, :] == 0, (t // GRID_W)[:, None], (t % GRID_W)[:, None]).astype(F32)
    ang = coord * inv[None, :]
    cos, sin = jnp.cos(ang), jnp.sin(ang)
    s_lo = jnp.where(upper[None, :], 0.0, -sin)
    s_hi = jnp.where(upper[None, :], sin, 0.0)
    pad = lambda a, fill: jnp.concatenate([jnp.full((n_ctx, 128), fill, F32), a], axis=0)
    return pad(cos, 1.0), pad(s_lo, 0.0), pad(s_hi, 0.0)


def _head_ones(n):
    h = jnp.arange(n) // HEAD_DIM
    return (h[:, None] == h[None, :]).astype(BF16)


def kernel(x, c, ctx, c_ctx, w_ada, b_ada, w_in, q_gain, k_gain, tshift_mu, decay_w0, decay_up, iclr_a0,
           iclr_up, gate_up, k_k, k_a, r_k, lnx_g, lnx_b, w_out, ln1_g, ln1_b, router_w, exp_w_gate,
           exp_w_up, exp_w_down, ln2_g, ln2_b):
    B, L, D = x.shape
    n_ctx = ctx.shape[1]
    li = 0
    row = lambda a: a.reshape(1, -1)

    n_rows = -(-(B + 1) // 8) * 8
    c_rows = jnp.concatenate([c, c_ctx[None, :], jnp.zeros((n_rows - B - 1, D), F32)], axis=0)
    mod = _ada_mod(c_rows, w_ada[li], b_ada[li]).reshape(n_rows, N_MOD, 1, D)
    sh1, sc1, gt1, sh2, sc2, gt2 = (mod[:, m] for m in range(N_MOD))

    bones = _head_ones(D_RWKV)
    cos, s_lo, s_hi = _rope_tables(n_ctx, L)
    q, k, v, rw = _inproj(x, ctx, sh1, sc1, w_in[li].astype(BF16), bones,
                          row(jnp.tile(q_gain[li], ATT_HEADS)), row(jnp.tile(k_gain[li], ATT_KV_HEADS)),
                          cos, s_lo, s_hi)
    Tc = n_ctx + L
    vt = v.reshape(B, ATT_KV_HEADS, Tc // ATT_K_TILE, ATT_K_TILE, HEAD_DIM).swapaxes(-1, -2)
    vt = jnp.concatenate([vt, jnp.ones(vt.shape[:3] + (16, ATT_K_TILE), BF16)], axis=3)
    att = _attention(q, k, vt)

    zpad = jnp.zeros((2, D_LORA // 2, D_RWKV), F32)
    dup = jnp.concatenate([decay_up[li], zpad], axis=1)
    aup = jnp.concatenate([zpad, iclr_up[li]], axis=1)
    scan_shared, g, bonus, scan_dir = _rwkv_prep(
        rw, n_ctx, row(tshift_mu[li]), decay_w0[li], dup, iclr_a0[li], aup, gate_up[li],
        row(k_k[li]), row(k_a[li]), row(r_k[li]), bones)
    yf, yb = _wkv_scan(scan_shared, scan_dir, n_ctx)

    x1, h2, aff = _post(yf, yb, bonus, g, att, x, gt1, sh2, sc2, w_out[li].astype(BF16),
                        row(lnx_g[li]), row(lnx_b[li]), row(ln1_g[li]), row(ln1_b[li]),
                        router_w[li].T, bones, n_ctx)
    cap = CAPACITY_FACTOR * L // N_EXPERTS
    pos, gate, cum = _route(aff, cap, min(COMBINE_TILE, L))
    y = _experts(h2, pos, exp_w_gate[li].astype(BF16), exp_w_up[li].astype(BF16),
                 exp_w_down[li].astype(BF16), cap)
    return _combine(cum.reshape(-1), pos, gate, y, x1, gt2, row(ln2_g[li]), row(ln2_b[li]), cap)
```

```python
import functools

import jax
import jax.numpy as jnp
from jax import lax
from jax.experimental import pallas as pl
from jax.experimental.pallas import tpu as pltpu

F32 = jnp.float32
BF16 = jnp.bfloat16

HEAD_DIM = 64
ATT_HEADS = 8
ATT_KV_HEADS = 2
GQA_GROUP = ATT_HEADS // ATT_KV_HEADS
D_ATT = ATT_HEADS * HEAD_DIM
D_ATT_KV = ATT_KV_HEADS * HEAD_DIM
ATT_SCALE = HEAD_DIM ** -0.5
LOG2_E = 1.4426950408889634
EXP_M05 = 0.6065306597126334
ROPE_THETA = 10000.0
ROPE_PAIRS = HEAD_DIM // 4
GRID_W = 64
RWKV_HEAD = 64
D_RWKV = 512
D_LORA = 128
N_EXPERTS = 16
CAPACITY_FACTOR = 2
N_MOD = 6
LN_EPS = 1e-5
RMS_EPS = 1e-6
GN_EPS = 64e-5
L2_EPS = 1e-12
DEPTH = 1
ALPHA = (2.0 * DEPTH) ** 0.25

WKV_CHUNK = 64
WKV_BLOCK = 256
WKV_SAMPLES = 2
ROW_TILE = 256
POST_TILE = 512
ATT_Q_TILE = 128
ATT_K_TILE = 256
COMBINE_TILE = 1024
COMBINE_WINDOW = 256
COMBINE_EXPERTS = 4
VMEM_LIMIT = 48 * 1024 * 1024


def _params(sem):
    return pltpu.CompilerParams(dimension_semantics=sem, vmem_limit_bytes=VMEM_LIMIT)


def _split3(x):
    hi = x.astype(BF16)
    r1 = x - hi.astype(F32)
    mid = r1.astype(BF16)
    lo = (r1 - mid.astype(F32)).astype(BF16)
    return hi, mid, lo


def _split2(x):
    hi = x.astype(BF16)
    lo = (x - hi.astype(F32)).astype(BF16)
    return hi, lo


def _hi(x):
    return (x.astype(BF16),)


def _dg(a, b, dims):
    return lax.dot_general(a, b, (dims, ((), ())), preferred_element_type=F32)


_NN = ((1,), (0,))
_NT = ((1,), (1,))
_TN = ((0,), (0,))


def _mm(a, b, dims=_NN, passes=3):
    if passes == 1:
        return _dg(a.astype(BF16), b.astype(BF16), dims)
    if passes == 3:
        ah, al = _split2(a)
        bh, bl = _split2(b)
        return _dg(ah, bh, dims) + (_dg(ah, bl, dims) + _dg(al, bh, dims))
    ah, am, al = _split3(a)
    bh, bm, bl = _split3(b)
    return (_dg(ah, bh, dims) + (_dg(ah, bm, dims) + _dg(am, bh, dims))
            + (_dg(am, bm, dims) + _dg(ah, bl, dims) + _dg(al, bh, dims)))


def _pieces(x, passes):
    return (x.astype(BF16),) if passes == 1 else _split2(x)


def _mmp(a, b, dims):
    out = _dg(a[0], b[0], dims)
    if len(a) > 1 and len(b) > 1:
        return out + (_dg(a[0], b[1], dims) + _dg(a[1], b[0], dims))
    if len(a) > 1:
        return out + _dg(a[1], b[0], dims)
    if len(b) > 1:
        return out + _dg(a[0], b[1], dims)
    return out


def _mm_exact_lhs(a_bf16, b, dims=_NN):
    bh, bm, bl = _split3(b)
    return _dg(a_bf16, bh, dims) + (_dg(a_bf16, bm, dims) + _dg(a_bf16, bl, dims))


def _mm_exact_rhs(a, b_bf16, dims=_NN):
    ah, al = _split2(a)
    return _dg(ah, b_bf16, dims) + _dg(al, b_bf16, dims)


def _sigmoid(x):
    return 1.0 / (1.0 + jnp.exp(-x))


def _layer_norm(x):
    mu = jnp.mean(x, axis=-1, keepdims=True)
    xc = x - mu
    var = jnp.mean(xc * xc, axis=-1, keepdims=True)
    return xc * lax.rsqrt(var + LN_EPS)


def _ada_kernel(c_ref, w_ref, b_ref, o_ref):
    c = c_ref[...]
    o_ref[...] = _mm(c * _sigmoid(c), w_ref[...], passes=6) + b_ref[...]


def _ada_mod(c_rows, w_ada, b_ada):
    R, D = c_rows.shape
    N = w_ada.shape[1]
    tn = 1024
    return pl.pallas_call(
        _ada_kernel,
        grid=(N // tn,),
        in_specs=[pl.BlockSpec((R, D), lambda j: (0, 0)),
                  pl.BlockSpec((D, tn), lambda j: (0, j)),
                  pl.BlockSpec((1, tn), lambda j: (0, j))],
        out_specs=pl.BlockSpec((R, tn), lambda j: (0, j)),
        out_shape=jax.ShapeDtypeStruct((R, N), F32),
        compiler_params=_params(("arbitrary",)),
        name="ada_mod",
    )(c_rows, w_ada, b_ada.reshape(1, N))


def _rope(t, cos, sin_lo, sin_hi, reps):
    w = t.shape[-1]
    tile = lambda a: jnp.concatenate([a] * reps, axis=1) if reps > 1 else a
    return (t * tile(cos) + pltpu.roll(t, w - ROPE_PAIRS, 1) * tile(sin_lo)
            + pltpu.roll(t, ROPE_PAIRS, 1) * tile(sin_hi))


def _inproj_kernel(x_ref, ctx_ref, sh_ref, sc_ref, w_ref, bones_ref, qg_ref, kg_ref,
                   cos_ref, slo_ref, shi_ref, q_ref, k_ref, v_ref, rw_ref, *, n_ctx_tiles):
    i = pl.program_id(1)
    x = jnp.where(i < n_ctx_tiles, ctx_ref[0], x_ref[0])
    h = _layer_norm(x) * (1.0 + sc_ref[0]) + sh_ref[0]
    p = _dg(h.astype(BF16), w_ref[...], _NN)
    bones = bones_ref[...]
    cos, slo, shi = cos_ref[...], slo_ref[...], shi_ref[...]

    q = p[:, :D_ATT]
    ss = _mm_exact_rhs(q * q, bones)
    q = q * lax.rsqrt(ss * (1.0 / HEAD_DIM) + RMS_EPS) * qg_ref[...]
    q = _rope(q, cos, slo, shi, D_ATT // 128) * (ATT_SCALE * LOG2_E)
    q_ref[0] = q.astype(BF16)

    k = p[:, D_ATT:D_ATT + D_ATT_KV]
    ss = _mm_exact_rhs(k * k, bones[:D_ATT_KV, :D_ATT_KV])
    k = k * lax.rsqrt(ss * (1.0 / HEAD_DIM) + RMS_EPS) * kg_ref[...]
    k = _rope(k, cos, slo, shi, D_ATT_KV // 128).astype(BF16)
    v = p[:, D_ATT + D_ATT_KV:D_ATT + 2 * D_ATT_KV].astype(BF16)
    for g in range(ATT_KV_HEADS):
        k_ref[0, g] = k[:, g * HEAD_DIM:(g + 1) * HEAD_DIM]
        v_ref[0, g] = v[:, g * HEAD_DIM:(g + 1) * HEAD_DIM]
    rw_ref[0] = p[:, D_ATT + 2 * D_ATT_KV:]


def _inproj(x, ctx, shift, scale, w_in_bf16, bones, qg, kg, cos, slo, shi):
    B, L, D = x.shape
    n_ctx = ctx.shape[1]
    tm = ROW_TILE
    nct = n_ctx // tm
    Tc = n_ctx + L
    d_in = w_in_bf16.shape[1]
    d_rw = d_in - D_ATT - 2 * D_ATT_KV
    mod_row = lambda b, i: (jnp.where(i < nct, B, b), 0, 0)
    const = lambda b, i: (0, 0)
    return pl.pallas_call(
        functools.partial(_inproj_kernel, n_ctx_tiles=nct),
        grid=(B, Tc // tm),
        in_specs=[pl.BlockSpec((1, tm, D), lambda b, i: (b, jnp.maximum(i - nct, 0), 0)),
                  pl.BlockSpec((1, tm, D), lambda b, i: (b, jnp.minimum(i, nct - 1), 0)),
                  pl.BlockSpec((1, 1, D), mod_row),
                  pl.BlockSpec((1, 1, D), mod_row),
                  pl.BlockSpec((D, d_in), const),
                  pl.BlockSpec(bones.shape, const),
                  pl.BlockSpec((1, D_ATT), const),
                  pl.BlockSpec((1, D_ATT_KV), const),
                  pl.BlockSpec((tm, 128), lambda b, i: (i, 0)),
                  pl.BlockSpec((tm, 128), lambda b, i: (i, 0)),
                  pl.BlockSpec((tm, 128), lambda b, i: (i, 0))],
        out_specs=[pl.BlockSpec((1, tm, D_ATT), lambda b, i: (b, jnp.maximum(i - nct, 0), 0)),
                   pl.BlockSpec((1, ATT_KV_HEADS, tm, HEAD_DIM), lambda b, i: (b, 0, i, 0)),
                   pl.BlockSpec((1, ATT_KV_HEADS, tm, HEAD_DIM), lambda b, i: (b, 0, i, 0)),
                   pl.BlockSpec((1, tm, d_rw), lambda b, i: (b, i, 0))],
        out_shape=[jax.ShapeDtypeStruct((B, L, D_ATT), BF16),
                   jax.ShapeDtypeStruct((B, ATT_KV_HEADS, Tc, HEAD_DIM), BF16),
                   jax.ShapeDtypeStruct((B, ATT_KV_HEADS, Tc, HEAD_DIM), BF16),
                   jax.ShapeDtypeStruct((B, Tc, d_rw), F32)],
        compiler_params=_params(("arbitrary", "arbitrary")),
        name="inproj",
    )(x, ctx, shift, scale, w_in_bf16, bones, qg, kg, cos, slo, shi)


def _attn_kernel(q_ref, k_ref, vt_ref, o_ref, s0, s1, p0, p1, m0, m1):
    i = pl.program_id(0)
    nk, tk, cols = s0.shape

    @pl.when(i == 0)
    def _():
        for ref in (s0, s1, m0, m1):
            ref[...] = jnp.zeros_like(ref)
        p0[...] = jnp.ones_like(p0)
        p1[...] = jnp.ones_like(p1)

    def step(s_w, m_w, s_r, m_r, p_w, p_r):
        q = q_ref[0]
        tq = q.shape[0]
        qs = jnp.concatenate([q[:, h * HEAD_DIM:(h + 1) * HEAD_DIM] for h in range(GQA_GROUP)], axis=0)
        m_prev = m_r[0:1, :]

        def body(j, carry):
            m_run, acc = carry
            off = pl.multiple_of(j * tk, tk)
            s_new = _dg(k_ref[0, 0, pl.ds(off, tk), :], qs, _NT)
            s_w[j] = s_new
            m_run = jnp.maximum(m_run, jnp.max(s_new.reshape(tk // 8, 8, cols), axis=0))
            p_w[j] = jnp.exp2(s_r[j] - m_prev).astype(BF16)
            acc = acc + _dg(vt_ref[0, 0, j], p_r[j], _NN)
            return m_run, acc

        init = (jnp.full((8, cols), -jnp.inf, F32), jnp.zeros((vt_ref.shape[3], cols), F32))
        m_run, acc = lax.fori_loop(0, nk, body, init, unroll=8)
        m_w[...] = jnp.broadcast_to(jnp.max(m_run, axis=0, keepdims=True), m_w.shape)
        o = acc[:HEAD_DIM] / acc[HEAD_DIM:HEAD_DIM + 1]
        o_ref[0] = jnp.concatenate([o[:, h * tq:(h + 1) * tq].T for h in range(GQA_GROUP)],
                                   axis=1).astype(BF16)

    @pl.when(i % 2 == 0)
    def _():
        step(s0, m0, s1, m1, p1, p0)

    @pl.when(i % 2 == 1)
    def _():
        step(s1, m1, s0, m0, p0, p1)


def _attention(q, k, vt):
    B, L, _ = q.shape
    Tc = k.shape[2]
    tq = ATT_Q_TILE
    n = L // tq
    gw = GQA_GROUP * HEAD_DIM
    cols = GQA_GROUP * tq
    tk = ATT_K_TILE
    nk = Tc // tk
    n_tiles = B * ATT_KV_HEADS * n

    def tile(s):
        t = jnp.clip(s, 0, n_tiles - 1)
        return t // (ATT_KV_HEADS * n), (t // n) % ATT_KV_HEADS, t % n

    def q_map(s):
        b, g, i = tile(s)
        return b, i, g

    def out_map(s):
        b, g, i = tile(s - 2)
        return b, i, g

    return pl.pallas_call(
        _attn_kernel,
        grid=(n_tiles + 2,),
        in_specs=[pl.BlockSpec((1, tq, gw), q_map),
                  pl.BlockSpec((1, 1, Tc, HEAD_DIM), lambda s: tile(s)[:2] + (0, 0)),
                  pl.BlockSpec((1, 1, nk, vt.shape[3], tk), lambda s: tile(s - 2)[:2] + (0, 0, 0))],
        out_specs=pl.BlockSpec((1, tq, gw), out_map),
        out_shape=jax.ShapeDtypeStruct((B, L, D_ATT), BF16),
        scratch_shapes=[pltpu.VMEM((nk, tk, cols), F32), pltpu.VMEM((nk, tk, cols), F32),
                        pltpu.VMEM((nk, tk, cols), BF16), pltpu.VMEM((nk, tk, cols), BF16),
                        pltpu.VMEM((8, cols), F32), pltpu.VMEM((8, cols), F32)],
        compiler_params=_params(("arbitrary",)),
        name="attention",
    )(q, k, vt)


def _rwkv_prep_kernel(rw_ref, prev_ref, next_ref, mu_ref, w0_ref, dup_ref, a0_ref, aup_ref, gup_ref,
                      kk_ref, ka_ref, rk_ref, bones_ref,
                      sh_o, g_o, bonus_o, dir_o, *, n_ctx, n_tok):
    i = pl.program_id(1)
    p = rw_ref[0]
    tm = p.shape[0]
    t0 = i * tm
    has_prev = jnp.where((t0 == 0) | (t0 == n_ctx), 0.0, 1.0)
    has_next = jnp.where((t0 + tm == n_ctx) | (t0 + tm == n_tok), 0.0, 1.0)
    row = lax.broadcasted_iota(jnp.int32, (tm, 1), 0)
    up = jnp.where(row == 0, prev_ref[0, 7:8, :] * has_prev, pltpu.roll(p, 1, 0))
    dn = jnp.where(row == tm - 1, next_ref[0, 0:1, :] * has_next, pltpu.roll(p, tm - 1, 0))
    s = p + (0.5 * (up + dn) - p) * mu_ref[...]

    D = D_RWKV
    r, k, v = s[:, :D], s[:, D:2 * D], s[:, 2 * D:3 * D]
    lora = s[:, 3 * D:3 * D + D_LORA]
    gl = s[:, 3 * D + D_LORA:]
    bones = bones_ref[...]

    g_o[0] = _mm(_sigmoid(gl), gup_ref[...], passes=3)
    kk = k * kk_ref[...]
    nrm = jnp.sqrt(_mm_exact_rhs(kk * kk, bones))
    kk = kk / jnp.maximum(nrm, L2_EPS)
    wt = jnp.tanh(lora)
    sh_o[0, :, 0:D] = r
    sh_o[0, :, D:2 * D] = v
    sh_o[0, :, 2 * D:3 * D] = kk
    bonus_o[0] = _mm_exact_rhs(r * k * rk_ref[...], bones) * v
    for d in range(2):
        z = w0_ref[d:d + 1, :] + _mm(wt, dup_ref[d], passes=3)
        dir_o[d, 0, :, 0:D] = -EXP_M05 * _sigmoid(z)
        a = _sigmoid(a0_ref[d:d + 1, :] + _mm(lora, aup_ref[d], passes=3))
        dir_o[d, 0, :, D:2 * D] = k * (1.0 + (a - 1.0) * ka_ref[...])
        dir_o[d, 0, :, 2 * D:3 * D] = kk * a


def _rwkv_prep(rw, n_ctx, mu, w0, dup, a0, aup, gup, k_k, k_a, r_k, bones):
    B, Tc, d_rw = rw.shape
    tm = ROW_TILE
    D = D_RWKV
    const2 = lambda b, i: (0, 0)
    const3 = lambda b, i: (0, 0, 0)
    tok = lambda n: pl.BlockSpec((1, tm, n * D), lambda b, i: (b, i, 0))
    tok2 = pl.BlockSpec((2, 1, tm, 3 * D), lambda b, i: (0, b, i, 0))
    s1 = lambda n: jax.ShapeDtypeStruct((B, Tc, n * D), F32)
    s2 = jax.ShapeDtypeStruct((2, B, Tc, 3 * D), F32)
    hb = tm // 8
    return pl.pallas_call(
        functools.partial(_rwkv_prep_kernel, n_ctx=n_ctx, n_tok=Tc),
        grid=(B, Tc // tm),
        in_specs=[pl.BlockSpec((1, tm, d_rw), lambda b, i: (b, i, 0)),
                  pl.BlockSpec((1, 8, d_rw), lambda b, i: (b, jnp.maximum(i * hb - 1, 0), 0)),
                  pl.BlockSpec((1, 8, d_rw), lambda b, i: (b, jnp.minimum((i + 1) * hb, Tc // 8 - 1), 0)),
                  pl.BlockSpec((1, d_rw), const2),
                  pl.BlockSpec((2, D), const2),
                  pl.BlockSpec((2, D_LORA, D), const3),
                  pl.BlockSpec((2, D), const2),
                  pl.BlockSpec((2, D_LORA, D), const3),
                  pl.BlockSpec((D_LORA, D), const2),
                  pl.BlockSpec((1, D), const2),
                  pl.BlockSpec((1, D), const2),
                  pl.BlockSpec((1, D), const2),
                  pl.BlockSpec(bones.shape, const2)],
        out_specs=[tok(3), tok(1), tok(1), tok2],
        out_shape=[s1(3), s1(1), s1(1), s2],
        compiler_params=_params(("arbitrary", "arbitrary")),
        name="rwkv_prep",
    )(rw, rw, rw, mu, w0, dup, a0, aup, gup, k_k, k_a, r_k, bones)


def _wkv_kernel(sh0, dir0, sh1, dir1, y0, y1, s_ref, *, n_heads):
    @pl.when(pl.program_id(1) == 0)
    def _():
        s_ref[...] = jnp.zeros_like(s_ref)

    C = WKV_CHUNK
    n_sub = sh0.shape[1] // C
    lax.fori_loop(0, n_sub, functools.partial(_wkv_chunk, refs=(sh0, dir0, sh1, dir1, y0, y1, s_ref),
                                              n_heads=n_heads, n_sub=n_sub), 0, unroll=2)


def _wkv_chunk(sub, carry, *, refs, n_heads, n_sub):
    sh0, dir0, sh1, dir1, y0, y1, s_ref = refs
    C = WKV_CHUNK
    N = RWKV_HEAD
    D = n_heads * N
    row = lax.broadcasted_iota(jnp.int32, (C, 2 * N), 0)
    lane = lax.broadcasted_iota(jnp.int32, (C, 2 * N), 1)
    col = lane % N
    first_half = lane < N
    trow = lax.broadcasted_iota(jnp.int32, (C, C), 0)
    tcol = lax.broadcasted_iota(jnp.int32, (C, C), 1)
    cat = lambda xs, ys: tuple(jnp.concatenate([x, y], axis=0) for x, y in zip(xs, ys))
    offs = (pl.multiple_of(sub * C, C), pl.multiple_of((n_sub - 1 - sub) * C, C))
    y_refs = (y0, y1)

    chains = []
    for nb, d in [(nb, d) for nb in range(sh0.shape[0]) for d in range(2)]:
        sh_ref, dir_ref = ((sh0, dir0), (sh1, dir1))[d]
        rows = pl.ds(offs[d], C)
        strict, incl = (row < col, row <= col) if d else (row > col, row >= col)
        tri = (trow <= tcol) if d else (trow >= tcol)
        lw = dir_ref[0, nb, rows, 0:D]
        cl = _mm_exact_lhs(jnp.where(tri, 1.0, 0.0).astype(BF16), lw)
        ecl = jnp.exp(cl)
        eneg = jnp.exp(-cl)
        last = 0 if d else C - 1
        etot = ecl[last:last + 1, :]
        kt = dir_ref[0, nb, rows, D:2 * D] * eneg
        bt = dir_ref[0, nb, rows, 2 * D:3 * D] * eneg
        full = dict(at=_split2(-(sh_ref[nb, rows, 2 * D:3 * D] * jnp.exp(cl - lw))),
                    rt=_split2(sh_ref[nb, rows, 0:D] * ecl),
                    v=_split2(sh_ref[nb, rows, D:2 * D]),
                    kt=_hi(kt), bt=_hi(bt), kh=_hi(kt * etot), bh=_hi(bt * etot))
        for h in range(n_heads):
            ch = {k: tuple(p[:, h * N:(h + 1) * N] for p in val) for k, val in full.items()}
            ch.update(nb=nb, d=d, h=h, strict=strict, incl=incl, etot=etot[:, h * N:(h + 1) * N],
                      s0=s_ref[nb, d, h])
            chains.append(ch)

    for ch in chains:
        rhs = cat(cat(ch["kt"], ch["bt"]), _hi(ch["s0"]))
        g = _mmp(cat(ch["at"], ch["rt"]), rhs, _NT)
        ch["top"] = jnp.where(ch["strict"], g[:C, :2 * N], 0.0)
        ch["bot"] = jnp.where(ch["incl"], g[C:, :2 * N], 0.0)
        ch["as0"], ch["rs0"] = g[:C, 2 * N:], g[C:, 2 * N:]
    for ch in chains:
        z = ch["as0"] + _mmp(_split2(ch["top"][:, :N]), ch["v"][:1], _NN)
        ch["w"] = jnp.where(first_half, jnp.concatenate([z, jnp.zeros_like(z)], axis=1), ch["top"])
        ch["pw"] = pltpu.roll(ch["top"], N, 1)[:, :N]
    n = 1
    while True:
        for ch in chains:
            ch["prod"] = _mmp(_split2(ch["pw"]), _split2(ch["w"]), _NN)
            ch["w"] = jnp.where(first_half, ch["w"], 0.0) + ch["prod"]
        n *= 2
        if n >= C:
            break
        for ch in chains:
            ch["pw"] = pltpu.roll(ch["prod"], N, 1)[:, :N]
    for ch in chains:
        ch["up"] = _split2(ch["w"][:, :N])
        y = ch["rs0"] + _mmp(_split2(ch["bot"]), cat(ch["v"][:1], ch["up"][:1]), _NN)
        y_refs[ch["d"]][ch["nb"], pl.ds(offs[ch["d"]], C), ch["h"] * N:(ch["h"] + 1) * N] = y
    for ch in chains:
        upd = _mmp(cat(ch["up"], ch["v"]), cat(ch["bh"], ch["kh"]), _TN)
        s_ref[ch["nb"], ch["d"], ch["h"]] = ch["s0"] * ch["etot"] + upd
    return carry


def _wkv_scan(shared, per_dir, n_ctx):
    B, T, D3 = shared.shape
    D = D3 // 3
    tb = WKV_BLOCK
    nb = WKV_SAMPLES if B % WKV_SAMPLES == 0 else 1
    n_blocks = T // tb
    n_cb = n_ctx // tb
    tok = (lambda c: c,
           lambda c: jnp.where(c < n_cb, n_cb - 1 - c, n_blocks - 1 + n_cb - c))
    in_specs, args = [], []
    for d in range(2):
        in_specs += [pl.BlockSpec((nb, tb, D3), lambda bi, c, d=d: (bi, tok[d](c), 0)),
                     pl.BlockSpec((1, nb, tb, D3), lambda bi, c, d=d: (d, bi, tok[d](c), 0))]
        args += [shared, per_dir]
    y = jax.ShapeDtypeStruct((B, T, D), F32)
    return pl.pallas_call(
        functools.partial(_wkv_kernel, n_heads=D // RWKV_HEAD),
        grid=(B // nb, n_blocks),
        in_specs=in_specs,
        out_specs=[pl.BlockSpec((nb, tb, D), lambda bi, c, d=d: (bi, tok[d](c), 0)) for d in range(2)],
        out_shape=[y, y],
        scratch_shapes=[pltpu.VMEM((nb, 2, D // RWKV_HEAD, RWKV_HEAD, RWKV_HEAD), F32)],
        compiler_params=_params(("arbitrary", "arbitrary")),
        name="wkv_scan",
    )(*args)


def _post_kernel(*refs, k):
    rows = lambda j: jnp.concatenate([r[0] for r in refs[j * k:(j + 1) * k]], axis=0)
    (att_ref, x_ref, gt_ref, sh_ref, sc_ref, wo_ref, lxg_ref, lxb_ref, l1g_ref, l1b_ref, rwt_ref, bones_ref,
     x1_ref, h2_ref, aff_ref) = refs[4 * k:]
    bones = bones_ref[...]
    y = rows(0) + rows(1)
    inv = 1.0 / RWKV_HEAD
    mu = _mm_exact_rhs(y, bones) * inv
    yc = y - mu
    var = _mm_exact_rhs(yc * yc, bones) * inv
    yn = yc * lax.rsqrt(var + GN_EPS) * lxg_ref[...] + lxb_ref[...]
    rw_out = (yn + rows(2)) * rows(3)
    mix = _dg(att_ref[0], wo_ref[:D_ATT, :], _NN) + _dg(rw_out.astype(BF16), wo_ref[D_ATT:, :], _NN)
    x1 = _layer_norm(ALPHA * x_ref[0] + gt_ref[0] * mix) * l1g_ref[...] + l1b_ref[...]
    x1_ref[0] = x1
    h2 = _layer_norm(x1) * (1.0 + sc_ref[0]) + sh_ref[0]
    h2_ref[0] = h2.astype(BF16)
    logits = _mm(rwt_ref[...], h2, _NT, passes=3)
    e = jnp.exp(logits - jnp.max(logits, axis=0, keepdims=True))
    aff_ref[0] = e / jnp.sum(e, axis=0, keepdims=True)


def _post(yf, yb, bonus, g, att, x, gt1, sh2, sc2, w_out_bf16, lnx_g, lnx_b, ln1_g, ln1_b, router_wt,
          bones, n_ctx):
    B, L, D = x.shape
    tm = min(POST_TILE, L)
    k = tm // ROW_TILE
    nct = n_ctx // ROW_TILE
    E = router_wt.shape[0]
    const = lambda b, i: (0, 0)
    cat = [pl.BlockSpec((1, ROW_TILE, D_RWKV), lambda b, i, j=j: (b, i * k + j + nct, 0)) for j in range(k)]
    mod = pl.BlockSpec((1, 1, D), lambda b, i: (b, 0, 0))
    vec = lambda n: pl.BlockSpec((1, n), const)
    return pl.pallas_call(
        functools.partial(_post_kernel, k=k),
        grid=(B, L // tm),
        in_specs=cat * 4 + [
                  pl.BlockSpec((1, tm, D_ATT), lambda b, i: (b, i, 0)),
                  pl.BlockSpec((1, tm, D), lambda b, i: (b, i, 0)),
                  mod, mod, mod,
                  pl.BlockSpec(w_out_bf16.shape, const),
                  vec(D_RWKV), vec(D_RWKV), vec(D), vec(D),
                  pl.BlockSpec((E, D), const),
                  pl.BlockSpec(bones.shape, const)],
        out_specs=[pl.BlockSpec((1, tm, D), lambda b, i: (b, i, 0)),
                   pl.BlockSpec((1, tm, D), lambda b, i: (b, i, 0)),
                   pl.BlockSpec((1, E, tm), lambda b, i: (b, 0, i))],
        out_shape=[jax.ShapeDtypeStruct((B, L, D), F32),
                   jax.ShapeDtypeStruct((B, L, D), BF16),
                   jax.ShapeDtypeStruct((B, E, L), F32)],
        compiler_params=_params(("arbitrary", "arbitrary")),
        name="post_mix",
    )(*([yf] * k + [yb] * k + [bonus] * k + [g] * k), att, x, gt1, sh2, sc2, w_out_bf16, lnx_g, lnx_b,
      ln1_g, ln1_b, router_wt, bones)


def _cumsum_lanes(x):
    n = x.shape[1]
    lane = lax.broadcasted_iota(jnp.int32, x.shape, 1)
    s = 1
    while s < n:
        x = x + jnp.where(lane >= s, pltpu.roll(x, s, 1), 0)
        s *= 2
    return x


def _route_kernel(aff_ref, pos_ref, gate_ref, cum_ref, *, cap, tile):
    aff = aff_ref[0]
    E = aff.shape[0]
    count = lambda m: jnp.sum(jnp.where(m, 1, 0), axis=1, keepdims=True)
    thr_bits = jnp.zeros((E, 1), jnp.int32)
    for bit in range(29, -1, -1):
        cand = thr_bits | (1 << bit)
        cand_f = lax.bitcast_convert_type(cand, F32)
        thr_bits = jnp.where(count(aff >= cand_f) >= cap, cand, thr_bits)
    thr = lax.bitcast_convert_type(thr_bits, F32)
    above = aff > thr
    tie = aff == thr
    need = cap - count(above)
    tie_rank = _cumsum_lanes(jnp.where(tie, 1, 0))
    sel = above | (tie & (tie_rank <= need))
    slot = _cumsum_lanes(jnp.where(sel, 1, 0)) - 1
    pos = jnp.where(sel, slot, -1)
    gate = jnp.where(sel, aff, 0.0)
    for e in range(E):
        pos_ref[0, e] = pos[e:e + 1, :]
        gate_ref[0, e] = gate[e:e + 1, :]
    lane = lax.broadcasted_iota(jnp.int32, (E, 128), 1)
    chosen = jnp.where(sel, 1, 0)
    run = jnp.zeros((E, 1), jnp.int32)
    cum = jnp.zeros((E, 128), jnp.int32)
    n_tiles = aff.shape[1] // tile
    for i in range(n_tiles):
        cum = jnp.where(lane == i, run, cum)
        run = run + jnp.sum(chosen[:, i * tile:(i + 1) * tile], axis=1, keepdims=True)
    cum_ref[0] = jnp.where(lane == n_tiles, run, cum)


def _route(aff, cap, tile):
    B, E, L = aff.shape
    out = pl.BlockSpec((1, E, 1, L), lambda b: (b, 0, 0, 0))
    return pl.pallas_call(
        functools.partial(_route_kernel, cap=cap, tile=tile),
        grid=(B,),
        in_specs=[pl.BlockSpec((1, E, L), lambda b: (b, 0, 0))],
        out_specs=[out, out, pl.BlockSpec((1, E, 128), lambda b: (b, 0, 0))],
        out_shape=[jax.ShapeDtypeStruct((B, E, 1, L), jnp.int32),
                   jax.ShapeDtypeStruct((B, E, 1, L), F32),
                   jax.ShapeDtypeStruct((B, E, 128), jnp.int32)],
        compiler_params=_params(("arbitrary",)),
        name="route",
    )(aff)


def _expert_kernel(h_ref, pos_ref, wg_ref, wu_ref, wd_ref, y_ref, *, cap):
    pos = pos_ref[0, 0]
    slot = lax.broadcasted_iota(jnp.int32, (cap, pos.shape[1]), 0)
    onehot = jnp.where(pos == slot, 1.0, 0.0).astype(BF16)
    xin = _dg(onehot, h_ref[0], _NN).astype(BF16)
    hg = _dg(xin, wg_ref[0], _NN)
    hu = _dg(xin, wu_ref[0], _NN)
    hid = (hg * _sigmoid(hg) * hu).astype(BF16)
    y_ref[0, 0] = _dg(hid, wd_ref[0], _NN).astype(BF16)


def _experts(h2, pos, wg, wu, wd, cap):
    B, L, D = h2.shape
    E, _, F = wg.shape
    rowv = pl.BlockSpec((1, 1, 1, L), lambda b, e: (b, e, 0, 0))
    return pl.pallas_call(
        functools.partial(_expert_kernel, cap=cap),
        grid=(B, E),
        in_specs=[pl.BlockSpec((1, L, D), lambda b, e: (b, 0, 0)),
                  rowv,
                  pl.BlockSpec((1, D, F), lambda b, e: (e, 0, 0)),
                  pl.BlockSpec((1, D, F), lambda b, e: (e, 0, 0)),
                  pl.BlockSpec((1, F, D), lambda b, e: (e, 0, 0))],
        out_specs=pl.BlockSpec((1, 1, cap, D), lambda b, e: (b, e, 0, 0)),
        out_shape=jax.ShapeDtypeStruct((B, E, cap, D), BF16),
        compiler_params=_params(("arbitrary", "arbitrary")),
        name="experts",
    )(h2, pos, wg, wu, wd)


def _combine_kernel(cum_ref, pos_ref, gate_ref, y_ref, x1_ref, gt_ref, lg_ref, lb_ref, o_ref, acc_ref, *, cap):
    b, i, e = pl.program_id(0), pl.program_id(1), pl.program_id(2)
    n_group = pos_ref.shape[1]

    @pl.when(e == 0)
    def _():
        acc_ref[...] = jnp.zeros_like(acc_ref)

    window = min(COMBINE_WINDOW, cap)
    align = min(128, window)

    def accumulate(j, first, width):
        pos = pos_ref[0, j]
        slot = first + lax.broadcasted_iota(jnp.int32, (width, pos.shape[1]), 0)
        weights = jnp.where(pos == slot, gate_ref[0, j], 0.0).astype(BF16)
        acc_ref[...] += _dg(weights, y_ref[0, j, pl.ds(first, width), :], _TN)

    for j in range(n_group):
        base = ((b * pl.num_programs(2) + e) * n_group + j) * 128
        lo, hi = cum_ref[base + i], cum_ref[base + i + 1]
        start = pl.multiple_of(jnp.minimum((lo // align) * align, cap - window), align)
        pl.when((hi > lo) & (hi - start <= window))(functools.partial(accumulate, j, start, window))
        pl.when(hi - start > window)(functools.partial(accumulate, j, 0, cap))

    @pl.when(e == pl.num_programs(2) - 1)
    def _():
        z = ALPHA * x1_ref[0] + gt_ref[0] * acc_ref[...]
        o_ref[0] = _layer_norm(z) * lg_ref[...] + lb_ref[...]


def _combine(cum, pos, gate, y, x1, gt2, ln2_g, ln2_b, cap):
    B, L, D = x1.shape
    E = y.shape[1]
    tl = min(COMBINE_TILE, L)
    ge = COMBINE_EXPERTS
    const = lambda b, i, e, cum: (0, 0)
    grid_spec = pltpu.PrefetchScalarGridSpec(
        num_scalar_prefetch=1,
        grid=(B, L // tl, E // ge),
        in_specs=[pl.BlockSpec((1, ge, 1, tl), lambda b, i, e, cum: (b, e, 0, i)),
                  pl.BlockSpec((1, ge, 1, tl), lambda b, i, e, cum: (b, e, 0, i)),
                  pl.BlockSpec((1, ge, cap, D), lambda b, i, e, cum: (b, e, 0, 0)),
                  pl.BlockSpec((1, tl, D), lambda b, i, e, cum: (b, i, 0)),
                  pl.BlockSpec((1, 1, D), lambda b, i, e, cum: (b, 0, 0)),
                  pl.BlockSpec((1, D), const),
                  pl.BlockSpec((1, D), const)],
        out_specs=pl.BlockSpec((1, tl, D), lambda b, i, e, cum: (b, i, 0)),
        scratch_shapes=[pltpu.VMEM((tl, D), F32)])
    return pl.pallas_call(
        functools.partial(_combine_kernel, cap=cap),
        grid_spec=grid_spec,
        out_shape=jax.ShapeDtypeStruct((B, L, D), F32),
        compiler_params=_params(("arbitrary", "arbitrary", "arbitrary")),
        name="combine",
    )(cum, pos, gate, y, x1, gt2, ln2_g, ln2_b)


def _rope_tables(n_ctx, n_lat):
    lane = jnp.arange(128)
    j = lane % HEAD_DIM
    axis = j // (2 * ROPE_PAIRS)
    upper = (j % (2 * ROPE_PAIRS)) >= ROPE_PAIRS
    inv = ROPE_THETA ** (-(j % ROPE_PAIRS).astype(F32) / ROPE_PAIRS)
    t = jnp.arange(n_lat)
    coord = jnp.where(axis[None, :] == 0, (t // GRID_W)[:, None], (t % GRID_W)[:, None]).astype(F32)
    ang = coord * inv[None, :]
    cos, sin = jnp.cos(ang), jnp.sin(ang)
    s_lo = jnp.where(upper[None, :], 0.0, -sin)
    s_hi = jnp.where(upper[None, :], sin, 0.0)
    pad = lambda a, fill: jnp.concatenate([jnp.full((n_ctx, 128), fill, F32), a], axis=0)
    return pad(cos, 1.0), pad(s_lo, 0.0), pad(s_hi, 0.0)


def _head_ones(n):
    h = jnp.arange(n) // HEAD_DIM
    return (h[:, None] == h[None, :]).astype(BF16)


def kernel(x, c, ctx, c_ctx, w_ada, b_ada, w_in, q_gain, k_gain, tshift_mu, decay_w0, decay_up, iclr_a0,
           iclr_up, gate_up, k_k, k_a, r_k, lnx_g, lnx_b, w_out, ln1_g, ln1_b, router_w, exp_w_gate,
           exp_w_up, exp_w_down, ln2_g, ln2_b):
    B, L, D = x.shape
    n_ctx = ctx.shape[1]
    li = 0
    row = lambda a: a.reshape(1, -1)

    n_rows = -(-(B + 1) // 8) * 8
    c_rows = jnp.concatenate([c, c_ctx[None, :], jnp.zeros((n_rows - B - 1, D), F32)], axis=0)
    mod = _ada_mod(c_rows, w_ada[li], b_ada[li]).reshape(n_rows, N_MOD, 1, D)
    sh1, sc1, gt1, sh2, sc2, gt2 = (mod[:, m] for m in range(N_MOD))

    bones = _head_ones(D_RWKV)
    cos, s_lo, s_hi = _rope_tables(n_ctx, L)
    q, k, v, rw = _inproj(x, ctx, sh1, sc1, w_in[li].astype(BF16), bones,
                          row(jnp.tile(q_gain[li], ATT_HEADS)), row(jnp.tile(k_gain[li], ATT_KV_HEADS)),
                          cos, s_lo, s_hi)
    Tc = n_ctx + L
    vt = v.reshape(B, ATT_KV_HEADS, Tc // ATT_K_TILE, ATT_K_TILE, HEAD_DIM).swapaxes(-1, -2)
    vt = jnp.concatenate([vt, jnp.ones(vt.shape[:3] + (16, ATT_K_TILE), BF16)], axis=3)
    att = _attention(q, k, vt)

    zpad = jnp.zeros((2, D_LORA // 2, D_RWKV), F32)
    dup = jnp.concatenate([decay_up[li], zpad], axis=1)
    aup = jnp.concatenate([zpad, iclr_up[li]], axis=1)
    scan_shared, g, bonus, scan_dir = _rwkv_prep(
        rw, n_ctx, row(tshift_mu[li]), decay_w0[li], dup, iclr_a0[li], aup, gate_up[li],
        row(k_k[li]), row(k_a[li]), row(r_k[li]), bones)
    yf, yb = _wkv_scan(scan_shared, scan_dir, n_ctx)

    x1, h2, aff = _post(yf, yb, bonus, g, att, x, gt1, sh2, sc2, w_out[li].astype(BF16),
                        row(lnx_g[li]), row(lnx_b[li]), row(ln1_g[li]), row(ln1_b[li]),
                        router_w[li].T, bones, n_ctx)
    cap = CAPACITY_FACTOR * L // N_EXPERTS
    pos, gate, cum = _route(aff, cap, min(COMBINE_TILE, L))
    y = _experts(h2, pos, exp_w_gate[li].astype(BF16), exp_w_up[li].astype(BF16),
                 exp_w_down[li].astype(BF16), cap)
    return _combine(cum.reshape(-1), pos, gate, y, x1, gt2, row(ln2_g[li]), row(ln2_b[li]), cap)
```

```python
import functools

import jax
import jax.numpy as jnp
from jax import lax
from jax.experimental import pallas as pl
from jax.experimental.pallas import tpu as pltpu

F32 = jnp.float32
BF16 = jnp.bfloat16

HEAD_DIM = 64
ATT_HEADS = 8
ATT_KV_HEADS = 2
GQA_GROUP = ATT_HEADS // ATT_KV_HEADS
D_ATT = ATT_HEADS * HEAD_DIM
D_ATT_KV = ATT_KV_HEADS * HEAD_DIM
ATT_SCALE = HEAD_DIM ** -0.5
LOG2_E = 1.4426950408889634
EXP_M05 = 0.6065306597126334
ROPE_THETA = 10000.0
ROPE_PAIRS = HEAD_DIM // 4
GRID_W = 64
RWKV_HEAD = 64
D_RWKV = 512
D_LORA = 128
N_EXPERTS = 16
CAPACITY_FACTOR = 2
N_MOD = 6
LN_EPS = 1e-5
RMS_EPS = 1e-6
GN_EPS = 64e-5
L2_EPS = 1e-12
DEPTH = 1
ALPHA = (2.0 * DEPTH) ** 0.25

WKV_CHUNK = 64
WKV_BLOCK = 256
WKV_SAMPLES = 2
ROW_TILE = 256
POST_TILE = 512
ATT_Q_TILE = 128
ATT_K_TILE = 256
COMBINE_TILE = 1024
COMBINE_WINDOW = 256
COMBINE_EXPERTS = 8
VMEM_LIMIT = 48 * 1024 * 1024


def _params(sem):
    return pltpu.CompilerParams(dimension_semantics=sem, vmem_limit_bytes=VMEM_LIMIT)


def _split3(x):
    hi = x.astype(BF16)
    r1 = x - hi.astype(F32)
    mid = r1.astype(BF16)
    lo = (r1 - mid.astype(F32)).astype(BF16)
    return hi, mid, lo


def _split2(x):
    hi = x.astype(BF16)
    lo = (x - hi.astype(F32)).astype(BF16)
    return hi, lo


def _hi(x):
    return (x.astype(BF16),)


def _dg(a, b, dims):
    return lax.dot_general(a, b, (dims, ((), ())), preferred_element_type=F32)


_NN = ((1,), (0,))
_NT = ((1,), (1,))
_TN = ((0,), (0,))


def _mm(a, b, dims=_NN, passes=3):
    if passes == 1:
        return _dg(a.astype(BF16), b.astype(BF16), dims)
    if passes == 3:
        ah, al = _split2(a)
        bh, bl = _split2(b)
        return _dg(ah, bh, dims) + (_dg(ah, bl, dims) + _dg(al, bh, dims))
    ah, am, al = _split3(a)
    bh, bm, bl = _split3(b)
    return (_dg(ah, bh, dims) + (_dg(ah, bm, dims) + _dg(am, bh, dims))
            + (_dg(am, bm, dims) + _dg(ah, bl, dims) + _dg(al, bh, dims)))


def _pieces(x, passes):
    return (x.astype(BF16),) if passes == 1 else _split2(x)


def _mmp(a, b, dims):
    out = _dg(a[0], b[0], dims)
    if len(a) > 1 and len(b) > 1:
        return out + (_dg(a[0], b[1], dims) + _dg(a[1], b[0], dims))
    if len(a) > 1:
        return out + _dg(a[1], b[0], dims)
    if len(b) > 1:
        return out + _dg(a[0], b[1], dims)
    return out


def _mm_exact_lhs(a_bf16, b, dims=_NN):
    bh, bm, bl = _split3(b)
    return _dg(a_bf16, bh, dims) + (_dg(a_bf16, bm, dims) + _dg(a_bf16, bl, dims))


def _mm_exact_rhs(a, b_bf16, dims=_NN):
    ah, al = _split2(a)
    return _dg(ah, b_bf16, dims) + _dg(al, b_bf16, dims)


def _sigmoid(x):
    return 1.0 / (1.0 + jnp.exp(-x))


def _layer_norm(x):
    mu = jnp.mean(x, axis=-1, keepdims=True)
    xc = x - mu
    var = jnp.mean(xc * xc, axis=-1, keepdims=True)
    return xc * lax.rsqrt(var + LN_EPS)


def _ada_kernel(c_ref, w_ref, b_ref, o_ref):
    c = c_ref[...]
    o_ref[...] = _mm(c * _sigmoid(c), w_ref[...], passes=6) + b_ref[...]


def _ada_mod(c_rows, w_ada, b_ada):
    R, D = c_rows.shape
    N = w_ada.shape[1]
    tn = 1024
    return pl.pallas_call(
        _ada_kernel,
        grid=(N // tn,),
        in_specs=[pl.BlockSpec((R, D), lambda j: (0, 0)),
                  pl.BlockSpec((D, tn), lambda j: (0, j)),
                  pl.BlockSpec((1, tn), lambda j: (0, j))],
        out_specs=pl.BlockSpec((R, tn), lambda j: (0, j)),
        out_shape=jax.ShapeDtypeStruct((R, N), F32),
        compiler_params=_params(("arbitrary",)),
        name="ada_mod",
    )(c_rows, w_ada, b_ada.reshape(1, N))


def _rope(t, cos, sin_lo, sin_hi, reps):
    w = t.shape[-1]
    tile = lambda a: jnp.concatenate([a] * reps, axis=1) if reps > 1 else a
    return (t * tile(cos) + pltpu.roll(t, w - ROPE_PAIRS, 1) * tile(sin_lo)
            + pltpu.roll(t, ROPE_PAIRS, 1) * tile(sin_hi))


def _inproj_kernel(x_ref, ctx_ref, sh_ref, sc_ref, w_ref, bones_ref, qg_ref, kg_ref,
                   cos_ref, slo_ref, shi_ref, q_ref, k_ref, v_ref, rw_ref, *, n_ctx_tiles):
    i = pl.program_id(1)
    x = jnp.where(i < n_ctx_tiles, ctx_ref[0], x_ref[0])
    h = _layer_norm(x) * (1.0 + sc_ref[0]) + sh_ref[0]
    p = _dg(h.astype(BF16), w_ref[...], _NN)
    bones = bones_ref[...]
    cos, slo, shi = cos_ref[...], slo_ref[...], shi_ref[...]

    q = p[:, :D_ATT]
    ss = _mm_exact_rhs(q * q, bones)
    q = q * lax.rsqrt(ss * (1.0 / HEAD_DIM) + RMS_EPS) * qg_ref[...]
    q = _rope(q, cos, slo, shi, D_ATT // 128) * (ATT_SCALE * LOG2_E)
    q_ref[0] = q.astype(BF16)

    k = p[:, D_ATT:D_ATT + D_ATT_KV]
    ss = _mm_exact_rhs(k * k, bones[:D_ATT_KV, :D_ATT_KV])
    k = k * lax.rsqrt(ss * (1.0 / HEAD_DIM) + RMS_EPS) * kg_ref[...]
    k = _rope(k, cos, slo, shi, D_ATT_KV // 128).astype(BF16)
    v = p[:, D_ATT + D_ATT_KV:D_ATT + 2 * D_ATT_KV].astype(BF16)
    for g in range(ATT_KV_HEADS):
        k_ref[0, g] = k[:, g * HEAD_DIM:(g + 1) * HEAD_DIM]
        v_ref[0, g] = v[:, g * HEAD_DIM:(g + 1) * HEAD_DIM]
    rw_ref[0] = p[:, D_ATT + 2 * D_ATT_KV:]


def _inproj(x, ctx, shift, scale, w_in_bf16, bones, qg, kg, cos, slo, shi):
    B, L, D = x.shape
    n_ctx = ctx.shape[1]
    tm = ROW_TILE
    nct = n_ctx // tm
    Tc = n_ctx + L
    d_in = w_in_bf16.shape[1]
    d_rw = d_in - D_ATT - 2 * D_ATT_KV
    mod_row = lambda b, i: (jnp.where(i < nct, B, b), 0, 0)
    const = lambda b, i: (0, 0)
    return pl.pallas_call(
        functools.partial(_inproj_kernel, n_ctx_tiles=nct),
        grid=(B, Tc // tm),
        in_specs=[pl.BlockSpec((1, tm, D), lambda b, i: (b, jnp.maximum(i - nct, 0), 0)),
                  pl.BlockSpec((1, tm, D), lambda b, i: (b, jnp.minimum(i, nct - 1), 0)),
                  pl.BlockSpec((1, 1, D), mod_row),
                  pl.BlockSpec((1, 1, D), mod_row),
                  pl.BlockSpec((D, d_in), const),
                  pl.BlockSpec(bones.shape, const),
                  pl.BlockSpec((1, D_ATT), const),
                  pl.BlockSpec((1, D_ATT_KV), const),
                  pl.BlockSpec((tm, 128), lambda b, i: (i, 0)),
                  pl.BlockSpec((tm, 128), lambda b, i: (i, 0)),
                  pl.BlockSpec((tm, 128), lambda b, i: (i, 0))],
        out_specs=[pl.BlockSpec((1, tm, D_ATT), lambda b, i: (b, jnp.maximum(i - nct, 0), 0)),
                   pl.BlockSpec((1, ATT_KV_HEADS, tm, HEAD_DIM), lambda b, i: (b, 0, i, 0)),
                   pl.BlockSpec((1, ATT_KV_HEADS, tm, HEAD_DIM), lambda b, i: (b, 0, i, 0)),
                   pl.BlockSpec((1, tm, d_rw), lambda b, i: (b, i, 0))],
        out_shape=[jax.ShapeDtypeStruct((B, L, D_ATT), BF16),
                   jax.ShapeDtypeStruct((B, ATT_KV_HEADS, Tc, HEAD_DIM), BF16),
                   jax.ShapeDtypeStruct((B, ATT_KV_HEADS, Tc, HEAD_DIM), BF16),
                   jax.ShapeDtypeStruct((B, Tc, d_rw), F32)],
        compiler_params=_params(("arbitrary", "arbitrary")),
        name="inproj",
    )(x, ctx, shift, scale, w_in_bf16, bones, qg, kg, cos, slo, shi)


def _attn_kernel(q_ref, k_ref, vt_ref, o_ref, s0, s1, p0, p1, m0, m1):
    i = pl.program_id(0)
    nk, tk, cols = s0.shape

    @pl.when(i == 0)
    def _():
        for ref in (s0, s1, m0, m1):
            ref[...] = jnp.zeros_like(ref)
        p0[...] = jnp.ones_like(p0)
        p1[...] = jnp.ones_like(p1)

    def step(s_w, m_w, s_r, m_r, p_w, p_r):
        q = q_ref[0]
        tq = q.shape[0]
        qs = jnp.concatenate([q[:, h * HEAD_DIM:(h + 1) * HEAD_DIM] for h in range(GQA_GROUP)], axis=0)
        m_prev = m_r[0:1, :]

        def body(j, carry):
            m_run, acc = carry
            off = pl.multiple_of(j * tk, tk)
            s_new = _dg(k_ref[0, 0, pl.ds(off, tk), :], qs, _NT)
            s_w[j] = s_new
            m_run = jnp.maximum(m_run, jnp.max(s_new.reshape(tk // 8, 8, cols), axis=0))
            p_w[j] = jnp.exp2(s_r[j] - m_prev).astype(BF16)
            acc = acc + _dg(vt_ref[0, 0, j], p_r[j], _NN)
            return m_run, acc

        init = (jnp.full((8, cols), -jnp.inf, F32), jnp.zeros((vt_ref.shape[3], cols), F32))
        m_run, acc = lax.fori_loop(0, nk, body, init, unroll=8)
        m_w[...] = jnp.broadcast_to(jnp.max(m_run, axis=0, keepdims=True), m_w.shape)
        o = acc[:HEAD_DIM] / acc[HEAD_DIM:HEAD_DIM + 1]
        o_ref[0] = jnp.concatenate([o[:, h * tq:(h + 1) * tq].T for h in range(GQA_GROUP)],
                                   axis=1).astype(BF16)

    @pl.when(i % 2 == 0)
    def _():
        step(s0, m0, s1, m1, p1, p0)

    @pl.when(i % 2 == 1)
    def _():
        step(s1, m1, s0, m0, p0, p1)


def _attention(q, k, vt):
    B, L, _ = q.shape
    Tc = k.shape[2]
    tq = ATT_Q_TILE
    n = L // tq
    gw = GQA_GROUP * HEAD_DIM
    cols = GQA_GROUP * tq
    tk = ATT_K_TILE
    nk = Tc // tk
    n_tiles = B * ATT_KV_HEADS * n

    def tile(s):
        t = jnp.clip(s, 0, n_tiles - 1)
        return t // (ATT_KV_HEADS * n), (t // n) % ATT_KV_HEADS, t % n

    def q_map(s):
        b, g, i = tile(s)
        return b, i, g

    def out_map(s):
        b, g, i = tile(s - 2)
        return b, i, g

    return pl.pallas_call(
        _attn_kernel,
        grid=(n_tiles + 2,),
        in_specs=[pl.BlockSpec((1, tq, gw), q_map),
                  pl.BlockSpec((1, 1, Tc, HEAD_DIM), lambda s: tile(s)[:2] + (0, 0)),
                  pl.BlockSpec((1, 1, nk, vt.shape[3], tk), lambda s: tile(s - 2)[:2] + (0, 0, 0))],
        out_specs=pl.BlockSpec((1, tq, gw), out_map),
        out_shape=jax.ShapeDtypeStruct((B, L, D_ATT), BF16),
        scratch_shapes=[pltpu.VMEM((nk, tk, cols), F32), pltpu.VMEM((nk, tk, cols), F32),
                        pltpu.VMEM((nk, tk, cols), BF16), pltpu.VMEM((nk, tk, cols), BF16),
                        pltpu.VMEM((8, cols), F32), pltpu.VMEM((8, cols), F32)],
        compiler_params=_params(("arbitrary",)),
        name="attention",
    )(q, k, vt)


def _rwkv_prep_kernel(rw_ref, prev_ref, next_ref, mu_ref, w0_ref, dup_ref, a0_ref, aup_ref, gup_ref,
                      kk_ref, ka_ref, rk_ref, bones_ref,
                      sh_o, g_o, bonus_o, dir_o, *, n_ctx, n_tok):
    i = pl.program_id(1)
    p = rw_ref[0]
    tm = p.shape[0]
    t0 = i * tm
    has_prev = jnp.where((t0 == 0) | (t0 == n_ctx), 0.0, 1.0)
    has_next = jnp.where((t0 + tm == n_ctx) | (t0 + tm == n_tok), 0.0, 1.0)
    row8 = lax.broadcasted_iota(jnp.int32, (8, 1), 0)
    up = pltpu.roll(p, 1, 0)
    dn = pltpu.roll(p, tm - 1, 0)
    up = jnp.concatenate([jnp.where(row8 == 0, prev_ref[0, 7:8, :] * has_prev, up[:8]), up[8:]], axis=0)
    dn = jnp.concatenate([dn[:tm - 8], jnp.where(row8 == 7, next_ref[0, 0:1, :] * has_next, dn[tm - 8:])], axis=0)
    mu = mu_ref[...]
    s = p * (1.0 - mu) + (up + dn) * (0.5 * mu)

    D = D_RWKV
    r, k, v = s[:, :D], s[:, D:2 * D], s[:, 2 * D:3 * D]
    lora = s[:, 3 * D:3 * D + D_LORA]
    gl = s[:, 3 * D + D_LORA:]
    bones = bones_ref[...]

    g_o[0] = _mm(_sigmoid(gl), gup_ref[...], passes=3)
    kk = k * kk_ref[...]
    nrm = jnp.sqrt(_mm_exact_rhs(kk * kk, bones))
    kk = kk / jnp.maximum(nrm, L2_EPS)
    wt = jnp.tanh(lora)
    sh_o[0, :, 0:D] = r
    sh_o[0, :, D:2 * D] = v
    sh_o[0, :, 2 * D:3 * D] = kk
    bonus_o[0] = _mm_exact_rhs(r * k * rk_ref[...], bones) * v
    for d in range(2):
        z = w0_ref[d:d + 1, :] + _mm(wt, dup_ref[d], passes=3)
        dir_o[d, 0, :, 0:D] = -EXP_M05 * _sigmoid(z)
        a = _sigmoid(a0_ref[d:d + 1, :] + _mm(lora, aup_ref[d], passes=3))
        dir_o[d, 0, :, D:2 * D] = k * (1.0 + (a - 1.0) * ka_ref[...])
        dir_o[d, 0, :, 2 * D:3 * D] = kk * a


def _rwkv_prep(rw, n_ctx, mu, w0, dup, a0, aup, gup, k_k, k_a, r_k, bones):
    B, Tc, d_rw = rw.shape
    tm = ROW_TILE
    D = D_RWKV
    const2 = lambda b, i: (0, 0)
    const3 = lambda b, i: (0, 0, 0)
    tok = lambda n: pl.BlockSpec((1, tm, n * D), lambda b, i: (b, i, 0))
    tok2 = pl.BlockSpec((2, 1, tm, 3 * D), lambda b, i: (0, b, i, 0))
    s1 = lambda n: jax.ShapeDtypeStruct((B, Tc, n * D), F32)
    s2 = jax.ShapeDtypeStruct((2, B, Tc, 3 * D), F32)
    hb = tm // 8
    return pl.pallas_call(
        functools.partial(_rwkv_prep_kernel, n_ctx=n_ctx, n_tok=Tc),
        grid=(B, Tc // tm),
        in_specs=[pl.BlockSpec((1, tm, d_rw), lambda b, i: (b, i, 0)),
                  pl.BlockSpec((1, 8, d_rw), lambda b, i: (b, jnp.maximum(i * hb - 1, 0), 0)),
                  pl.BlockSpec((1, 8, d_rw), lambda b, i: (b, jnp.minimum((i + 1) * hb, Tc // 8 - 1), 0)),
                  pl.BlockSpec((1, d_rw), const2),
                  pl.BlockSpec((2, D), const2),
                  pl.BlockSpec((2, D_LORA, D), const3),
                  pl.BlockSpec((2, D), const2),
                  pl.BlockSpec((2, D_LORA, D), const3),
                  pl.BlockSpec((D_LORA, D), const2),
                  pl.BlockSpec((1, D), const2),
                  pl.BlockSpec((1, D), const2),
                  pl.BlockSpec((1, D), const2),
                  pl.BlockSpec(bones.shape, const2)],
        out_specs=[tok(3), tok(1), tok(1), tok2],
        out_shape=[s1(3), s1(1), s1(1), s2],
        compiler_params=_params(("arbitrary", "arbitrary")),
        name="rwkv_prep",
    )(rw, rw, rw, mu, w0, dup, a0, aup, gup, k_k, k_a, r_k, bones)


def _wkv_kernel(sh0, dir0, sh1, dir1, y0, y1, s_ref, *, n_heads):
    @pl.when(pl.program_id(1) == 0)
    def _():
        s_ref[...] = jnp.zeros_like(s_ref)

    C = WKV_CHUNK
    n_sub = sh0.shape[1] // C
    lax.fori_loop(0, n_sub, functools.partial(_wkv_chunk, refs=(sh0, dir0, sh1, dir1, y0, y1, s_ref),
                                              n_heads=n_heads, n_sub=n_sub), 0, unroll=2)


def _wkv_chunk(sub, carry, *, refs, n_heads, n_sub):
    sh0, dir0, sh1, dir1, y0, y1, s_ref = refs
    C = WKV_CHUNK
    N = RWKV_HEAD
    D = n_heads * N
    row = lax.broadcasted_iota(jnp.int32, (C, 2 * N), 0)
    lane = lax.broadcasted_iota(jnp.int32, (C, 2 * N), 1)
    col = lane % N
    first_half = lane < N
    trow = lax.broadcasted_iota(jnp.int32, (C, C), 0)
    tcol = lax.broadcasted_iota(jnp.int32, (C, C), 1)
    cat = lambda xs, ys: tuple(jnp.concatenate([x, y], axis=0) for x, y in zip(xs, ys))
    offs = (pl.multiple_of(sub * C, C), pl.multiple_of((n_sub - 1 - sub) * C, C))
    y_refs = (y0, y1)

    chains = []
    for nb, d in [(nb, d) for nb in range(sh0.shape[0]) for d in range(2)]:
        sh_ref, dir_ref = ((sh0, dir0), (sh1, dir1))[d]
        rows = pl.ds(offs[d], C)
        strict, incl = (row < col, row <= col) if d else (row > col, row >= col)
        tri = (trow <= tcol) if d else (trow >= tcol)
        lw = dir_ref[0, nb, rows, 0:D]
        cl = _mm_exact_lhs(jnp.where(tri, 1.0, 0.0).astype(BF16), lw)
        ecl = jnp.exp(cl)
        eneg = jnp.exp(-cl)
        last = 0 if d else C - 1
        etot = ecl[last:last + 1, :]
        kt = dir_ref[0, nb, rows, D:2 * D] * eneg
        bt = dir_ref[0, nb, rows, 2 * D:3 * D] * eneg
        full = dict(at=_split2(-(sh_ref[nb, rows, 2 * D:3 * D] * jnp.exp(cl - lw))),
                    rt=_split2(sh_ref[nb, rows, 0:D] * ecl),
                    v=_split2(sh_ref[nb, rows, D:2 * D]),
                    kt=_hi(kt), bt=_hi(bt), kh=_hi(kt * etot), bh=_hi(bt * etot))
        for h in range(n_heads):
            ch = {k: tuple(p[:, h * N:(h + 1) * N] for p in val) for k, val in full.items()}
            ch.update(nb=nb, d=d, h=h, strict=strict, incl=incl, etot=etot[:, h * N:(h + 1) * N],
                      s0=s_ref[nb, d, h])
            chains.append(ch)

    for ch in chains:
        rhs = cat(cat(ch["kt"], ch["bt"]), _hi(ch["s0"]))
        g = _mmp(cat(ch["at"], ch["rt"]), rhs, _NT)
        ch["top"] = jnp.where(ch["strict"], g[:C, :2 * N], 0.0)
        ch["bot"] = jnp.where(ch["incl"], g[C:, :2 * N], 0.0)
        ch["as0"], ch["rs0"] = g[:C, 2 * N:], g[C:, 2 * N:]
    for ch in chains:
        z = ch["as0"] + _mmp(_split2(ch["top"][:, :N]), ch["v"][:1], _NN)
        ch["w"] = jnp.where(first_half, jnp.concatenate([z, jnp.zeros_like(z)], axis=1), ch["top"])
        ch["pw"] = pltpu.roll(ch["top"], N, 1)[:, :N]
    n = 1
    while True:
        for ch in chains:
            ch["prod"] = _mmp(_split2(ch["pw"]), _split2(ch["w"]), _NN)
            ch["w"] = jnp.where(first_half, ch["w"], 0.0) + ch["prod"]
        n *= 2
        if n >= C:
            break
        for ch in chains:
            ch["pw"] = pltpu.roll(ch["prod"], N, 1)[:, :N]
    for ch in chains:
        ch["up"] = _split2(ch["w"][:, :N])
        y = ch["rs0"] + _mmp(_split2(ch["bot"]), cat(ch["v"][:1], ch["up"][:1]), _NN)
        y_refs[ch["d"]][ch["nb"], pl.ds(offs[ch["d"]], C), ch["h"] * N:(ch["h"] + 1) * N] = y
    for ch in chains:
        upd = _mmp(cat(ch["up"], ch["v"]), cat(ch["bh"], ch["kh"]), _TN)
        s_ref[ch["nb"], ch["d"], ch["h"]] = ch["s0"] * ch["etot"] + upd
    return carry


def _wkv_scan(shared, per_dir, n_ctx):
    B, T, D3 = shared.shape
    D = D3 // 3
    tb = WKV_BLOCK
    nb = WKV_SAMPLES if B % WKV_SAMPLES == 0 else 1
    n_blocks = T // tb
    n_cb = n_ctx // tb
    tok = (lambda c: c,
           lambda c: jnp.where(c < n_cb, n_cb - 1 - c, n_blocks - 1 + n_cb - c))
    in_specs, args = [], []
    for d in range(2):
        in_specs += [pl.BlockSpec((nb, tb, D3), lambda bi, c, d=d: (bi, tok[d](c), 0)),
                     pl.BlockSpec((1, nb, tb, D3), lambda bi, c, d=d: (d, bi, tok[d](c), 0))]
        args += [shared, per_dir]
    y = jax.ShapeDtypeStruct((B, T, D), F32)
    return pl.pallas_call(
        functools.partial(_wkv_kernel, n_heads=D // RWKV_HEAD),
        grid=(B // nb, n_blocks),
        in_specs=in_specs,
        out_specs=[pl.BlockSpec((nb, tb, D), lambda bi, c, d=d: (bi, tok[d](c), 0)) for d in range(2)],
        out_shape=[y, y],
        scratch_shapes=[pltpu.VMEM((nb, 2, D // RWKV_HEAD, RWKV_HEAD, RWKV_HEAD), F32)],
        compiler_params=_params(("arbitrary", "arbitrary")),
        name="wkv_scan",
    )(*args)


def _post_kernel(*refs, k):
    yf, yb, bonus, gate = (refs[j * k:(j + 1) * k] for j in range(4))
    (att_ref, x_ref, gt_ref, sh_ref, sc_ref, wo_ref, lxg_ref, lxb_ref, l1g_ref, l1b_ref, rwt_ref, bones_ref,
     x1_ref, h2_ref, aff_ref) = refs[4 * k:]
    n = yf[0].shape[1]
    blocks = range(k)
    rows = [slice(j * n, (j + 1) * n) for j in blocks]
    bones = bones_ref[...]
    inv = 1.0 / RWKV_HEAD
    y = [yf[j][0] + yb[j][0] for j in blocks]
    mu = [_mm_exact_rhs(y[j], bones) * inv for j in blocks]
    yc = [y[j] - mu[j] for j in blocks]
    var = [_mm_exact_rhs(yc[j] * yc[j], bones) * inv for j in blocks]
    rw_out = [((yc[j] * lax.rsqrt(var[j] + GN_EPS) * lxg_ref[...] + lxb_ref[...] + bonus[j][0])
               * gate[j][0]).astype(BF16) for j in blocks]
    mix = [_dg(att_ref[0, rows[j], :], wo_ref[:D_ATT, :], _NN) + _dg(rw_out[j], wo_ref[D_ATT:, :], _NN)
           for j in blocks]
    x1 = [_layer_norm(ALPHA * x_ref[0, rows[j], :] + gt_ref[0] * mix[j]) * l1g_ref[...] + l1b_ref[...]
          for j in blocks]
    h2 = [_layer_norm(x1[j]) * (1.0 + sc_ref[0]) + sh_ref[0] for j in blocks]
    logits = [_mm(rwt_ref[...], h2[j], _NT, passes=3) for j in blocks]
    for j in blocks:
        x1_ref[0, rows[j], :] = x1[j]
        h2_ref[0, rows[j], :] = h2[j].astype(BF16)
        e = jnp.exp(logits[j] - jnp.max(logits[j], axis=0, keepdims=True))
        aff_ref[0, :, rows[j]] = e / jnp.sum(e, axis=0, keepdims=True)


def _post(yf, yb, bonus, g, att, x, gt1, sh2, sc2, w_out_bf16, lnx_g, lnx_b, ln1_g, ln1_b, router_wt,
          bones, n_ctx):
    B, L, D = x.shape
    tm = min(POST_TILE, L)
    k = tm // ROW_TILE
    nct = n_ctx // ROW_TILE
    E = router_wt.shape[0]
    const = lambda b, i: (0, 0)
    cat = [pl.BlockSpec((1, ROW_TILE, D_RWKV), lambda b, i, j=j: (b, i * k + j + nct, 0)) for j in range(k)]
    mod = pl.BlockSpec((1, 1, D), lambda b, i: (b, 0, 0))
    vec = lambda n: pl.BlockSpec((1, n), const)
    return pl.pallas_call(
        functools.partial(_post_kernel, k=k),
        grid=(B, L // tm),
        in_specs=cat * 4 + [
                  pl.BlockSpec((1, tm, D_ATT), lambda b, i: (b, i, 0)),
                  pl.BlockSpec((1, tm, D), lambda b, i: (b, i, 0)),
                  mod, mod, mod,
                  pl.BlockSpec(w_out_bf16.shape, const),
                  vec(D_RWKV), vec(D_RWKV), vec(D), vec(D),
                  pl.BlockSpec((E, D), const),
                  pl.BlockSpec(bones.shape, const)],
        out_specs=[pl.BlockSpec((1, tm, D), lambda b, i: (b, i, 0)),
                   pl.BlockSpec((1, tm, D), lambda b, i: (b, i, 0)),
                   pl.BlockSpec((1, E, tm), lambda b, i: (b, 0, i))],
        out_shape=[jax.ShapeDtypeStruct((B, L, D), F32),
                   jax.ShapeDtypeStruct((B, L, D), BF16),
                   jax.ShapeDtypeStruct((B, E, L), F32)],
        compiler_params=_params(("arbitrary", "arbitrary")),
        name="post_mix",
    )(*([yf] * k + [yb] * k + [bonus] * k + [g] * k), att, x, gt1, sh2, sc2, w_out_bf16, lnx_g, lnx_b,
      ln1_g, ln1_b, router_wt, bones)


def _cumsum_lanes(x):
    n = x.shape[1]
    lane = lax.broadcasted_iota(jnp.int32, x.shape, 1)
    s = 1
    while s < n:
        x = x + jnp.where(lane >= s, pltpu.roll(x, s, 1), 0)
        s *= 2
    return x


def _route_kernel(aff_ref, pos_ref, gate_ref, cum_ref, *, cap, tile):
    aff = aff_ref[0]
    E = aff.shape[0]
    count = lambda m: jnp.sum(jnp.where(m, 1, 0), axis=1, keepdims=True)
    thr_bits = jnp.zeros((E, 1), jnp.int32)
    for bit in range(29, -1, -1):
        cand = thr_bits | (1 << bit)
        cand_f = lax.bitcast_convert_type(cand, F32)
        thr_bits = jnp.where(count(aff >= cand_f) >= cap, cand, thr_bits)
    thr = lax.bitcast_convert_type(thr_bits, F32)
    above = aff > thr
    tie = aff == thr
    need = cap - count(above)
    tie_rank = _cumsum_lanes(jnp.where(tie, 1, 0))
    sel = above | (tie & (tie_rank <= need))
    slot = _cumsum_lanes(jnp.where(sel, 1, 0)) - 1
    pos = jnp.where(sel, slot, -1)
    gate = jnp.where(sel, aff, 0.0)
    for e in range(E):
        pos_ref[0, e] = pos[e:e + 1, :]
        gate_ref[0, e] = gate[e:e + 1, :]
    lane = lax.broadcasted_iota(jnp.int32, (E, 128), 1)
    chosen = jnp.where(sel, 1, 0)
    run = jnp.zeros((E, 1), jnp.int32)
    cum = jnp.zeros((E, 128), jnp.int32)
    n_tiles = aff.shape[1] // tile
    for i in range(n_tiles):
        cum = jnp.where(lane == i, run, cum)
        run = run + jnp.sum(chosen[:, i * tile:(i + 1) * tile], axis=1, keepdims=True)
    cum_ref[0] = jnp.where(lane == n_tiles, run, cum)


def _route(aff, cap, tile):
    B, E, L = aff.shape
    out = pl.BlockSpec((1, E, 1, L), lambda b: (b, 0, 0, 0))
    return pl.pallas_call(
        functools.partial(_route_kernel, cap=cap, tile=tile),
        grid=(B,),
        in_specs=[pl.BlockSpec((1, E, L), lambda b: (b, 0, 0))],
        out_specs=[out, out, pl.BlockSpec((1, E, 128), lambda b: (b, 0, 0))],
        out_shape=[jax.ShapeDtypeStruct((B, E, 1, L), jnp.int32),
                   jax.ShapeDtypeStruct((B, E, 1, L), F32),
                   jax.ShapeDtypeStruct((B, E, 128), jnp.int32)],
        compiler_params=_params(("arbitrary",)),
        name="route",
    )(aff)


def _expert_kernel(h_ref, pos_ref, wg_ref, wu_ref, wd_ref, y_ref, *, cap):
    pos = pos_ref[0, 0]
    slot = lax.broadcasted_iota(jnp.int32, (cap, pos.shape[1]), 0)
    onehot = jnp.where(pos == slot, 1.0, 0.0).astype(BF16)
    xin = _dg(onehot, h_ref[0], _NN).astype(BF16)
    hg = _dg(xin, wg_ref[0], _NN)
    hu = _dg(xin, wu_ref[0], _NN)
    hid = (hg * _sigmoid(hg) * hu).astype(BF16)
    y_ref[0, 0] = _dg(hid, wd_ref[0], _NN).astype(BF16)


def _experts(h2, pos, wg, wu, wd, cap):
    B, L, D = h2.shape
    E, _, F = wg.shape
    rowv = pl.BlockSpec((1, 1, 1, L), lambda b, e: (b, e, 0, 0))
    return pl.pallas_call(
        functools.partial(_expert_kernel, cap=cap),
        grid=(B, E),
        in_specs=[pl.BlockSpec((1, L, D), lambda b, e: (b, 0, 0)),
                  rowv,
                  pl.BlockSpec((1, D, F), lambda b, e: (e, 0, 0)),
                  pl.BlockSpec((1, D, F), lambda b, e: (e, 0, 0)),
                  pl.BlockSpec((1, F, D), lambda b, e: (e, 0, 0))],
        out_specs=pl.BlockSpec((1, 1, cap, D), lambda b, e: (b, e, 0, 0)),
        out_shape=jax.ShapeDtypeStruct((B, E, cap, D), BF16),
        compiler_params=_params(("arbitrary", "arbitrary")),
        name="experts",
    )(h2, pos, wg, wu, wd)


def _combine_kernel(cum_ref, pos_ref, gate_ref, y_ref, x1_ref, gt_ref, lg_ref, lb_ref, o_ref, acc_ref, *, cap):
    b, i, e = pl.program_id(0), pl.program_id(1), pl.program_id(2)
    n_group = pos_ref.shape[1]

    @pl.when(e == 0)
    def _():
        acc_ref[...] = jnp.zeros_like(acc_ref)

    window = min(COMBINE_WINDOW, cap)
    align = min(128, window)

    def accumulate(j, first, width):
        pos = pos_ref[0, j]
        slot = first + lax.broadcasted_iota(jnp.int32, (width, pos.shape[1]), 0)
        weights = jnp.where(pos == slot, gate_ref[0, j], 0.0).astype(BF16)
        acc_ref[...] += _dg(weights, y_ref[0, j, pl.ds(first, width), :], _TN)

    for j in range(n_group):
        base = ((b * pl.num_programs(2) + e) * n_group + j) * 128
        lo, hi = cum_ref[base + i], cum_ref[base + i + 1]
        start = pl.multiple_of(jnp.minimum((lo // align) * align, cap - window), align)
        pl.when((hi > lo) & (hi - start <= window))(functools.partial(accumulate, j, start, window))
        pl.when(hi - start > window)(functools.partial(accumulate, j, 0, cap))

    @pl.when(e == pl.num_programs(2) - 1)
    def _():
        z = ALPHA * x1_ref[0] + gt_ref[0] * acc_ref[...]
        o_ref[0] = _layer_norm(z) * lg_ref[...] + lb_ref[...]


def _combine(cum, pos, gate, y, x1, gt2, ln2_g, ln2_b, cap):
    B, L, D = x1.shape
    E = y.shape[1]
    tl = min(COMBINE_TILE, L)
    ge = COMBINE_EXPERTS
    const = lambda b, i, e, cum: (0, 0)
    grid_spec = pltpu.PrefetchScalarGridSpec(
        num_scalar_prefetch=1,
        grid=(B, L // tl, E // ge),
        in_specs=[pl.BlockSpec((1, ge, 1, tl), lambda b, i, e, cum: (b, e, 0, i)),
                  pl.BlockSpec((1, ge, 1, tl), lambda b, i, e, cum: (b, e, 0, i)),
                  pl.BlockSpec((1, ge, cap, D), lambda b, i, e, cum: (b, e, 0, 0)),
                  pl.BlockSpec((1, tl, D), lambda b, i, e, cum: (b, i, 0)),
                  pl.BlockSpec((1, 1, D), lambda b, i, e, cum: (b, 0, 0)),
                  pl.BlockSpec((1, D), const),
                  pl.BlockSpec((1, D), const)],
        out_specs=pl.BlockSpec((1, tl, D), lambda b, i, e, cum: (b, i, 0)),
        scratch_shapes=[pltpu.VMEM((tl, D), F32)])
    return pl.pallas_call(
        functools.partial(_combine_kernel, cap=cap),
        grid_spec=grid_spec,
        out_shape=jax.ShapeDtypeStruct((B, L, D), F32),
        compiler_params=_params(("arbitrary", "arbitrary", "arbitrary")),
        name="combine",
    )(cum, pos, gate, y, x1, gt2, ln2_g, ln2_b)


def _rope_tables(n_ctx, n_lat):
    lane = jnp.arange(128)
    j = lane % HEAD_DIM
    axis = j // (2 * ROPE_PAIRS)
    upper = (j % (2 * ROPE_PAIRS)) >= ROPE_PAIRS
    inv = ROPE_THETA ** (-(j % ROPE_PAIRS).astype(F32) / ROPE_PAIRS)
    t = jnp.arange(n_lat)
    coord = jnp.where(axis[None, :] == 0, (t // GRID_W)[:, None], (t % GRID_W)[:, None]).astype(F32)
    ang = coord * inv[None, :]
    cos, sin = jnp.cos(ang), jnp.sin(ang)
    s_lo = jnp.where(upper[None, :], 0.0, -sin)
    s_hi = jnp.where(upper[None, :], sin, 0.0)
    pad = lambda a, fill: jnp.concatenate([jnp.full((n_ctx, 128), fill, F32), a], axis=0)
    return pad(cos, 1.0), pad(s_lo, 0.0), pad(s_hi, 0.0)


def _head_ones(n):
    h = jnp.arange(n) // HEAD_DIM
    return (h[:, None] == h[None, :]).astype(BF16)


def kernel(x, c, ctx, c_ctx, w_ada, b_ada, w_in, q_gain, k_gain, tshift_mu, decay_w0, decay_up, iclr_a0,
           iclr_up, gate_up, k_k, k_a, r_k, lnx_g, lnx_b, w_out, ln1_g, ln1_b, router_w, exp_w_gate,
           exp_w_up, exp_w_down, ln2_g, ln2_b):
    B, L, D = x.shape
    n_ctx = ctx.shape[1]
    li = 0
    row = lambda a: a.reshape(1, -1)

    n_rows = -(-(B + 1) // 8) * 8
    c_rows = jnp.concatenate([c, c_ctx[None, :], jnp.zeros((n_rows - B - 1, D), F32)], axis=0)
    mod = _ada_mod(c_rows, w_ada[li], b_ada[li]).reshape(n_rows, N_MOD, 1, D)
    sh1, sc1, gt1, sh2, sc2, gt2 = (mod[:, m] for m in range(N_MOD))

    bones = _head_ones(D_RWKV)
    cos, s_lo, s_hi = _rope_tables(n_ctx, L)
    q, k, v, rw = _inproj(x, ctx, sh1, sc1, w_in[li].astype(BF16), bones,
                          row(jnp.tile(q_gain[li], ATT_HEADS)), row(jnp.tile(k_gain[li], ATT_KV_HEADS)),
                          cos, s_lo, s_hi)
    Tc = n_ctx + L
    vt = v.reshape(B, ATT_KV_HEADS, Tc // ATT_K_TILE, ATT_K_TILE, HEAD_DIM).swapaxes(-1, -2)
    vt = jnp.concatenate([vt, jnp.ones(vt.shape[:3] + (16, ATT_K_TILE), BF16)], axis=3)
    att = _attention(q, k, vt)

    zpad = jnp.zeros((2, D_LORA // 2, D_RWKV), F32)
    dup = jnp.concatenate([decay_up[li], zpad], axis=1)
    aup = jnp.concatenate([zpad, iclr_up[li]], axis=1)
    scan_shared, g, bonus, scan_dir = _rwkv_prep(
        rw, n_ctx, row(tshift_mu[li]), decay_w0[li], dup, iclr_a0[li], aup, gate_up[li],
        row(k_k[li]), row(k_a[li]), row(r_k[li]), bones)
    yf, yb = _wkv_scan(scan_shared, scan_dir, n_ctx)

    x1, h2, aff = _post(yf, yb, bonus, g, att, x, gt1, sh2, sc2, w_out[li].astype(BF16),
                        row(lnx_g[li]), row(lnx_b[li]), row(ln1_g[li]), row(ln1_b[li]),
                        router_w[li].T, bones, n_ctx)
    cap = CAPACITY_FACTOR * L // N_EXPERTS
    pos, gate, cum = _route(aff, cap, min(COMBINE_TILE, L))
    y = _experts(h2, pos, exp_w_gate[li].astype(BF16), exp_w_up[li].astype(BF16),
                 exp_w_down[li].astype(BF16), cap)
    return _combine(cum.reshape(-1), pos, gate, y, x1, gt2, row(ln2_g[li]), row(ln2_b[li]), cap)
```

```python
import functools

import jax
import jax.numpy as jnp
from jax import lax
from jax.experimental import pallas as pl
from jax.experimental.pallas import tpu as pltpu

F32 = jnp.float32
BF16 = jnp.bfloat16

HEAD_DIM = 64
ATT_HEADS = 8
ATT_KV_HEADS = 2
GQA_GROUP = ATT_HEADS // ATT_KV_HEADS
D_ATT = ATT_HEADS * HEAD_DIM
D_ATT_KV = ATT_KV_HEADS * HEAD_DIM
ATT_SCALE = HEAD_DIM ** -0.5
LOG2_E = 1.4426950408889634
EXP_M05 = 0.6065306597126334
ROPE_THETA = 10000.0
ROPE_PAIRS = HEAD_DIM // 4
GRID_W = 64
RWKV_HEAD = 64
D_RWKV = 512
D_LORA = 128
N_EXPERTS = 16
CAPACITY_FACTOR = 2
N_MOD = 6
LN_EPS = 1e-5
RMS_EPS = 1e-6
GN_EPS = 64e-5
L2_EPS = 1e-12
DEPTH = 1
ALPHA = (2.0 * DEPTH) ** 0.25

WKV_CHUNK = 64
WKV_BLOCK = 256
WKV_SAMPLES = 2
ROW_TILE = 256
POST_TILE = 512
ATT_Q_TILE = 128
ATT_K_TILE = 256
COMBINE_TILE = 1024
COMBINE_WINDOW = 256
COMBINE_EXPERTS = 8
VMEM_LIMIT = 48 * 1024 * 1024


def _params(sem):
    return pltpu.CompilerParams(dimension_semantics=sem, vmem_limit_bytes=VMEM_LIMIT)


def _split3(x):
    hi = x.astype(BF16)
    r1 = x - hi.astype(F32)
    mid = r1.astype(BF16)
    lo = (r1 - mid.astype(F32)).astype(BF16)
    return hi, mid, lo


def _split2(x):
    hi = x.astype(BF16)
    lo = (x - hi.astype(F32)).astype(BF16)
    return hi, lo


def _hi(x):
    return (x.astype(BF16),)


def _dg(a, b, dims):
    return lax.dot_general(a, b, (dims, ((), ())), preferred_element_type=F32)


_NN = ((1,), (0,))
_NT = ((1,), (1,))
_TN = ((0,), (0,))


def _mm(a, b, dims=_NN, passes=3):
    if passes == 3:
        ah, al = _split2(a)
        bh, bl = _split2(b)
        return _dg(ah, bh, dims) + (_dg(ah, bl, dims) + _dg(al, bh, dims))
    ah, am, al = _split3(a)
    bh, bm, bl = _split3(b)
    return (_dg(ah, bh, dims) + (_dg(ah, bm, dims) + _dg(am, bh, dims))
            + (_dg(am, bm, dims) + _dg(ah, bl, dims) + _dg(al, bh, dims)))


def _mmp(a, b, dims):
    out = _dg(a[0], b[0], dims)
    if len(a) > 1 and len(b) > 1:
        return out + (_dg(a[0], b[1], dims) + _dg(a[1], b[0], dims))
    if len(a) > 1:
        return out + _dg(a[1], b[0], dims)
    if len(b) > 1:
        return out + _dg(a[0], b[1], dims)
    return out


def _mm_exact_lhs(a_bf16, b, dims=_NN):
    bh, bm, bl = _split3(b)
    return _dg(a_bf16, bh, dims) + (_dg(a_bf16, bm, dims) + _dg(a_bf16, bl, dims))


def _mm_exact_rhs(a, b_bf16, dims=_NN):
    ah, al = _split2(a)
    return _dg(ah, b_bf16, dims) + _dg(al, b_bf16, dims)


def _sigmoid(x):
    return 1.0 / (1.0 + jnp.exp(-x))


def _layer_norm(x):
    mu = jnp.mean(x, axis=-1, keepdims=True)
    xc = x - mu
    var = jnp.mean(xc * xc, axis=-1, keepdims=True)
    return xc * lax.rsqrt(var + LN_EPS)


def _ada_kernel(c_ref, w_ref, b_ref, o_ref):
    c = c_ref[...]
    o_ref[...] = _mm(c * _sigmoid(c), w_ref[...], passes=6) + b_ref[...]


def _ada_mod(c_rows, w_ada, b_ada):
    R, D = c_rows.shape
    N = w_ada.shape[1]
    tn = 1024
    return pl.pallas_call(
        _ada_kernel,
        grid=(N // tn,),
        in_specs=[pl.BlockSpec((R, D), lambda j: (0, 0)),
                  pl.BlockSpec((D, tn), lambda j: (0, j)),
                  pl.BlockSpec((1, tn), lambda j: (0, j))],
        out_specs=pl.BlockSpec((R, tn), lambda j: (0, j)),
        out_shape=jax.ShapeDtypeStruct((R, N), F32),
        compiler_params=_params(("arbitrary",)),
        name="ada_mod",
    )(c_rows, w_ada, b_ada.reshape(1, N))


def _rope(t, cos, sin_lo, sin_hi, reps):
    w = t.shape[-1]
    tile = lambda a: jnp.concatenate([a] * reps, axis=1) if reps > 1 else a
    return (t * tile(cos) + pltpu.roll(t, w - ROPE_PAIRS, 1) * tile(sin_lo)
            + pltpu.roll(t, ROPE_PAIRS, 1) * tile(sin_hi))


def _inproj_kernel(x_ref, ctx_ref, sh_ref, sc_ref, w_ref, bones_ref, qg_ref, kg_ref,
                   cos_ref, slo_ref, shi_ref, q_ref, k_ref, v_ref, rw_ref, *, n_ctx_tiles):
    i = pl.program_id(1)
    x = jnp.where(i < n_ctx_tiles, ctx_ref[0], x_ref[0])
    h = _layer_norm(x) * (1.0 + sc_ref[0]) + sh_ref[0]
    p = _dg(h.astype(BF16), w_ref[...], _NN)
    bones = bones_ref[...]
    cos, slo, shi = cos_ref[...], slo_ref[...], shi_ref[...]

    q = p[:, :D_ATT]
    ss = _mm_exact_rhs(q * q, bones)
    q = q * lax.rsqrt(ss * (1.0 / HEAD_DIM) + RMS_EPS) * qg_ref[...]
    q = _rope(q, cos, slo, shi, D_ATT // 128) * (ATT_SCALE * LOG2_E)
    q_ref[0] = q.astype(BF16)

    k = p[:, D_ATT:D_ATT + D_ATT_KV]
    ss = _mm_exact_rhs(k * k, bones[:D_ATT_KV, :D_ATT_KV])
    k = k * lax.rsqrt(ss * (1.0 / HEAD_DIM) + RMS_EPS) * kg_ref[...]
    k = _rope(k, cos, slo, shi, D_ATT_KV // 128).astype(BF16)
    v = p[:, D_ATT + D_ATT_KV:D_ATT + 2 * D_ATT_KV].astype(BF16)
    for g in range(ATT_KV_HEADS):
        k_ref[0, g] = k[:, g * HEAD_DIM:(g + 1) * HEAD_DIM]
        v_ref[0, g] = v[:, g * HEAD_DIM:(g + 1) * HEAD_DIM]
    rw_ref[0] = p[:, D_ATT + 2 * D_ATT_KV:]


def _inproj(x, ctx, shift, scale, w_in_bf16, bones, qg, kg, cos, slo, shi):
    B, L, D = x.shape
    n_ctx = ctx.shape[1]
    tm = ROW_TILE
    nct = n_ctx // tm
    Tc = n_ctx + L
    d_in = w_in_bf16.shape[1]
    d_rw = d_in - D_ATT - 2 * D_ATT_KV
    mod_row = lambda b, i: (jnp.where(i < nct, B, b), 0, 0)
    const = lambda b, i: (0, 0)
    return pl.pallas_call(
        functools.partial(_inproj_kernel, n_ctx_tiles=nct),
        grid=(B, Tc // tm),
        in_specs=[pl.BlockSpec((1, tm, D), lambda b, i: (b, jnp.maximum(i - nct, 0), 0)),
                  pl.BlockSpec((1, tm, D), lambda b, i: (b, jnp.minimum(i, nct - 1), 0)),
                  pl.BlockSpec((1, 1, D), mod_row),
                  pl.BlockSpec((1, 1, D), mod_row),
                  pl.BlockSpec((D, d_in), const),
                  pl.BlockSpec(bones.shape, const),
                  pl.BlockSpec((1, D_ATT), const),
                  pl.BlockSpec((1, D_ATT_KV), const),
                  pl.BlockSpec((tm, 128), lambda b, i: (i, 0)),
                  pl.BlockSpec((tm, 128), lambda b, i: (i, 0)),
                  pl.BlockSpec((tm, 128), lambda b, i: (i, 0))],
        out_specs=[pl.BlockSpec((1, tm, D_ATT), lambda b, i: (b, jnp.maximum(i - nct, 0), 0)),
                   pl.BlockSpec((1, ATT_KV_HEADS, tm, HEAD_DIM), lambda b, i: (b, 0, i, 0)),
                   pl.BlockSpec((1, ATT_KV_HEADS, tm, HEAD_DIM), lambda b, i: (b, 0, i, 0)),
                   pl.BlockSpec((1, tm, d_rw), lambda b, i: (b, i, 0))],
        out_shape=[jax.ShapeDtypeStruct((B, L, D_ATT), BF16),
                   jax.ShapeDtypeStruct((B, ATT_KV_HEADS, Tc, HEAD_DIM), BF16),
                   jax.ShapeDtypeStruct((B, ATT_KV_HEADS, Tc, HEAD_DIM), BF16),
                   jax.ShapeDtypeStruct((B, Tc, d_rw), F32)],
        compiler_params=_params(("arbitrary", "arbitrary")),
        name="inproj",
    )(x, ctx, shift, scale, w_in_bf16, bones, qg, kg, cos, slo, shi)


def _attn_kernel(q_ref, k_ref, vt_ref, o_ref, s0, s1, p0, p1, m0, m1):
    i = pl.program_id(0)
    nk, tk, cols = s0.shape

    @pl.when(i == 0)
    def _():
        for ref in (s0, s1, m0, m1):
            ref[...] = jnp.zeros_like(ref)
        p0[...] = jnp.ones_like(p0)
        p1[...] = jnp.ones_like(p1)

    def step(s_w, m_w, s_r, m_r, p_w, p_r):
        q = q_ref[0]
        tq = q.shape[0]
        qs = jnp.concatenate([q[:, h * HEAD_DIM:(h + 1) * HEAD_DIM] for h in range(GQA_GROUP)], axis=0)
        m_prev = m_r[0:1, :]

        def body(j, carry):
            m_run, acc = carry
            off = pl.multiple_of(j * tk, tk)
            s_new = _dg(k_ref[0, 0, pl.ds(off, tk), :], qs, _NT)
            s_w[j] = s_new
            m_run = jnp.maximum(m_run, jnp.max(s_new.reshape(tk // 8, 8, cols), axis=0))
            p_w[j] = jnp.exp2(s_r[j] - m_prev).astype(BF16)
            acc = acc + _dg(vt_ref[0, 0, j], p_r[j], _NN)
            return m_run, acc

        init = (jnp.full((8, cols), -jnp.inf, F32), jnp.zeros((vt_ref.shape[3], cols), F32))
        m_run, acc = lax.fori_loop(0, nk, body, init, unroll=8)
        m_w[...] = jnp.broadcast_to(jnp.max(m_run, axis=0, keepdims=True), m_w.shape)
        o = acc[:HEAD_DIM] / acc[HEAD_DIM:HEAD_DIM + 1]
        o_ref[0] = jnp.concatenate([o[:, h * tq:(h + 1) * tq].T for h in range(GQA_GROUP)],
                                   axis=1).astype(BF16)

    @pl.when(i % 2 == 0)
    def _():
        step(s0, m0, s1, m1, p1, p0)

    @pl.when(i % 2 == 1)
    def _():
        step(s1, m1, s0, m0, p0, p1)


def _attention(q, k, vt):
    B, L, _ = q.shape
    Tc = k.shape[2]
    tq = ATT_Q_TILE
    n = L // tq
    gw = GQA_GROUP * HEAD_DIM
    cols = GQA_GROUP * tq
    tk = ATT_K_TILE
    nk = Tc // tk
    n_tiles = B * ATT_KV_HEADS * n

    def tile(s):
        t = jnp.clip(s, 0, n_tiles - 1)
        return t // (ATT_KV_HEADS * n), (t // n) % ATT_KV_HEADS, t % n

    def q_map(s):
        b, g, i = tile(s)
        return b, i, g

    def out_map(s):
        b, g, i = tile(s - 2)
        return b, i, g

    return pl.pallas_call(
        _attn_kernel,
        grid=(n_tiles + 2,),
        in_specs=[pl.BlockSpec((1, tq, gw), q_map),
                  pl.BlockSpec((1, 1, Tc, HEAD_DIM), lambda s: tile(s)[:2] + (0, 0)),
                  pl.BlockSpec((1, 1, nk, vt.shape[3], tk), lambda s: tile(s - 2)[:2] + (0, 0, 0))],
        out_specs=pl.BlockSpec((1, tq, gw), out_map),
        out_shape=jax.ShapeDtypeStruct((B, L, D_ATT), BF16),
        scratch_shapes=[pltpu.VMEM((nk, tk, cols), F32), pltpu.VMEM((nk, tk, cols), F32),
                        pltpu.VMEM((nk, tk, cols), BF16), pltpu.VMEM((nk, tk, cols), BF16),
                        pltpu.VMEM((8, cols), F32), pltpu.VMEM((8, cols), F32)],
        compiler_params=_params(("arbitrary",)),
        name="attention",
    )(q, k, vt)


def _rwkv_prep_kernel(rw_ref, prev_ref, next_ref, mu_ref, w0_ref, dup_ref, a0_ref, aup_ref, gup_ref,
                      kk_ref, ka_ref, rk_ref, bones_ref,
                      sh_o, g_o, bonus_o, dir_o, *, n_ctx, n_tok):
    i = pl.program_id(1)
    p = rw_ref[0]
    tm = p.shape[0]
    t0 = i * tm
    has_prev = jnp.where((t0 == 0) | (t0 == n_ctx), 0.0, 1.0)
    has_next = jnp.where((t0 + tm == n_ctx) | (t0 + tm == n_tok), 0.0, 1.0)
    row8 = lax.broadcasted_iota(jnp.int32, (8, 1), 0)
    up = pltpu.roll(p, 1, 0)
    dn = pltpu.roll(p, tm - 1, 0)
    up = jnp.concatenate([jnp.where(row8 == 0, prev_ref[0, 7:8, :] * has_prev, up[:8]), up[8:]], axis=0)
    dn = jnp.concatenate([dn[:tm - 8], jnp.where(row8 == 7, next_ref[0, 0:1, :] * has_next, dn[tm - 8:])], axis=0)
    mu = mu_ref[...]
    s = p * (1.0 - mu) + (up + dn) * (0.5 * mu)

    D = D_RWKV
    r, k, v = s[:, :D], s[:, D:2 * D], s[:, 2 * D:3 * D]
    lora = s[:, 3 * D:3 * D + D_LORA]
    gl = s[:, 3 * D + D_LORA:]
    bones = bones_ref[...]

    g_o[0] = _mm(_sigmoid(gl), gup_ref[...], passes=3)
    kk = k * kk_ref[...]
    nrm = jnp.sqrt(_mm_exact_rhs(kk * kk, bones))
    kk = kk / jnp.maximum(nrm, L2_EPS)
    wt = jnp.tanh(lora)
    sh_o[0, :, 0:D] = r
    sh_o[0, :, D:2 * D] = v
    sh_o[0, :, 2 * D:3 * D] = kk
    bonus_o[0] = _mm_exact_rhs(r * k * rk_ref[...], bones) * v
    for d in range(2):
        z = w0_ref[d:d + 1, :] + _mm(wt, dup_ref[d], passes=3)
        dir_o[d, 0, :, 0:D] = -EXP_M05 * _sigmoid(z)
        a = _sigmoid(a0_ref[d:d + 1, :] + _mm(lora, aup_ref[d], passes=3))
        dir_o[d, 0, :, D:2 * D] = k * (1.0 + (a - 1.0) * ka_ref[...])
        dir_o[d, 0, :, 2 * D:3 * D] = kk * a


def _rwkv_prep(rw, n_ctx, mu, w0, dup, a0, aup, gup, k_k, k_a, r_k, bones):
    B, Tc, d_rw = rw.shape
    tm = ROW_TILE
    D = D_RWKV
    const2 = lambda b, i: (0, 0)
    const3 = lambda b, i: (0, 0, 0)
    tok = lambda n: pl.BlockSpec((1, tm, n * D), lambda b, i: (b, i, 0))
    tok2 = pl.BlockSpec((2, 1, tm, 3 * D), lambda b, i: (0, b, i, 0))
    s1 = lambda n: jax.ShapeDtypeStruct((B, Tc, n * D), F32)
    s2 = jax.ShapeDtypeStruct((2, B, Tc, 3 * D), F32)
    hb = tm // 8
    return pl.pallas_call(
        functools.partial(_rwkv_prep_kernel, n_ctx=n_ctx, n_tok=Tc),
        grid=(B, Tc // tm),
        in_specs=[pl.BlockSpec((1, tm, d_rw), lambda b, i: (b, i, 0)),
                  pl.BlockSpec((1, 8, d_rw), lambda b, i: (b, jnp.maximum(i * hb - 1, 0), 0)),
                  pl.BlockSpec((1, 8, d_rw), lambda b, i: (b, jnp.minimum((i + 1) * hb, Tc // 8 - 1), 0)),
                  pl.BlockSpec((1, d_rw), const2),
                  pl.BlockSpec((2, D), const2),
                  pl.BlockSpec((2, D_LORA, D), const3),
                  pl.BlockSpec((2, D), const2),
                  pl.BlockSpec((2, D_LORA, D), const3),
                  pl.BlockSpec((D_LORA, D), const2),
                  pl.BlockSpec((1, D), const2),
                  pl.BlockSpec((1, D), const2),
                  pl.BlockSpec((1, D), const2),
                  pl.BlockSpec(bones.shape, const2)],
        out_specs=[tok(3), tok(1), tok(1), tok2],
        out_shape=[s1(3), s1(1), s1(1), s2],
        compiler_params=_params(("arbitrary", "arbitrary")),
        name="rwkv_prep",
    )(rw, rw, rw, mu, w0, dup, a0, aup, gup, k_k, k_a, r_k, bones)


def _wkv_kernel(sh0, dir0, sh1, dir1, y0, y1, s_ref, *, n_heads):
    @pl.when(pl.program_id(1) == 0)
    def _():
        s_ref[...] = jnp.zeros_like(s_ref)

    C = WKV_CHUNK
    n_sub = sh0.shape[1] // C
    lax.fori_loop(0, n_sub, functools.partial(_wkv_chunk, refs=(sh0, dir0, sh1, dir1, y0, y1, s_ref),
                                              n_heads=n_heads, n_sub=n_sub), 0, unroll=2)


def _wkv_chunk(sub, carry, *, refs, n_heads, n_sub):
    sh0, dir0, sh1, dir1, y0, y1, s_ref = refs
    C = WKV_CHUNK
    N = RWKV_HEAD
    D = n_heads * N
    row = lax.broadcasted_iota(jnp.int32, (C, 2 * N), 0)
    lane = lax.broadcasted_iota(jnp.int32, (C, 2 * N), 1)
    col = lane % N
    first_half = lane < N
    trow = lax.broadcasted_iota(jnp.int32, (C, C), 0)
    tcol = lax.broadcasted_iota(jnp.int32, (C, C), 1)
    cat = lambda xs, ys: tuple(jnp.concatenate([x, y], axis=0) for x, y in zip(xs, ys))
    offs = (pl.multiple_of(sub * C, C), pl.multiple_of((n_sub - 1 - sub) * C, C))
    y_refs = (y0, y1)

    chains = []
    for nb, d in [(nb, d) for nb in range(sh0.shape[0]) for d in range(2)]:
        sh_ref, dir_ref = ((sh0, dir0), (sh1, dir1))[d]
        rows = pl.ds(offs[d], C)
        strict, incl = (row < col, row <= col) if d else (row > col, row >= col)
        tri = (trow <= tcol) if d else (trow >= tcol)
        lw = dir_ref[0, nb, rows, 0:D]
        cl = _mm_exact_lhs(jnp.where(tri, 1.0, 0.0).astype(BF16), lw)
        ecl = jnp.exp(cl)
        eneg = jnp.exp(-cl)
        last = 0 if d else C - 1
        etot = ecl[last:last + 1, :]
        kt = dir_ref[0, nb, rows, D:2 * D] * eneg
        bt = dir_ref[0, nb, rows, 2 * D:3 * D] * eneg
        full = dict(at=_split2(-(sh_ref[nb, rows, 2 * D:3 * D] * jnp.exp(cl - lw))),
                    rt=_split2(sh_ref[nb, rows, 0:D] * ecl),
                    v=_split2(sh_ref[nb, rows, D:2 * D]),
                    kt=_hi(kt), bt=_hi(bt), kh=_hi(kt * etot), bh=_hi(bt * etot))
        for h in range(n_heads):
            ch = {k: tuple(p[:, h * N:(h + 1) * N] for p in val) for k, val in full.items()}
            ch.update(nb=nb, d=d, h=h, strict=strict, incl=incl, etot=etot[:, h * N:(h + 1) * N],
                      s0=s_ref[nb, d, h])
            chains.append(ch)

    for ch in chains:
        rhs = cat(cat(ch["kt"], ch["bt"]), _hi(ch["s0"]))
        g = _mmp(cat(ch["at"], ch["rt"]), rhs, _NT)
        ch["top"] = jnp.where(ch["strict"], g[:C, :2 * N], 0.0)
        ch["bot"] = jnp.where(ch["incl"], g[C:, :2 * N], 0.0)
        ch["as0"], ch["rs0"] = g[:C, 2 * N:], g[C:, 2 * N:]
    for ch in chains:
        z = ch["as0"] + _mmp(_split2(ch["top"][:, :N]), ch["v"][:1], _NN)
        ch["w"] = jnp.where(first_half, jnp.concatenate([z, jnp.zeros_like(z)], axis=1), ch["top"])
        ch["pw"] = pltpu.roll(ch["top"], N, 1)[:, :N]
    n = 1
    while True:
        for ch in chains:
            ch["prod"] = _mmp(_split2(ch["pw"]), _split2(ch["w"]), _NN)
            ch["w"] = jnp.where(first_half, ch["w"], 0.0) + ch["prod"]
        n *= 2
        if n >= C:
            break
        for ch in chains:
            ch["pw"] = pltpu.roll(ch["prod"], N, 1)[:, :N]
    for ch in chains:
        ch["up"] = _split2(ch["w"][:, :N])
        y = ch["rs0"] + _mmp(_split2(ch["bot"]), cat(ch["v"][:1], ch["up"][:1]), _NN)
        y_refs[ch["d"]][ch["nb"], pl.ds(offs[ch["d"]], C), ch["h"] * N:(ch["h"] + 1) * N] = y
    for ch in chains:
        upd = _mmp(cat(ch["up"], ch["v"]), cat(ch["bh"], ch["kh"]), _TN)
        s_ref[ch["nb"], ch["d"], ch["h"]] = ch["s0"] * ch["etot"] + upd
    return carry


def _wkv_scan(shared, per_dir, n_ctx):
    B, T, D3 = shared.shape
    D = D3 // 3
    tb = WKV_BLOCK
    nb = WKV_SAMPLES if B % WKV_SAMPLES == 0 else 1
    n_blocks = T // tb
    n_cb = n_ctx // tb
    tok = (lambda c: c,
           lambda c: jnp.where(c < n_cb, n_cb - 1 - c, n_blocks - 1 + n_cb - c))
    in_specs, args = [], []
    for d in range(2):
        in_specs += [pl.BlockSpec((nb, tb, D3), lambda bi, c, d=d: (bi, tok[d](c), 0)),
                     pl.BlockSpec((1, nb, tb, D3), lambda bi, c, d=d: (d, bi, tok[d](c), 0))]
        args += [shared, per_dir]
    y = jax.ShapeDtypeStruct((B, T, D), F32)
    return pl.pallas_call(
        functools.partial(_wkv_kernel, n_heads=D // RWKV_HEAD),
        grid=(B // nb, n_blocks),
        in_specs=in_specs,
        out_specs=[pl.BlockSpec((nb, tb, D), lambda bi, c, d=d: (bi, tok[d](c), 0)) for d in range(2)],
        out_shape=[y, y],
        scratch_shapes=[pltpu.VMEM((nb, 2, D // RWKV_HEAD, RWKV_HEAD, RWKV_HEAD), F32)],
        compiler_params=_params(("arbitrary", "arbitrary")),
        name="wkv_scan",
    )(*args)


def _post_kernel(*refs, k):
    yf, yb, bonus, gate = (refs[j * k:(j + 1) * k] for j in range(4))
    (att_ref, x_ref, gt_ref, sh_ref, sc_ref, wo_ref, lxg_ref, lxb_ref, l1g_ref, l1b_ref, rwt_ref, bones_ref,
     x1_ref, h2_ref, aff_ref) = refs[4 * k:]
    n = yf[0].shape[1]
    blocks = range(k)
    rows = [slice(j * n, (j + 1) * n) for j in blocks]
    bones = bones_ref[...]
    inv = 1.0 / RWKV_HEAD
    y = [yf[j][0] + yb[j][0] for j in blocks]
    mu = [_mm_exact_rhs(y[j], bones) * inv for j in blocks]
    yc = [y[j] - mu[j] for j in blocks]
    var = [_mm_exact_rhs(yc[j] * yc[j], bones) * inv for j in blocks]
    rw_out = [((yc[j] * lax.rsqrt(var[j] + GN_EPS) * lxg_ref[...] + lxb_ref[...] + bonus[j][0])
               * gate[j][0]).astype(BF16) for j in blocks]
    mix = [_dg(att_ref[0, rows[j], :], wo_ref[:D_ATT, :], _NN) + _dg(rw_out[j], wo_ref[D_ATT:, :], _NN)
           for j in blocks]
    x1 = [_layer_norm(ALPHA * x_ref[0, rows[j], :] + gt_ref[0] * mix[j]) * l1g_ref[...] + l1b_ref[...]
          for j in blocks]
    h2 = [_layer_norm(x1[j]) * (1.0 + sc_ref[0]) + sh_ref[0] for j in blocks]
    logits = [_mm(rwt_ref[...], h2[j], _NT, passes=3) for j in blocks]
    for j in blocks:
        x1_ref[0, rows[j], :] = x1[j]
        h2_ref[0, rows[j], :] = h2[j].astype(BF16)
        e = jnp.exp(logits[j] - jnp.max(logits[j], axis=0, keepdims=True))
        aff_ref[0, :, rows[j]] = e / jnp.sum(e, axis=0, keepdims=True)


def _post(yf, yb, bonus, g, att, x, gt1, sh2, sc2, w_out_bf16, lnx_g, lnx_b, ln1_g, ln1_b, router_wt,
          bones, n_ctx):
    B, L, D = x.shape
    tm = min(POST_TILE, L)
    k = tm // ROW_TILE
    nct = n_ctx // ROW_TILE
    E = router_wt.shape[0]
    const = lambda b, i: (0, 0)
    cat = [pl.BlockSpec((1, ROW_TILE, D_RWKV), lambda b, i, j=j: (b, i * k + j + nct, 0)) for j in range(k)]
    mod = pl.BlockSpec((1, 1, D), lambda b, i: (b, 0, 0))
    vec = lambda n: pl.BlockSpec((1, n), const)
    return pl.pallas_call(
        functools.partial(_post_kernel, k=k),
        grid=(B, L // tm),
        in_specs=cat * 4 + [
                  pl.BlockSpec((1, tm, D_ATT), lambda b, i: (b, i, 0)),
                  pl.BlockSpec((1, tm, D), lambda b, i: (b, i, 0)),
                  mod, mod, mod,
                  pl.BlockSpec(w_out_bf16.shape, const),
                  vec(D_RWKV), vec(D_RWKV), vec(D), vec(D),
                  pl.BlockSpec((E, D), const),
                  pl.BlockSpec(bones.shape, const)],
        out_specs=[pl.BlockSpec((1, tm, D), lambda b, i: (b, i, 0)),
                   pl.BlockSpec((1, tm, D), lambda b, i: (b, i, 0)),
                   pl.BlockSpec((1, E, tm), lambda b, i: (b, 0, i))],
        out_shape=[jax.ShapeDtypeStruct((B, L, D), F32),
                   jax.ShapeDtypeStruct((B, L, D), BF16),
                   jax.ShapeDtypeStruct((B, E, L), F32)],
        compiler_params=_params(("arbitrary", "arbitrary")),
        name="post_mix",
    )(*([yf] * k + [yb] * k + [bonus] * k + [g] * k), att, x, gt1, sh2, sc2, w_out_bf16, lnx_g, lnx_b,
      ln1_g, ln1_b, router_wt, bones)


def _cumsum_lanes(x):
    n = x.shape[1]
    lane = lax.broadcasted_iota(jnp.int32, x.shape, 1)
    s = 1
    while s < n:
        x = x + jnp.where(lane >= s, pltpu.roll(x, s, 1), 0)
        s *= 2
    return x


def _route_kernel(aff_ref, pos_ref, gate_ref, cum_ref, *, cap, tile):
    aff = aff_ref[0]
    E = aff.shape[0]
    count = lambda m: jnp.sum(jnp.where(m, 1, 0), axis=1, keepdims=True)
    thr_bits = jnp.zeros((E, 1), jnp.int32)
    for bit in range(29, -1, -1):
        cand = thr_bits | (1 << bit)
        cand_f = lax.bitcast_convert_type(cand, F32)
        thr_bits = jnp.where(count(aff >= cand_f) >= cap, cand, thr_bits)
    thr = lax.bitcast_convert_type(thr_bits, F32)
    above = aff > thr
    tie = aff == thr
    need = cap - count(above)
    tie_rank = _cumsum_lanes(jnp.where(tie, 1, 0))
    sel = above | (tie & (tie_rank <= need))
    slot = _cumsum_lanes(jnp.where(sel, 1, 0)) - 1
    pos = jnp.where(sel, slot, -1)
    gate = jnp.where(sel, aff, 0.0)
    for e in range(E):
        pos_ref[0, e] = pos[e:e + 1, :]
        gate_ref[0, e] = gate[e:e + 1, :]
    lane = lax.broadcasted_iota(jnp.int32, (E, 128), 1)
    chosen = jnp.where(sel, 1, 0)
    run = jnp.zeros((E, 1), jnp.int32)
    cum = jnp.zeros((E, 128), jnp.int32)
    n_tiles = aff.shape[1] // tile
    for i in range(n_tiles):
        cum = jnp.where(lane == i, run, cum)
        run = run + jnp.sum(chosen[:, i * tile:(i + 1) * tile], axis=1, keepdims=True)
    cum_ref[0] = jnp.where(lane == n_tiles, run, cum)


def _route(aff, cap, tile):
    B, E, L = aff.shape
    out = pl.BlockSpec((1, E, 1, L), lambda b: (b, 0, 0, 0))
    return pl.pallas_call(
        functools.partial(_route_kernel, cap=cap, tile=tile),
        grid=(B,),
        in_specs=[pl.BlockSpec((1, E, L), lambda b: (b, 0, 0))],
        out_specs=[out, out, pl.BlockSpec((1, E, 128), lambda b: (b, 0, 0))],
        out_shape=[jax.ShapeDtypeStruct((B, E, 1, L), jnp.int32),
                   jax.ShapeDtypeStruct((B, E, 1, L), F32),
                   jax.ShapeDtypeStruct((B, E, 128), jnp.int32)],
        compiler_params=_params(("arbitrary",)),
        name="route",
    )(aff)


def _expert_kernel(h_ref, pos_ref, wg_ref, wu_ref, wd_ref, y_ref, *, cap):
    pos = pos_ref[0, 0]
    slot = lax.broadcasted_iota(jnp.int32, (cap, pos.shape[1]), 0)
    onehot = jnp.where(pos == slot, 1.0, 0.0).astype(BF16)
    xin = _dg(onehot, h_ref[0], _NN).astype(BF16)
    hg = _dg(xin, wg_ref[0], _NN)
    hu = _dg(xin, wu_ref[0], _NN)
    hid = (hg * _sigmoid(hg) * hu).astype(BF16)
    y_ref[0, 0] = _dg(hid, wd_ref[0], _NN).astype(BF16)


def _experts(h2, pos, wg, wu, wd, cap):
    B, L, D = h2.shape
    E, _, F = wg.shape
    rowv = pl.BlockSpec((1, 1, 1, L), lambda b, e: (b, e, 0, 0))
    return pl.pallas_call(
        functools.partial(_expert_kernel, cap=cap),
        grid=(B, E),
        in_specs=[pl.BlockSpec((1, L, D), lambda b, e: (b, 0, 0)),
                  rowv,
                  pl.BlockSpec((1, D, F), lambda b, e: (e, 0, 0)),
                  pl.BlockSpec((1, D, F), lambda b, e: (e, 0, 0)),
                  pl.BlockSpec((1, F, D), lambda b, e: (e, 0, 0))],
        out_specs=pl.BlockSpec((1, 1, cap, D), lambda b, e: (b, e, 0, 0)),
        out_shape=jax.ShapeDtypeStruct((B, E, cap, D), BF16),
        compiler_params=_params(("arbitrary", "arbitrary")),
        name="experts",
    )(h2, pos, wg, wu, wd)


def _combine_kernel(cum_ref, pos_ref, gate_ref, y_ref, x1_ref, gt_ref, lg_ref, lb_ref, o_ref, acc_ref, *, cap):
    b, i, e = pl.program_id(0), pl.program_id(1), pl.program_id(2)
    n_group = pos_ref.shape[1]

    @pl.when(e == 0)
    def _():
        acc_ref[...] = jnp.zeros_like(acc_ref)

    window = min(COMBINE_WINDOW, cap)
    align = min(128, window)

    def accumulate(j, first, width):
        pos = pos_ref[0, j]
        slot = first + lax.broadcasted_iota(jnp.int32, (width, pos.shape[1]), 0)
        weights = jnp.where(pos == slot, gate_ref[0, j], 0.0).astype(BF16)
        acc_ref[...] += _dg(weights, y_ref[0, j, pl.ds(first, width), :], _TN)

    for j in range(n_group):
        base = ((b * pl.num_programs(2) + e) * n_group + j) * 128
        lo, hi = cum_ref[base + i], cum_ref[base + i + 1]
        start = pl.multiple_of(jnp.minimum((lo // align) * align, cap - window), align)
        pl.when((hi > lo) & (hi - start <= window))(functools.partial(accumulate, j, start, window))
        pl.when(hi - start > window)(functools.partial(accumulate, j, 0, cap))

    @pl.when(e == pl.num_programs(2) - 1)
    def _():
        z = ALPHA * x1_ref[0] + gt_ref[0] * acc_ref[...]
        o_ref[0] = _layer_norm(z) * lg_ref[...] + lb_ref[...]


def _combine(cum, pos, gate, y, x1, gt2, ln2_g, ln2_b, cap):
    B, L, D = x1.shape
    E = y.shape[1]
    tl = min(COMBINE_TILE, L)
    ge = COMBINE_EXPERTS
    const = lambda b, i, e, cum: (0, 0)
    grid_spec = pltpu.PrefetchScalarGridSpec(
        num_scalar_prefetch=1,
        grid=(B, L // tl, E // ge),
        in_specs=[pl.BlockSpec((1, ge, 1, tl), lambda b, i, e, cum: (b, e, 0, i)),
                  pl.BlockSpec((1, ge, 1, tl), lambda b, i, e, cum: (b, e, 0, i)),
                  pl.BlockSpec((1, ge, cap, D), lambda b, i, e, cum: (b, e, 0, 0)),
                  pl.BlockSpec((1, tl, D), lambda b, i, e, cum: (b, i, 0)),
                  pl.BlockSpec((1, 1, D), lambda b, i, e, cum: (b, 0, 0)),
                  pl.BlockSpec((1, D), const),
                  pl.BlockSpec((1, D), const)],
        out_specs=pl.BlockSpec((1, tl, D), lambda b, i, e, cum: (b, i, 0)),
        scratch_shapes=[pltpu.VMEM((tl, D), F32)])
    return pl.pallas_call(
        functools.partial(_combine_kernel, cap=cap),
        grid_spec=grid_spec,
        out_shape=jax.ShapeDtypeStruct((B, L, D), F32),
        compiler_params=_params(("arbitrary", "arbitrary", "arbitrary")),
        name="combine",
    )(cum, pos, gate, y, x1, gt2, ln2_g, ln2_b)


def _rope_tables(n_ctx, n_lat):
    lane = jnp.arange(128)
    j = lane % HEAD_DIM
    axis = j // (2 * ROPE_PAIRS)
    upper = (j % (2 * ROPE_PAIRS)) >= ROPE_PAIRS
    inv = ROPE_THETA ** (-(j % ROPE_PAIRS).astype(F32) / ROPE_PAIRS)
    t = jnp.arange(n_lat)
    coord = jnp.where(axis[None, :] == 0, (t // GRID_W)[:, None], (t % GRID_W)[:, None]).astype(F32)
    ang = coord * inv[None, :]
    cos, sin = jnp.cos(ang), jnp.sin(ang)
    s_lo = jnp.where(upper[None, :], 0.0, -sin)
    s_hi = jnp.where(upper[None, :], sin, 0.0)
    pad = lambda a, fill: jnp.concatenate([jnp.full((n_ctx, 128), fill, F32), a], axis=0)
    return pad(cos, 1.0), pad(s_lo, 0.0), pad(s_hi, 0.0)


def _head_ones(n):
    h = jnp.arange(n) // HEAD_DIM
    return (h[:, None] == h[None, :]).astype(BF16)


def kernel(x, c, ctx, c_ctx, w_ada, b_ada, w_in, q_gain, k_gain, tshift_mu, decay_w0, decay_up, iclr_a0,
           iclr_up, gate_up, k_k, k_a, r_k, lnx_g, lnx_b, w_out, ln1_g, ln1_b, router_w, exp_w_gate,
           exp_w_up, exp_w_down, ln2_g, ln2_b):
    B, L, D = x.shape
    n_ctx = ctx.shape[1]
    li = 0
    row = lambda a: a.reshape(1, -1)

    n_rows = -(-(B + 1) // 8) * 8
    c_rows = jnp.concatenate([c, c_ctx[None, :], jnp.zeros((n_rows - B - 1, D), F32)], axis=0)
    mod = _ada_mod(c_rows, w_ada[li], b_ada[li]).reshape(n_rows, N_MOD, 1, D)
    sh1, sc1, gt1, sh2, sc2, gt2 = (mod[:, m] for m in range(N_MOD))

    bones = _head_ones(D_RWKV)
    cos, s_lo, s_hi = _rope_tables(n_ctx, L)
    q, k, v, rw = _inproj(x, ctx, sh1, sc1, w_in[li].astype(BF16), bones,
                          row(jnp.tile(q_gain[li], ATT_HEADS)), row(jnp.tile(k_gain[li], ATT_KV_HEADS)),
                          cos, s_lo, s_hi)
    Tc = n_ctx + L
    vt = v.reshape(B, ATT_KV_HEADS, Tc // ATT_K_TILE, ATT_K_TILE, HEAD_DIM).swapaxes(-1, -2)
    vt = jnp.concatenate([vt, jnp.ones(vt.shape[:3] + (16, ATT_K_TILE), BF16)], axis=3)
    att = _attention(q, k, vt)

    zpad = jnp.zeros((2, D_LORA // 2, D_RWKV), F32)
    dup = jnp.concatenate([decay_up[li], zpad], axis=1)
    aup = jnp.concatenate([zpad, iclr_up[li]], axis=1)
    scan_shared, g, bonus, scan_dir = _rwkv_prep(
        rw, n_ctx, row(tshift_mu[li]), decay_w0[li], dup, iclr_a0[li], aup, gate_up[li],
        row(k_k[li]), row(k_a[li]), row(r_k[li]), bones)
    yf, yb = _wkv_scan(scan_shared, scan_dir, n_ctx)

    x1, h2, aff = _post(yf, yb, bonus, g, att, x, gt1, sh2, sc2, w_out[li].astype(BF16),
                        row(lnx_g[li]), row(lnx_b[li]), row(ln1_g[li]), row(ln1_b[li]),
                        router_w[li].T, bones, n_ctx)
    cap = CAPACITY_FACTOR * L // N_EXPERTS
    pos, gate, cum = _route(aff, cap, min(COMBINE_TILE, L))
    y = _experts(h2, pos, exp_w_gate[li].astype(BF16), exp_w_up[li].astype(BF16),
                 exp_w_down[li].astype(BF16), cap)
    return _combine(cum.reshape(-1), pos, gate, y, x1, gt2, row(ln2_g[li]), row(ln2_b[li]), cap)
```

```python
import functools

import jax
import jax.numpy as jnp
from jax import lax
from jax.experimental import pallas as pl
from jax.experimental.pallas import tpu as pltpu

F32 = jnp.float32
BF16 = jnp.bfloat16

HEAD_DIM = 64
ATT_HEADS = 8
ATT_KV_HEADS = 2
GQA_GROUP = ATT_HEADS // ATT_KV_HEADS
D_ATT = ATT_HEADS * HEAD_DIM
D_ATT_KV = ATT_KV_HEADS * HEAD_DIM
ATT_SCALE = HEAD_DIM ** -0.5
LOG2_E = 1.4426950408889634
EXP_M05 = 0.6065306597126334
ROPE_THETA = 10000.0
ROPE_PAIRS = HEAD_DIM // 4
GRID_W = 64
RWKV_HEAD = 64
D_RWKV = 512
D_LORA = 128
N_EXPERTS = 16
CAPACITY_FACTOR = 2
N_MOD = 6
LN_EPS = 1e-5
RMS_EPS = 1e-6
GN_EPS = 64e-5
L2_EPS = 1e-12
DEPTH = 1
ALPHA = (2.0 * DEPTH) ** 0.25

WKV_CHUNK = 64
WKV_BLOCK = 256
WKV_SAMPLES = 2
ROW_TILE = 256
POST_TILE = 512
ATT_Q_TILE = 128
ATT_K_TILE = 256
COMBINE_TILE = 1024
COMBINE_WINDOW = 256
COMBINE_EXPERTS = 8
VMEM_LIMIT = 48 * 1024 * 1024


def _params(sem):
    return pltpu.CompilerParams(dimension_semantics=sem, vmem_limit_bytes=VMEM_LIMIT)


def _split3(x):
    hi = x.astype(BF16)
    r1 = x - hi.astype(F32)
    mid = r1.astype(BF16)
    lo = (r1 - mid.astype(F32)).astype(BF16)
    return hi, mid, lo


def _split2(x):
    hi = x.astype(BF16)
    lo = (x - hi.astype(F32)).astype(BF16)
    return hi, lo


def _hi(x):
    return (x.astype(BF16),)


def _dg(a, b, dims):
    return lax.dot_general(a, b, (dims, ((), ())), preferred_element_type=F32)


_NN = ((1,), (0,))
_NT = ((1,), (1,))
_TN = ((0,), (0,))


def _mm(a, b, dims=_NN, passes=3):
    if passes == 3:
        ah, al = _split2(a)
        bh, bl = _split2(b)
        return _dg(ah, bh, dims) + (_dg(ah, bl, dims) + _dg(al, bh, dims))
    ah, am, al = _split3(a)
    bh, bm, bl = _split3(b)
    return (_dg(ah, bh, dims) + (_dg(ah, bm, dims) + _dg(am, bh, dims))
            + (_dg(am, bm, dims) + _dg(ah, bl, dims) + _dg(al, bh, dims)))


def _mmp(a, b, dims):
    out = _dg(a[0], b[0], dims)
    if len(a) > 1 and len(b) > 1:
        return out + (_dg(a[0], b[1], dims) + _dg(a[1], b[0], dims))
    if len(a) > 1:
        return out + _dg(a[1], b[0], dims)
    if len(b) > 1:
        return out + _dg(a[0], b[1], dims)
    return out


def _mm_exact_lhs(a_bf16, b, dims=_NN):
    bh, bm, bl = _split3(b)
    return _dg(a_bf16, bh, dims) + (_dg(a_bf16, bm, dims) + _dg(a_bf16, bl, dims))


def _mm_exact_rhs(a, b_bf16, dims=_NN):
    ah, al = _split2(a)
    return _dg(ah, b_bf16, dims) + _dg(al, b_bf16, dims)


def _sigmoid(x):
    return 1.0 / (1.0 + jnp.exp(-x))


def _layer_norm(x):
    mu = jnp.mean(x, axis=-1, keepdims=True)
    xc = x - mu
    var = jnp.mean(xc * xc, axis=-1, keepdims=True)
    return xc * lax.rsqrt(var + LN_EPS)


def _ada_kernel(c_ref, w_ref, b_ref, o_ref):
    c = c_ref[...]
    o_ref[...] = _mm(c * _sigmoid(c), w_ref[...], passes=6) + b_ref[...]


def _ada_mod(c_rows, w_ada, b_ada):
    R, D = c_rows.shape
    N = w_ada.shape[1]
    tn = 1024
    return pl.pallas_call(
        _ada_kernel,
        grid=(N // tn,),
        in_specs=[pl.BlockSpec((R, D), lambda j: (0, 0)),
                  pl.BlockSpec((D, tn), lambda j: (0, j)),
                  pl.BlockSpec((1, tn), lambda j: (0, j))],
        out_specs=pl.BlockSpec((R, tn), lambda j: (0, j)),
        out_shape=jax.ShapeDtypeStruct((R, N), F32),
        compiler_params=_params(("arbitrary",)),
        name="ada_mod",
    )(c_rows, w_ada, b_ada.reshape(1, N))


def _rope(t, cos, sin_lo, sin_hi, reps):
    w = t.shape[-1]
    tile = lambda a: jnp.concatenate([a] * reps, axis=1) if reps > 1 else a
    return (t * tile(cos) + pltpu.roll(t, w - ROPE_PAIRS, 1) * tile(sin_lo)
            + pltpu.roll(t, ROPE_PAIRS, 1) * tile(sin_hi))


def _inproj_kernel(x_ref, ctx_ref, sh_ref, sc_ref, w_ref, bones_ref, qg_ref, kg_ref,
                   cos_ref, slo_ref, shi_ref, q_ref, k_ref, v_ref, rw_ref, *, n_ctx_tiles):
    i = pl.program_id(1)
    x = jnp.where(i < n_ctx_tiles, ctx_ref[0], x_ref[0])
    h = _layer_norm(x) * (1.0 + sc_ref[0]) + sh_ref[0]
    p = _dg(h.astype(BF16), w_ref[...], _NN)
    bones = bones_ref[...]
    cos, slo, shi = cos_ref[...], slo_ref[...], shi_ref[...]

    q = p[:, :D_ATT]
    ss = _mm_exact_rhs(q * q, bones)
    q = q * lax.rsqrt(ss * (1.0 / HEAD_DIM) + RMS_EPS) * qg_ref[...]
    q = _rope(q, cos, slo, shi, D_ATT // 128) * (ATT_SCALE * LOG2_E)
    q_ref[0] = q.astype(BF16)

    k = p[:, D_ATT:D_ATT + D_ATT_KV]
    ss = _mm_exact_rhs(k * k, bones[:D_ATT_KV, :D_ATT_KV])
    k = k * lax.rsqrt(ss * (1.0 / HEAD_DIM) + RMS_EPS) * kg_ref[...]
    k = _rope(k, cos, slo, shi, D_ATT_KV // 128).astype(BF16)
    v = p[:, D_ATT + D_ATT_KV:D_ATT + 2 * D_ATT_KV].astype(BF16)
    for g in range(ATT_KV_HEADS):
        k_ref[0, g] = k[:, g * HEAD_DIM:(g + 1) * HEAD_DIM]
        v_ref[0, g] = v[:, g * HEAD_DIM:(g + 1) * HEAD_DIM]
    rw_ref[0] = p[:, D_ATT + 2 * D_ATT_KV:]


def _inproj(x, ctx, shift, scale, w_in_bf16, bones, qg, kg, cos, slo, shi):
    B, L, D = x.shape
    n_ctx = ctx.shape[1]
    tm = ROW_TILE
    nct = n_ctx // tm
    Tc = n_ctx + L
    d_in = w_in_bf16.shape[1]
    d_rw = d_in - D_ATT - 2 * D_ATT_KV
    mod_row = lambda b, i: (jnp.where(i < nct, B, b), 0, 0)
    const = lambda b, i: (0, 0)
    return pl.pallas_call(
        functools.partial(_inproj_kernel, n_ctx_tiles=nct),
        grid=(B, Tc // tm),
        in_specs=[pl.BlockSpec((1, tm, D), lambda b, i: (b, jnp.maximum(i - nct, 0), 0)),
                  pl.BlockSpec((1, tm, D), lambda b, i: (b, jnp.minimum(i, nct - 1), 0)),
                  pl.BlockSpec((1, 1, D), mod_row),
                  pl.BlockSpec((1, 1, D), mod_row),
                  pl.BlockSpec((D, d_in), const),
                  pl.BlockSpec(bones.shape, const),
                  pl.BlockSpec((1, D_ATT), const),
                  pl.BlockSpec((1, D_ATT_KV), const),
                  pl.BlockSpec((tm, 128), lambda b, i: (i, 0)),
                  pl.BlockSpec((tm, 128), lambda b, i: (i, 0)),
                  pl.BlockSpec((tm, 128), lambda b, i: (i, 0))],
        out_specs=[pl.BlockSpec((1, tm, D_ATT), lambda b, i: (b, jnp.maximum(i - nct, 0), 0)),
                   pl.BlockSpec((1, ATT_KV_HEADS, tm, HEAD_DIM), lambda b, i: (b, 0, i, 0)),
                   pl.BlockSpec((1, ATT_KV_HEADS, tm, HEAD_DIM), lambda b, i: (b, 0, i, 0)),
                   pl.BlockSpec((1, tm, d_rw), lambda b, i: (b, i, 0))],
        out_shape=[jax.ShapeDtypeStruct((B, L, D_ATT), BF16),
                   jax.ShapeDtypeStruct((B, ATT_KV_HEADS, Tc, HEAD_DIM), BF16),
                   jax.ShapeDtypeStruct((B, ATT_KV_HEADS, Tc, HEAD_DIM), BF16),
                   jax.ShapeDtypeStruct((B, Tc, d_rw), F32)],
        compiler_params=_params(("arbitrary", "arbitrary")),
        name="inproj",
    )(x, ctx, shift, scale, w_in_bf16, bones, qg, kg, cos, slo, shi)


def _attn_kernel(q_ref, k_ref, vt_ref, o_ref, s0, s1, p0, p1, m0, m1):
    i = pl.program_id(0)
    nk, tk, cols = s0.shape

    @pl.when(i == 0)
    def _():
        for ref in (s0, s1, m0, m1):
            ref[...] = jnp.zeros_like(ref)
        p0[...] = jnp.ones_like(p0)
        p1[...] = jnp.ones_like(p1)

    def step(s_w, m_w, s_r, m_r, p_w, p_r):
        q = q_ref[0]
        tq = q.shape[0]
        qs = jnp.concatenate([q[:, h * HEAD_DIM:(h + 1) * HEAD_DIM] for h in range(GQA_GROUP)], axis=0)
        m_prev = m_r[0:1, :]

        def body(j, carry):
            m_run, acc = carry
            off = pl.multiple_of(j * tk, tk)
            s_new = _dg(k_ref[0, 0, pl.ds(off, tk), :], qs, _NT)
            s_w[j] = s_new
            m_run = jnp.maximum(m_run, jnp.max(s_new.reshape(tk // 8, 8, cols), axis=0))
            p_w[j] = jnp.exp2(s_r[j] - m_prev).astype(BF16)
            acc = acc + _dg(vt_ref[0, 0, j], p_r[j], _NN)
            return m_run, acc

        init = (jnp.full((8, cols), -jnp.inf, F32), jnp.zeros((vt_ref.shape[3], cols), F32))
        m_run, acc = lax.fori_loop(0, nk, body, init, unroll=8)
        m_w[...] = jnp.broadcast_to(jnp.max(m_run, axis=0, keepdims=True), m_w.shape)
        o = acc[:HEAD_DIM] / acc[HEAD_DIM:HEAD_DIM + 1]
        o_ref[0] = jnp.concatenate([o[:, h * tq:(h + 1) * tq].T for h in range(GQA_GROUP)],
                                   axis=1).astype(BF16)

    @pl.when(i % 2 == 0)
    def _():
        step(s0, m0, s1, m1, p1, p0)

    @pl.when(i % 2 == 1)
    def _():
        step(s1, m1, s0, m0, p0, p1)


def _attention(q, k, vt):
    B, L, _ = q.shape
    Tc = k.shape[2]
    tq = ATT_Q_TILE
    n = L // tq
    gw = GQA_GROUP * HEAD_DIM
    cols = GQA_GROUP * tq
    tk = ATT_K_TILE
    nk = Tc // tk
    n_tiles = B * ATT_KV_HEADS * n

    def tile(s):
        t = jnp.clip(s, 0, n_tiles - 1)
        return t // (ATT_KV_HEADS * n), (t // n) % ATT_KV_HEADS, t % n

    def q_map(s):
        b, g, i = tile(s)
        return b, i, g

    def out_map(s):
        b, g, i = tile(s - 2)
        return b, i, g

    return pl.pallas_call(
        _attn_kernel,
        grid=(n_tiles + 2,),
        in_specs=[pl.BlockSpec((1, tq, gw), q_map),
                  pl.BlockSpec((1, 1, Tc, HEAD_DIM), lambda s: tile(s)[:2] + (0, 0)),
                  pl.BlockSpec((1, 1, nk, vt.shape[3], tk), lambda s: tile(s - 2)[:2] + (0, 0, 0))],
        out_specs=pl.BlockSpec((1, tq, gw), out_map),
        out_shape=jax.ShapeDtypeStruct((B, L, D_ATT), BF16),
        scratch_shapes=[pltpu.VMEM((nk, tk, cols), F32), pltpu.VMEM((nk, tk, cols), F32),
                        pltpu.VMEM((nk, tk, cols), BF16), pltpu.VMEM((nk, tk, cols), BF16),
                        pltpu.VMEM((8, cols), F32), pltpu.VMEM((8, cols), F32)],
        compiler_params=_params(("arbitrary",)),
        name="attention",
    )(q, k, vt)


def _rwkv_prep_kernel(rw_ref, prev_ref, next_ref, mu_ref, w0_ref, dup_ref, a0_ref, aup_ref, gup_ref,
                      kk_ref, ka_ref, rk_ref, bones_ref,
                      sh_o, g_o, bonus_o, lw_o, kb_o, *, n_ctx, n_tok):
    i = pl.program_id(1)
    p = rw_ref[0]
    tm = p.shape[0]
    t0 = i * tm
    has_prev = jnp.where((t0 == 0) | (t0 == n_ctx), 0.0, 1.0)
    has_next = jnp.where((t0 + tm == n_ctx) | (t0 + tm == n_tok), 0.0, 1.0)
    row8 = lax.broadcasted_iota(jnp.int32, (8, 1), 0)
    up = pltpu.roll(p, 1, 0)
    dn = pltpu.roll(p, tm - 1, 0)
    up = jnp.concatenate([jnp.where(row8 == 0, prev_ref[0, 7:8, :] * has_prev, up[:8]), up[8:]], axis=0)
    dn = jnp.concatenate([dn[:tm - 8], jnp.where(row8 == 7, next_ref[0, 0:1, :] * has_next, dn[tm - 8:])], axis=0)
    mu = mu_ref[...]
    s = p * (1.0 - mu) + (up + dn) * (0.5 * mu)

    D = D_RWKV
    r, k, v = s[:, :D], s[:, D:2 * D], s[:, 2 * D:3 * D]
    lora = s[:, 3 * D:3 * D + D_LORA]
    gl = s[:, 3 * D + D_LORA:]
    bones = bones_ref[...]

    g_o[0] = _mm(_sigmoid(gl), gup_ref[...], passes=3)
    kk = k * kk_ref[...]
    nrm = jnp.sqrt(_mm_exact_rhs(kk * kk, bones))
    kk = kk / jnp.maximum(nrm, L2_EPS)
    wt = jnp.tanh(lora)
    sh_o[0, :, 0:D] = r
    sh_o[0, :, D:2 * D] = v
    sh_o[0, :, 2 * D:3 * D] = kk
    bonus_o[0] = _mm_exact_rhs(r * k * rk_ref[...], bones) * v
    for d in range(2):
        z = w0_ref[d:d + 1, :] + _mm(wt, dup_ref[d], passes=3)
        lw_o[d, 0] = -EXP_M05 * _sigmoid(z)
        a = _sigmoid(a0_ref[d:d + 1, :] + _mm(lora, aup_ref[d], passes=3))
        kb_o[d, 0, :, 0:D] = (k * (1.0 + (a - 1.0) * ka_ref[...])).astype(BF16)
        kb_o[d, 0, :, D:2 * D] = (kk * a).astype(BF16)


def _rwkv_prep(rw, n_ctx, mu, w0, dup, a0, aup, gup, k_k, k_a, r_k, bones):
    B, Tc, d_rw = rw.shape
    tm = ROW_TILE
    D = D_RWKV
    const2 = lambda b, i: (0, 0)
    const3 = lambda b, i: (0, 0, 0)
    tok = lambda n: pl.BlockSpec((1, tm, n * D), lambda b, i: (b, i, 0))
    tok2 = lambda n: pl.BlockSpec((2, 1, tm, n * D), lambda b, i: (0, b, i, 0))
    s1 = lambda n: jax.ShapeDtypeStruct((B, Tc, n * D), F32)
    s2 = lambda n, dt: jax.ShapeDtypeStruct((2, B, Tc, n * D), dt)
    hb = tm // 8
    return pl.pallas_call(
        functools.partial(_rwkv_prep_kernel, n_ctx=n_ctx, n_tok=Tc),
        grid=(B, Tc // tm),
        in_specs=[pl.BlockSpec((1, tm, d_rw), lambda b, i: (b, i, 0)),
                  pl.BlockSpec((1, 8, d_rw), lambda b, i: (b, jnp.maximum(i * hb - 1, 0), 0)),
                  pl.BlockSpec((1, 8, d_rw), lambda b, i: (b, jnp.minimum((i + 1) * hb, Tc // 8 - 1), 0)),
                  pl.BlockSpec((1, d_rw), const2),
                  pl.BlockSpec((2, D), const2),
                  pl.BlockSpec((2, D_LORA, D), const3),
                  pl.BlockSpec((2, D), const2),
                  pl.BlockSpec((2, D_LORA, D), const3),
                  pl.BlockSpec((D_LORA, D), const2),
                  pl.BlockSpec((1, D), const2),
                  pl.BlockSpec((1, D), const2),
                  pl.BlockSpec((1, D), const2),
                  pl.BlockSpec(bones.shape, const2)],
        out_specs=[tok(3), tok(1), tok(1), tok2(1), tok2(2)],
        out_shape=[s1(3), s1(1), s1(1), s2(1, F32), s2(2, BF16)],
        compiler_params=_params(("arbitrary", "arbitrary")),
        name="rwkv_prep",
    )(rw, rw, rw, mu, w0, dup, a0, aup, gup, k_k, k_a, r_k, bones)


def _wkv_kernel(sh0, lw0, kb0, sh1, lw1, kb1, y0, y1, s_ref, *, n_heads):
    @pl.when(pl.program_id(1) == 0)
    def _():
        s_ref[...] = jnp.zeros_like(s_ref)

    C = WKV_CHUNK
    n_sub = sh0.shape[1] // C
    lax.fori_loop(0, n_sub, functools.partial(_wkv_chunk, refs=(sh0, lw0, kb0, sh1, lw1, kb1, y0, y1, s_ref),
                                              n_heads=n_heads, n_sub=n_sub), 0, unroll=2)


def _wkv_chunk(sub, carry, *, refs, n_heads, n_sub):
    sh0, lw0, kb0, sh1, lw1, kb1, y0, y1, s_ref = refs
    C = WKV_CHUNK
    N = RWKV_HEAD
    D = n_heads * N
    row = lax.broadcasted_iota(jnp.int32, (C, 2 * N), 0)
    lane = lax.broadcasted_iota(jnp.int32, (C, 2 * N), 1)
    col = lane % N
    first_half = lane < N
    trow = lax.broadcasted_iota(jnp.int32, (C, C), 0)
    tcol = lax.broadcasted_iota(jnp.int32, (C, C), 1)
    cat = lambda xs, ys: tuple(jnp.concatenate([x, y], axis=0) for x, y in zip(xs, ys))
    offs = (pl.multiple_of(sub * C, C), pl.multiple_of((n_sub - 1 - sub) * C, C))
    y_refs = (y0, y1)

    chains = []
    for nb, d in [(nb, d) for nb in range(sh0.shape[0]) for d in range(2)]:
        sh_ref, lw_ref, kb_ref = ((sh0, lw0, kb0), (sh1, lw1, kb1))[d]
        rows = pl.ds(offs[d], C)
        strict, incl = (row < col, row <= col) if d else (row > col, row >= col)
        tri = (trow <= tcol) if d else (trow >= tcol)
        lw = lw_ref[0, nb, rows, :]
        cl = _mm_exact_lhs(jnp.where(tri, 1.0, 0.0).astype(BF16), lw)
        ecl = jnp.exp(cl)
        eneg = jnp.exp(-cl)
        last = 0 if d else C - 1
        etot = ecl[last:last + 1, :]
        kt = kb_ref[0, nb, rows, 0:D].astype(F32) * eneg
        bt = kb_ref[0, nb, rows, D:2 * D].astype(F32) * eneg
        full = dict(at=_split2(-(sh_ref[nb, rows, 2 * D:3 * D] * jnp.exp(cl - lw))),
                    rt=_split2(sh_ref[nb, rows, 0:D] * ecl),
                    v=_split2(sh_ref[nb, rows, D:2 * D]),
                    kt=_hi(kt), bt=_hi(bt), kh=_hi(kt * etot), bh=_hi(bt * etot))
        for h in range(n_heads):
            ch = {k: tuple(p[:, h * N:(h + 1) * N] for p in val) for k, val in full.items()}
            ch.update(nb=nb, d=d, h=h, strict=strict, incl=incl, etot=etot[:, h * N:(h + 1) * N],
                      s0=s_ref[nb, d, h])
            chains.append(ch)

    for ch in chains:
        rhs = cat(cat(ch["kt"], ch["bt"]), _hi(ch["s0"]))
        g = _mmp(cat(ch["at"], ch["rt"]), rhs, _NT)
        ch["top"] = jnp.where(ch["strict"], g[:C, :2 * N], 0.0)
        ch["bot"] = jnp.where(ch["incl"], g[C:, :2 * N], 0.0)
        ch["as0"], ch["rs0"] = g[:C, 2 * N:], g[C:, 2 * N:]
    for ch in chains:
        z = ch["as0"] + _mmp(_split2(ch["top"][:, :N]), ch["v"][:1], _NN)
        ch["w"] = jnp.where(first_half, jnp.concatenate([z, jnp.zeros_like(z)], axis=1), ch["top"])
        ch["pw"] = pltpu.roll(ch["top"], N, 1)[:, :N]
    n = 1
    while True:
        for ch in chains:
            ch["prod"] = _mmp(_split2(ch["pw"]), _split2(ch["w"]), _NN)
            ch["w"] = jnp.where(first_half, ch["w"], 0.0) + ch["prod"]
        n *= 2
        if n >= C:
            break
        for ch in chains:
            ch["pw"] = pltpu.roll(ch["prod"], N, 1)[:, :N]
    for ch in chains:
        ch["up"] = _split2(ch["w"][:, :N])
        y = ch["rs0"] + _mmp(_split2(ch["bot"]), cat(ch["v"][:1], ch["up"][:1]), _NN)
        y_refs[ch["d"]][ch["nb"], pl.ds(offs[ch["d"]], C), ch["h"] * N:(ch["h"] + 1) * N] = y
    for ch in chains:
        upd = _mmp(cat(ch["up"], ch["v"]), cat(ch["bh"], ch["kh"]), _TN)
        s_ref[ch["nb"], ch["d"], ch["h"]] = ch["s0"] * ch["etot"] + upd
    return carry


def _wkv_scan(shared, log_decay, kb, n_ctx):
    B, T, D3 = shared.shape
    D = D3 // 3
    tb = WKV_BLOCK
    nb = WKV_SAMPLES if B % WKV_SAMPLES == 0 else 1
    n_blocks = T // tb
    n_cb = n_ctx // tb
    tok = (lambda c: c,
           lambda c: jnp.where(c < n_cb, n_cb - 1 - c, n_blocks - 1 + n_cb - c))
    in_specs, args = [], []
    for d in range(2):
        in_specs += [pl.BlockSpec((nb, tb, D3), lambda bi, c, d=d: (bi, tok[d](c), 0)),
                     pl.BlockSpec((1, nb, tb, D), lambda bi, c, d=d: (d, bi, tok[d](c), 0)),
                     pl.BlockSpec((1, nb, tb, 2 * D), lambda bi, c, d=d: (d, bi, tok[d](c), 0))]
        args += [shared, log_decay, kb]
    y = jax.ShapeDtypeStruct((B, T, D), F32)
    return pl.pallas_call(
        functools.partial(_wkv_kernel, n_heads=D // RWKV_HEAD),
        grid=(B // nb, n_blocks),
        in_specs=in_specs,
        out_specs=[pl.BlockSpec((nb, tb, D), lambda bi, c, d=d: (bi, tok[d](c), 0)) for d in range(2)],
        out_shape=[y, y],
        scratch_shapes=[pltpu.VMEM((nb, 2, D // RWKV_HEAD, RWKV_HEAD, RWKV_HEAD), F32)],
        compiler_params=_params(("arbitrary", "arbitrary")),
        name="wkv_scan",
    )(*args)


def _post_kernel(*refs, k):
    yf, yb, bonus, gate = (refs[j * k:(j + 1) * k] for j in range(4))
    (att_ref, x_ref, gt_ref, sh_ref, sc_ref, wo_ref, lxg_ref, lxb_ref, l1g_ref, l1b_ref, rwt_ref, bones_ref,
     x1_ref, h2_ref, aff_ref) = refs[4 * k:]
    n = yf[0].shape[1]
    blocks = range(k)
    rows = [slice(j * n, (j + 1) * n) for j in blocks]
    bones = bones_ref[...]
    inv = 1.0 / RWKV_HEAD
    y = [yf[j][0] + yb[j][0] for j in blocks]
    mu = [_mm_exact_rhs(y[j], bones) * inv for j in blocks]
    yc = [y[j] - mu[j] for j in blocks]
    var = [_mm_exact_rhs(yc[j] * yc[j], bones) * inv for j in blocks]
    rw_out = [((yc[j] * lax.rsqrt(var[j] + GN_EPS) * lxg_ref[...] + lxb_ref[...] + bonus[j][0])
               * gate[j][0]).astype(BF16) for j in blocks]
    mix = [_dg(att_ref[0, rows[j], :], wo_ref[:D_ATT, :], _NN) + _dg(rw_out[j], wo_ref[D_ATT:, :], _NN)
           for j in blocks]
    x1 = [_layer_norm(ALPHA * x_ref[0, rows[j], :] + gt_ref[0] * mix[j]) * l1g_ref[...] + l1b_ref[...]
          for j in blocks]
    h2 = [_layer_norm(x1[j]) * (1.0 + sc_ref[0]) + sh_ref[0] for j in blocks]
    logits = [_mm(rwt_ref[...], h2[j], _NT, passes=3) for j in blocks]
    for j in blocks:
        x1_ref[0, rows[j], :] = x1[j]
        h2_ref[0, rows[j], :] = h2[j].astype(BF16)
        e = jnp.exp(logits[j] - jnp.max(logits[j], axis=0, keepdims=True))
        aff_ref[0, :, rows[j]] = e / jnp.sum(e, axis=0, keepdims=True)


def _post(yf, yb, bonus, g, att, x, gt1, sh2, sc2, w_out_bf16, lnx_g, lnx_b, ln1_g, ln1_b, router_wt,
          bones, n_ctx):
    B, L, D = x.shape
    tm = min(POST_TILE, L)
    k = tm // ROW_TILE
    nct = n_ctx // ROW_TILE
    E = router_wt.shape[0]
    const = lambda b, i: (0, 0)
    cat = [pl.BlockSpec((1, ROW_TILE, D_RWKV), lambda b, i, j=j: (b, i * k + j + nct, 0)) for j in range(k)]
    mod = pl.BlockSpec((1, 1, D), lambda b, i: (b, 0, 0))
    vec = lambda n: pl.BlockSpec((1, n), const)
    return pl.pallas_call(
        functools.partial(_post_kernel, k=k),
        grid=(B, L // tm),
        in_specs=cat * 4 + [
                  pl.BlockSpec((1, tm, D_ATT), lambda b, i: (b, i, 0)),
                  pl.BlockSpec((1, tm, D), lambda b, i: (b, i, 0)),
                  mod, mod, mod,
                  pl.BlockSpec(w_out_bf16.shape, const),
                  vec(D_RWKV), vec(D_RWKV), vec(D), vec(D),
                  pl.BlockSpec((E, D), const),
                  pl.BlockSpec(bones.shape, const)],
        out_specs=[pl.BlockSpec((1, tm, D), lambda b, i: (b, i, 0)),
                   pl.BlockSpec((1, tm, D), lambda b, i: (b, i, 0)),
                   pl.BlockSpec((1, E, tm), lambda b, i: (b, 0, i))],
        out_shape=[jax.ShapeDtypeStruct((B, L, D), F32),
                   jax.ShapeDtypeStruct((B, L, D), BF16),
                   jax.ShapeDtypeStruct((B, E, L), F32)],
        compiler_params=_params(("arbitrary", "arbitrary")),
        name="post_mix",
    )(*([yf] * k + [yb] * k + [bonus] * k + [g] * k), att, x, gt1, sh2, sc2, w_out_bf16, lnx_g, lnx_b,
      ln1_g, ln1_b, router_wt, bones)


def _cumsum_lanes(x):
    n = x.shape[1]
    lane = lax.broadcasted_iota(jnp.int32, x.shape, 1)
    s = 1
    while s < n:
        x = x + jnp.where(lane >= s, pltpu.roll(x, s, 1), 0)
        s *= 2
    return x


def _route_kernel(aff_ref, pos_ref, gate_ref, cum_ref, *, cap, tile):
    aff = aff_ref[0]
    E = aff.shape[0]
    count = lambda m: jnp.sum(jnp.where(m, 1, 0), axis=1, keepdims=True)
    thr_bits = jnp.zeros((E, 1), jnp.int32)
    for bit in range(29, -1, -1):
        cand = thr_bits | (1 << bit)
        cand_f = lax.bitcast_convert_type(cand, F32)
        thr_bits = jnp.where(count(aff >= cand_f) >= cap, cand, thr_bits)
    thr = lax.bitcast_convert_type(thr_bits, F32)
    above = aff > thr
    tie = aff == thr
    need = cap - count(above)
    tie_rank = _cumsum_lanes(jnp.where(tie, 1, 0))
    sel = above | (tie & (tie_rank <= need))
    slot = _cumsum_lanes(jnp.where(sel, 1, 0)) - 1
    pos = jnp.where(sel, slot, -1)
    gate = jnp.where(sel, aff, 0.0)
    for e in range(E):
        pos_ref[0, e] = pos[e:e + 1, :]
        gate_ref[0, e] = gate[e:e + 1, :]
    lane = lax.broadcasted_iota(jnp.int32, (E, 128), 1)
    chosen = jnp.where(sel, 1, 0)
    run = jnp.zeros((E, 1), jnp.int32)
    cum = jnp.zeros((E, 128), jnp.int32)
    n_tiles = aff.shape[1] // tile
    for i in range(n_tiles):
        cum = jnp.where(lane == i, run, cum)
        run = run + jnp.sum(chosen[:, i * tile:(i + 1) * tile], axis=1, keepdims=True)
    cum_ref[0] = jnp.where(lane == n_tiles, run, cum)


def _route(aff, cap, tile):
    B, E, L = aff.shape
    out = pl.BlockSpec((1, E, 1, L), lambda b: (b, 0, 0, 0))
    return pl.pallas_call(
        functools.partial(_route_kernel, cap=cap, tile=tile),
        grid=(B,),
        in_specs=[pl.BlockSpec((1, E, L), lambda b: (b, 0, 0))],
        out_specs=[out, out, pl.BlockSpec((1, E, 128), lambda b: (b, 0, 0))],
        out_shape=[jax.ShapeDtypeStruct((B, E, 1, L), jnp.int32),
                   jax.ShapeDtypeStruct((B, E, 1, L), F32),
                   jax.ShapeDtypeStruct((B, E, 128), jnp.int32)],
        compiler_params=_params(("arbitrary",)),
        name="route",
    )(aff)


def _expert_kernel(h_ref, pos_ref, wg_ref, wu_ref, wd_ref, y_ref, *, cap):
    pos = pos_ref[0, 0]
    slot = lax.broadcasted_iota(jnp.int32, (cap, pos.shape[1]), 0)
    onehot = jnp.where(pos == slot, 1.0, 0.0).astype(BF16)
    xin = _dg(onehot, h_ref[0], _NN).astype(BF16)
    hg = _dg(xin, wg_ref[0], _NN)
    hu = _dg(xin, wu_ref[0], _NN)
    hid = (hg * _sigmoid(hg) * hu).astype(BF16)
    y_ref[0, 0] = _dg(hid, wd_ref[0], _NN).astype(BF16)


def _experts(h2, pos, wg, wu, wd, cap):
    B, L, D = h2.shape
    E, _, F = wg.shape
    rowv = pl.BlockSpec((1, 1, 1, L), lambda b, e: (b, e, 0, 0))
    return pl.pallas_call(
        functools.partial(_expert_kernel, cap=cap),
        grid=(B, E),
        in_specs=[pl.BlockSpec((1, L, D), lambda b, e: (b, 0, 0)),
                  rowv,
                  pl.BlockSpec((1, D, F), lambda b, e: (e, 0, 0)),
                  pl.BlockSpec((1, D, F), lambda b, e: (e, 0, 0)),
                  pl.BlockSpec((1, F, D), lambda b, e: (e, 0, 0))],
        out_specs=pl.BlockSpec((1, 1, cap, D), lambda b, e: (b, e, 0, 0)),
        out_shape=jax.ShapeDtypeStruct((B, E, cap, D), BF16),
        compiler_params=_params(("arbitrary", "arbitrary")),
        name="experts",
    )(h2, pos, wg, wu, wd)


def _combine_kernel(cum_ref, pos_ref, gate_ref, y_ref, x1_ref, gt_ref, lg_ref, lb_ref, o_ref, acc_ref, *, cap):
    b, i, e = pl.program_id(0), pl.program_id(1), pl.program_id(2)
    n_group = pos_ref.shape[1]

    @pl.when(e == 0)
    def _():
        acc_ref[...] = jnp.zeros_like(acc_ref)

    window = min(COMBINE_WINDOW, cap)
    align = min(128, window)

    def accumulate(j, first, width):
        pos = pos_ref[0, j]
        slot = first + lax.broadcasted_iota(jnp.int32, (width, pos.shape[1]), 0)
        weights = jnp.where(pos == slot, gate_ref[0, j], 0.0).astype(BF16)
        acc_ref[...] += _dg(weights, y_ref[0, j, pl.ds(first, width), :], _TN)

    for j in range(n_group):
        base = ((b * pl.num_programs(2) + e) * n_group + j) * 128
        lo, hi = cum_ref[base + i], cum_ref[base + i + 1]
        start = pl.multiple_of(jnp.minimum((lo // align) * align, cap - window), align)
        pl.when((hi > lo) & (hi - start <= window))(functools.partial(accumulate, j, start, window))
        pl.when(hi - start > window)(functools.partial(accumulate, j, 0, cap))

    @pl.when(e == pl.num_programs(2) - 1)
    def _():
        z = ALPHA * x1_ref[0] + gt_ref[0] * acc_ref[...]
        o_ref[0] = _layer_norm(z) * lg_ref[...] + lb_ref[...]


def _combine(cum, pos, gate, y, x1, gt2, ln2_g, ln2_b, cap):
    B, L, D = x1.shape
    E = y.shape[1]
    tl = min(COMBINE_TILE, L)
    ge = COMBINE_EXPERTS
    const = lambda b, i, e, cum: (0, 0)
    grid_spec = pltpu.PrefetchScalarGridSpec(
        num_scalar_prefetch=1,
        grid=(B, L // tl, E // ge),
        in_specs=[pl.BlockSpec((1, ge, 1, tl), lambda b, i, e, cum: (b, e, 0, i)),
                  pl.BlockSpec((1, ge, 1, tl), lambda b, i, e, cum: (b, e, 0, i)),
                  pl.BlockSpec((1, ge, cap, D), lambda b, i, e, cum: (b, e, 0, 0)),
                  pl.BlockSpec((1, tl, D), lambda b, i, e, cum: (b, i, 0)),
                  pl.BlockSpec((1, 1, D), lambda b, i, e, cum: (b, 0, 0)),
                  pl.BlockSpec((1, D), const),
                  pl.BlockSpec((1, D), const)],
        out_specs=pl.BlockSpec((1, tl, D), lambda b, i, e, cum: (b, i, 0)),
        scratch_shapes=[pltpu.VMEM((tl, D), F32)])
    return pl.pallas_call(
        functools.partial(_combine_kernel, cap=cap),
        grid_spec=grid_spec,
        out_shape=jax.ShapeDtypeStruct((B, L, D), F32),
        compiler_params=_params(("arbitrary", "arbitrary", "arbitrary")),
        name="combine",
    )(cum, pos, gate, y, x1, gt2, ln2_g, ln2_b)


def _rope_tables(n_ctx, n_lat):
    lane = jnp.arange(128)
    j = lane % HEAD_DIM
    axis = j // (2 * ROPE_PAIRS)
    upper = (j % (2 * ROPE_PAIRS)) >= ROPE_PAIRS
    inv = ROPE_THETA ** (-(j % ROPE_PAIRS).astype(F32) / ROPE_PAIRS)
    t = jnp.arange(n_lat)
    coord = jnp.where(axis[None, :] == 0, (t // GRID_W)[:, None], (t % GRID_W)[:, None]).astype(F32)
    ang = coord * inv[None, :]
    cos, sin = jnp.cos(ang), jnp.sin(ang)
    s_lo = jnp.where(upper[None, :], 0.0, -sin)
    s_hi = jnp.where(upper[None, :], sin, 0.0)
    pad = lambda a, fill: jnp.concatenate([jnp.full((n_ctx, 128), fill, F32), a], axis=0)
    return pad(cos, 1.0), pad(s_lo, 0.0), pad(s_hi, 0.0)


def _head_ones(n):
    h = jnp.arange(n) // HEAD_DIM
    return (h[:, None] == h[None, :]).astype(BF16)


def kernel(x, c, ctx, c_ctx, w_ada, b_ada, w_in, q_gain, k_gain, tshift_mu, decay_w0, decay_up, iclr_a0,
           iclr_up, gate_up, k_k, k_a, r_k, lnx_g, lnx_b, w_out, ln1_g, ln1_b, router_w, exp_w_gate,
           exp_w_up, exp_w_down, ln2_g, ln2_b):
    B, L, D = x.shape
    n_ctx = ctx.shape[1]
    li = 0
    row = lambda a: a.reshape(1, -1)

    n_rows = -(-(B + 1) // 8) * 8
    c_rows = jnp.concatenate([c, c_ctx[None, :], jnp.zeros((n_rows - B - 1, D), F32)], axis=0)
    mod = _ada_mod(c_rows, w_ada[li], b_ada[li]).reshape(n_rows, N_MOD, 1, D)
    sh1, sc1, gt1, sh2, sc2, gt2 = (mod[:, m] for m in range(N_MOD))

    bones = _head_ones(D_RWKV)
    cos, s_lo, s_hi = _rope_tables(n_ctx, L)
    q, k, v, rw = _inproj(x, ctx, sh1, sc1, w_in[li].astype(BF16), bones,
                          row(jnp.tile(q_gain[li], ATT_HEADS)), row(jnp.tile(k_gain[li], ATT_KV_HEADS)),
                          cos, s_lo, s_hi)
    Tc = n_ctx + L
    vt = v.reshape(B, ATT_KV_HEADS, Tc // ATT_K_TILE, ATT_K_TILE, HEAD_DIM).swapaxes(-1, -2)
    vt = jnp.concatenate([vt, jnp.ones(vt.shape[:3] + (16, ATT_K_TILE), BF16)], axis=3)
    att = _attention(q, k, vt)

    zpad = jnp.zeros((2, D_LORA // 2, D_RWKV), F32)
    dup = jnp.concatenate([decay_up[li], zpad], axis=1)
    aup = jnp.concatenate([zpad, iclr_up[li]], axis=1)
    scan_shared, g, bonus, scan_lw, scan_kb = _rwkv_prep(
        rw, n_ctx, row(tshift_mu[li]), decay_w0[li], dup, iclr_a0[li], aup, gate_up[li],
        row(k_k[li]), row(k_a[li]), row(r_k[li]), bones)
    yf, yb = _wkv_scan(scan_shared, scan_lw, scan_kb, n_ctx)

    x1, h2, aff = _post(yf, yb, bonus, g, att, x, gt1, sh2, sc2, w_out[li].astype(BF16),
                        row(lnx_g[li]), row(lnx_b[li]), row(ln1_g[li]), row(ln1_b[li]),
                        router_w[li].T, bones, n_ctx)
    cap = CAPACITY_FACTOR * L // N_EXPERTS
    pos, gate, cum = _route(aff, cap, min(COMBINE_TILE, L))
    y = _experts(h2, pos, exp_w_gate[li].astype(BF16), exp_w_up[li].astype(BF16),
                 exp_w_down[li].astype(BF16), cap)
    return _combine(cum.reshape(-1), pos, gate, y, x1, gt2, row(ln2_g[li]), row(ln2_b[li]), cap)
```
